```python
import jax, jax.numpy as jnp
from jax import lax
import numpy as np

D_MODEL = 2048
BATCH = 8
SEQ = 2048
DEPTH = 1

HEAD_DIM = 64
N_Q_HEADS = D_MODEL // HEAD_DIM
N_KV_HEADS = 4
GQA_GROUP = N_Q_HEADS // N_KV_HEADS
WINDOW = 128

GLA_HEADS = 4
GLA_DK = (D_MODEL // 2) // GLA_HEADS
GLA_DV = D_MODEL // GLA_HEADS
GLA_GATE_RANK = 16
GLA_GATE_NORMALIZER = 16.0
GLA_CHUNK = 64

FFN_HIDDEN = ((8 * D_MODEL // 3 + 255) // 256) * 256

RMS_EPS = 1e-6
MASK_VALUE = -1e30

IN_WIDTHS = (
    N_Q_HEADS * HEAD_DIM,
    N_KV_HEADS * HEAD_DIM,
    N_KV_HEADS * HEAD_DIM,
    GLA_HEADS * GLA_DK,
    GLA_HEADS * GLA_DK,
    GLA_HEADS * GLA_DV,
    GLA_GATE_RANK,
    GLA_HEADS * GLA_DV,
    D_MODEL,
    D_MODEL,
)
D_IN = sum(IN_WIDTHS)

kernel_name = "hybrid_swa_sink_gla_swiglu_block"


def _rmsnorm(x, w):
    xf = x.astype(jnp.float32)
    y = xf * lax.rsqrt(jnp.mean(xf * xf, axis=-1, keepdims=True) + RMS_EPS)
    return (y * w.astype(jnp.float32)).astype(x.dtype)


def _split_offsets():
    offs, acc = [], 0
    for w in IN_WIDTHS[:-1]:
        acc += w
        offs.append(acc)
    return offs


def _swa_sink_attention(q, k, v, sinks):
    B, T = q.shape[0], q.shape[1]
    nb = T // WINDOW
    qb = q.reshape(B, nb, WINDOW, N_KV_HEADS, GQA_GROUP, HEAD_DIM)
    kb = k.reshape(B, nb, WINDOW, N_KV_HEADS, HEAD_DIM)
    vb = v.reshape(B, nb, WINDOW, N_KV_HEADS, HEAD_DIM)
    k_prev = jnp.concatenate([jnp.zeros_like(kb[:, :1]), kb[:, :-1]], axis=1)
    v_prev = jnp.concatenate([jnp.zeros_like(vb[:, :1]), vb[:, :-1]], axis=1)
    kk = jnp.concatenate([k_prev, kb], axis=2)
    vv = jnp.concatenate([v_prev, vb], axis=2)
    s = jnp.einsum('bnqhgd,bnkhd->bhgnqk', qb, kk).astype(jnp.float32) * (HEAD_DIM ** -0.5)
    qi = jnp.arange(WINDOW)[:, None]
    ki = jnp.arange(2 * WINDOW)[None, :]
    rel = qi + WINDOW - ki
    band = (rel >= 0) & (rel < WINDOW)
    blk = jnp.arange(nb)[:, None, None]
    mask = band[None] & ((blk > 0) | (ki[None] >= WINDOW))
    s = jnp.where(mask, s, MASK_VALUE)
    sink = sinks.astype(jnp.float32).reshape(N_KV_HEADS, GQA_GROUP)[None, :, :, None, None, None]
    sink = jnp.broadcast_to(sink, s.shape[:-1] + (1,))
    p = jax.nn.softmax(jnp.concatenate([s, sink], axis=-1), axis=-1)[..., :-1]
    o = jnp.einsum('bhgnqk,bnkhd->bnqhgd', p.astype(vv.dtype), vv)
    return o.reshape(B, T, N_Q_HEADS * HEAD_DIM)


def _gla(q, k, v, log_a):
    B, T = q.shape[0], q.shape[1]
    nc = T // GLA_CHUNK

    def chunk(t, d):
        return t.astype(jnp.float32).reshape(B, nc, GLA_CHUNK, GLA_HEADS, d).transpose(0, 3, 1, 2, 4)

    qc = chunk(q, GLA_DK) * (GLA_DK ** -0.5)
    kc = chunk(k, GLA_DK)
    vc = chunk(v, GLA_DV)
    g = jnp.cumsum(chunk(log_a, GLA_DK), axis=3)
    g_last = g[..., -1:, :]
    q_dec = qc * jnp.exp(g)
    k_inv = kc * jnp.exp(-g)
    k_to_end = kc * jnp.exp(g_last - g)
    causal = jnp.tril(jnp.ones((GLA_CHUNK, GLA_CHUNK), dtype=bool))
    att = jnp.where(causal, jnp.einsum('bhnid,bhnjd->bhnij', q_dec, k_inv), 0.0)
    o_intra = jnp.einsum('bhnij,bhnjv->bhniv', att, vc)
    upd = jnp.einsum('bhnjd,bhnjv->bhndv', k_to_end, vc)
    decay = jnp.exp(g_last[..., 0, :])

    def step(state, inp):
        d_c, u_c = inp
        return d_c[..., None] * state + u_c, state

    s0 = jnp.zeros((B, GLA_HEADS, GLA_DK, GLA_DV), jnp.float32)
    _, s_prev = lax.scan(step, s0, (jnp.moveaxis(decay, 2, 0), jnp.moveaxis(upd, 2, 0)))
    s_prev = jnp.moveaxis(s_prev, 0, 2)
    o_inter = jnp.einsum('bhnid,bhndv->bhniv', q_dec, s_prev)
    o = o_intra + o_inter
    return o.transpose(0, 2, 3, 1, 4).reshape(B, T, GLA_HEADS, GLA_DV)


def setup_inputs(seed: int = 0) -> dict:
    key = jax.random.key(seed)
    ks = jax.random.split(key, 14)
    f32 = jnp.float32

    def normal(k, shape, scale):
        return jax.random.normal(k, shape, f32) * scale

    return {
        "x": jax.random.normal(ks[0], (BATCH, SEQ, D_MODEL), f32),
        "norm1_w": 1.0 + normal(ks[1], (DEPTH, D_MODEL), 0.02),
        "w_in": normal(ks[2], (DEPTH, D_MODEL, D_IN), D_MODEL ** -0.5),
        "gla_gate_w2": normal(ks[3], (DEPTH, GLA_GATE_RANK, GLA_HEADS * GLA_DK), GLA_GATE_RANK ** -0.5),
        "gla_gate_b": normal(ks[4], (DEPTH, GLA_HEADS * GLA_DK), 0.02),
        "attn_sinks": normal(ks[5], (DEPTH, N_Q_HEADS), 0.5),
        "gla_norm_w": 1.0 + normal(ks[6], (DEPTH, GLA_DV), 0.02),
        "w_out": normal(ks[7], (DEPTH, D_MODEL, D_MODEL), D_MODEL ** -0.5),
        "norm2_w": 1.0 + normal(ks[8], (DEPTH, D_MODEL), 0.02),
        "w_ffn_gate": normal(ks[9], (DEPTH, D_MODEL, FFN_HIDDEN), D_MODEL ** -0.5),
        "w_ffn_up": normal(ks[10], (DEPTH, D_MODEL, FFN_HIDDEN), D_MODEL ** -0.5),
        "w_ffn_down": normal(ks[11], (DEPTH, FFN_HIDDEN, D_MODEL), FFN_HIDDEN ** -0.5),
        "final_norm_w": 1.0 + normal(ks[12], (D_MODEL,), 0.02),
    }


def reference(x, norm1_w, w_in, gla_gate_w2, gla_gate_b, attn_sinks, gla_norm_w, w_out,
              norm2_w, w_ffn_gate, w_ffn_up, w_ffn_down, final_norm_w):
    B, T, _ = x.shape
    offs = _split_offsets()
    h = x
    for l in range(DEPTH):
        u = _rmsnorm(h, norm1_w[l])
        proj = u @ w_in[l]
        aq, ak, av, gq, gk, gv, g_lr, g_r, gate_a, gate_b = jnp.split(proj, offs, axis=-1)

        attn_o = _swa_sink_attention(aq, ak, av, attn_sinks[l])

        gate_logit = (g_lr @ gla_gate_w2[l] + gla_gate_b[l]).astype(jnp.float32)
        log_a = jax.nn.log_sigmoid(gate_logit) / GLA_GATE_NORMALIZER
        gla_o = _gla(gq.reshape(B, T, GLA_HEADS, GLA_DK),
                     gk.reshape(B, T, GLA_HEADS, GLA_DK),
                     gv.reshape(B, T, GLA_HEADS, GLA_DV),
                     log_a.reshape(B, T, GLA_HEADS, GLA_DK))
        gla_o = _rmsnorm(gla_o, gla_norm_w[l]).astype(x.dtype)
        gla_o = gla_o.reshape(B, T, GLA_HEADS * GLA_DV) * jax.nn.silu(g_r)

        merged = jax.nn.sigmoid(gate_a) * attn_o + jax.nn.sigmoid(gate_b) * gla_o
        h = h + merged @ w_out[l]

        v2 = _rmsnorm(h, norm2_w[l])
        ff = jax.nn.silu(v2 @ w_ffn_gate[l]) * (v2 @ w_ffn_up[l])
        h = h + ff @ w_ffn_down[l]
    return _rmsnorm(h, final_norm_w)
```

```python
import functools

import jax
import jax.numpy as jnp
from jax import lax
from jax.experimental import pallas as pl
from jax.experimental.pallas import tpu as pltpu

F32 = jnp.float32
BF16 = jnp.bfloat16

D_MODEL = 2048
HEAD_DIM = 64
N_Q_HEADS = D_MODEL // HEAD_DIM
N_KV_HEADS = 4
GQA_GROUP = N_Q_HEADS // N_KV_HEADS
KV_WIDTH = N_KV_HEADS * HEAD_DIM
WINDOW = 128

GLA_HEADS = 4
GLA_DK = (D_MODEL // 2) // GLA_HEADS
GLA_DV = D_MODEL // GLA_HEADS
GLA_GATE_RANK = 16
GLA_GATE_NORMALIZER = 16.0
GLA_CHUNK = 64

FFN_HIDDEN = ((8 * D_MODEL // 3 + 255) // 256) * 256
RMS_EPS = 1e-6
MASK_VALUE = -1e30

LANES = 128

COL_AQ = 0
COL_GV = COL_AQ + D_MODEL
COL_GR = COL_GV + D_MODEL
COL_GATE_A = COL_GR + D_MODEL
COL_GATE_B = COL_GATE_A + D_MODEL
COL_GQ = COL_GATE_B + D_MODEL
COL_GK = COL_GQ + GLA_HEADS * GLA_DK
COL_AK = COL_GK + GLA_HEADS * GLA_DK
COL_AV = COL_AK + KV_WIDTH
PROJ_WIDTH = COL_AV + KV_WIDTH

VMEM_LIMIT = 56 * 1024 * 1024

_NT = (((1,), (1,)), ((), ()))
_TN = (((0,), (0,)), ((), ()))


def _rms_scale(x):
    return lax.rsqrt(jnp.mean(x * x, axis=-1, keepdims=True) + RMS_EPS)


def _inproj_kernel(x_ref, nw_ref, w_ref, wlr_ref, proj_ref, lr_ref, u_ref):
    @pl.when(pl.program_id(1) == 0)
    def _():
        x = x_ref[...]
        u = (x * _rms_scale(x) * nw_ref[...]).astype(BF16)
        u_ref[...] = u
        lr_ref[...] = jnp.dot(u, wlr_ref[...], preferred_element_type=F32)

    proj_ref[...] = jnp.dot(u_ref[...], w_ref[...], preferred_element_type=F32).astype(BF16)


def _inproj(x2, norm_w, w_main, w_lr, *, tm, tn):
    n = x2.shape[0]
    return pl.pallas_call(
        _inproj_kernel,
        grid=(n // tm, PROJ_WIDTH // tn),
        in_specs=[
            pl.BlockSpec((tm, D_MODEL), lambda i, j: (i, 0)),
            pl.BlockSpec((1, D_MODEL), lambda i, j: (0, 0)),
            pl.BlockSpec((D_MODEL, tn), lambda i, j: (0, j)),
            pl.BlockSpec((D_MODEL, LANES), lambda i, j: (0, 0)),
        ],
        out_specs=[
            pl.BlockSpec((tm, tn), lambda i, j: (i, j)),
            pl.BlockSpec((tm, LANES), lambda i, j: (i, 0)),
        ],
        out_shape=[
            jax.ShapeDtypeStruct((n, PROJ_WIDTH), BF16),
            jax.ShapeDtypeStruct((n, LANES), F32),
        ],
        scratch_shapes=[pltpu.VMEM((tm, D_MODEL), BF16)],
        compiler_params=pltpu.CompilerParams(
            dimension_semantics=("arbitrary", "arbitrary"), vmem_limit_bytes=VMEM_LIMIT),
        name="inproj",
    )(x2, norm_w, w_main, w_lr)


def _attn_kernel(sink_ref, q_ref, k_ref, v_ref, ga_ref, o_ref, acc_ref, *, tq):
    t = pl.program_id(1)
    for blk in range(tq // WINDOW):
        q_start = t * tq + blk * WINDOW
        k_start = pl.multiple_of(jnp.maximum(q_start - WINDOW, 0), WINDOW)
        rows = pl.ds(blk * WINDOW, WINDOW)
        q = q_ref[0, rows, :] * (HEAD_DIM ** -0.5)
        kk = k_ref[0, pl.ds(k_start, 2 * WINDOW), :]
        vv = v_ref[0, pl.ds(k_start, 2 * WINDOW), :]
        qi = lax.broadcasted_iota(jnp.int32, (WINDOW, 2 * WINDOW), 0)
        ki = lax.broadcasted_iota(jnp.int32, (WINDOW, 2 * WINDOW), 1)
        rel = (q_start - k_start) + qi - ki
        mask = (rel >= 0) & (rel < WINDOW)
        for g in range(N_KV_HEADS):
            kg = kk[:, g * HEAD_DIM:(g + 1) * HEAD_DIM]
            vg = vv[:, g * HEAD_DIM:(g + 1) * HEAD_DIM]
            for hh in range(GQA_GROUP):
                h = g * GQA_GROUP + hh
                sink = sink_ref[h]
                s = lax.dot_general(q[:, h * HEAD_DIM:(h + 1) * HEAD_DIM], kg, _NT,
                                    preferred_element_type=F32)
                s = jnp.where(mask, s, MASK_VALUE)
                m = jnp.maximum(jnp.max(s, axis=-1, keepdims=True), sink)
                p = jnp.exp(s - m)
                denom = jnp.sum(p, axis=-1, keepdims=True) + jnp.exp(sink - m)
                o = jnp.dot(p.astype(BF16), vg, preferred_element_type=F32)
                acc_ref[:, h * HEAD_DIM:(h + 1) * HEAD_DIM] = o / denom
        gate = jax.nn.sigmoid(ga_ref[0, rows, :].astype(F32))
        o_ref[0, rows, :] = (gate * acc_ref[...]).astype(BF16)


def _attention(proj3, sinks, *, tq):
    b, t, _ = proj3.shape
    return pl.pallas_call(
        functools.partial(_attn_kernel, tq=tq),
        grid=(b, t // tq),
        in_specs=[
            pl.BlockSpec(memory_space=pltpu.SMEM),
            pl.BlockSpec((1, tq, D_MODEL), lambda i, j: (i, j, COL_AQ // D_MODEL)),
            pl.BlockSpec((1, t, KV_WIDTH), lambda i, j: (i, 0, COL_AK // KV_WIDTH)),
            pl.BlockSpec((1, t, KV_WIDTH), lambda i, j: (i, 0, COL_AV // KV_WIDTH)),
            pl.BlockSpec((1, tq, D_MODEL), lambda i, j: (i, j, COL_GATE_A // D_MODEL)),
        ],
        out_specs=pl.BlockSpec((1, tq, D_MODEL), lambda i, j: (i, j, 0)),
        out_shape=jax.ShapeDtypeStruct((b, t, D_MODEL), BF16),
        scratch_shapes=[pltpu.VMEM((WINDOW, D_MODEL), F32)],
        compiler_params=pltpu.CompilerParams(
            dimension_semantics=("arbitrary", "arbitrary"), vmem_limit_bytes=VMEM_LIMIT),
        name="swa_attention",
    )(sinks, proj3, proj3, proj3, proj3)


def _log_sigmoid(x):
    return jnp.minimum(x, 0.0) - jnp.log1p(jnp.exp(-jnp.abs(x)))


def _gla_kernel(q_ref, k_ref, v_ref, gr_ref, gb_ref, lr_ref, w2_ref, w2t_ref, b_ref, bt_ref, nw_ref,
                o_ref, s_ref, *, tb):
    @pl.when(pl.program_id(2) == 0)
    def _():
        s_ref[...] = jnp.zeros_like(s_ref)

    lr = lr_ref[0].astype(BF16)
    logit = jnp.dot(lr, w2_ref[...], preferred_element_type=F32) + b_ref[...]
    log_a = _log_sigmoid(logit) * (1.0 / GLA_GATE_NORMALIZER)
    logit_t = lax.dot_general(w2t_ref[...], lr, _NT, preferred_element_type=F32) + bt_ref[...]
    log_a_t = _log_sigmoid(logit_t) * (1.0 / GLA_GATE_NORMALIZER)

    ri = lax.broadcasted_iota(jnp.int32, (GLA_CHUNK, GLA_CHUNK), 0)
    ci = lax.broadcasted_iota(jnp.int32, (GLA_CHUNK, GLA_CHUNK), 1)
    causal = ri >= ci
    tri = causal.astype(BF16)

    for c in range(tb // GLA_CHUNK):
        rows = pl.ds(c * GLA_CHUNK, GLA_CHUNK)
        la = log_a[c * GLA_CHUNK:(c + 1) * GLA_CHUNK, :]
        la_hi = la.astype(BF16)
        la_lo = (la - la_hi.astype(F32)).astype(BF16)
        g = (jnp.dot(tri, la_hi, preferred_element_type=F32)
             + jnp.dot(tri, la_lo, preferred_element_type=F32))
        g_last = g[GLA_CHUNK - 1:GLA_CHUNK, :]
        q = q_ref[0, rows, :].astype(F32)
        k = k_ref[0, rows, :].astype(F32)
        v = v_ref[0, rows, :]
        q_dec = (q * (GLA_DK ** -0.5) * jnp.exp(g)).astype(BF16)
        k_inv = (k * jnp.exp(-g)).astype(BF16)
        k_end = (k * jnp.exp(g_last - g)).astype(BF16)
        att = lax.dot_general(q_dec, k_inv, _NT, preferred_element_type=F32)
        att = jnp.where(causal, att, 0.0).astype(BF16)
        state = s_ref[...]
        o = (jnp.dot(att, v, preferred_element_type=F32)
             + jnp.dot(q_dec, state.astype(BF16), preferred_element_type=F32))
        upd = lax.dot_general(k_end, v, _TN, preferred_element_type=F32)
        decay = jnp.exp(jnp.sum(log_a_t[:, c * GLA_CHUNK:(c + 1) * GLA_CHUNK], axis=-1, keepdims=True))
        s_ref[...] = decay * state + upd

        o = o * _rms_scale(o) * nw_ref[...]
        gr = gr_ref[0, rows, :].astype(F32)
        gb = gb_ref[0, rows, :].astype(F32)
        o = o * (gr * jax.nn.sigmoid(gr)) * jax.nn.sigmoid(gb)
        o_ref[0, rows, :] = o.astype(BF16)


def _gla(proj3, lr3, w2, w2t, bias, bias_t, norm_w, *, tb):
    b, t, _ = proj3.shape
    return pl.pallas_call(
        functools.partial(_gla_kernel, tb=tb),
        grid=(b, GLA_HEADS, t // tb),
        in_specs=[
            pl.BlockSpec((1, tb, GLA_DK), lambda i, h, j: (i, j, COL_GQ // GLA_DK + h)),
            pl.BlockSpec((1, tb, GLA_DK), lambda i, h, j: (i, j, COL_GK // GLA_DK + h)),
            pl.BlockSpec((1, tb, GLA_DV), lambda i, h, j: (i, j, COL_GV // GLA_DV + h)),
            pl.BlockSpec((1, tb, GLA_DV), lambda i, h, j: (i, j, COL_GR // GLA_DV + h)),
            pl.BlockSpec((1, tb, GLA_DV), lambda i, h, j: (i, j, COL_GATE_B // GLA_DV + h)),
            pl.BlockSpec((1, tb, LANES), lambda i, h, j: (i, j, 0)),
            pl.BlockSpec((LANES, GLA_DK), lambda i, h, j: (0, h)),
            pl.BlockSpec((GLA_DK, LANES), lambda i, h, j: (h, 0)),
            pl.BlockSpec((1, GLA_DK), lambda i, h, j: (0, h)),
            pl.BlockSpec((GLA_DK, 1), lambda i, h, j: (h, 0)),
            pl.BlockSpec((1, GLA_DV), lambda i, h, j: (0, 0)),
        ],
        out_specs=pl.BlockSpec((1, tb, GLA_DV), lambda i, h, j: (i, j, h)),
        out_shape=jax.ShapeDtypeStruct((b, t, D_MODEL), BF16),
        scratch_shapes=[pltpu.VMEM((GLA_DK, GLA_DV), F32)],
        compiler_params=pltpu.CompilerParams(
            dimension_semantics=("arbitrary", "arbitrary", "arbitrary"), vmem_limit_bytes=VMEM_LIMIT),
        name="gla",
    )(proj3, proj3, proj3, proj3, proj3, lr3, w2, w2t, bias, bias_t, norm_w)


def _outproj_kernel(a_ref, g_ref, x_ref, w_ref, h_ref):
    merged = a_ref[...] + g_ref[...]
    h_ref[...] = x_ref[...] + jnp.dot(merged, w_ref[...], preferred_element_type=F32)


def _outproj(a2, g2, x2, w_out, *, tm):
    n = x2.shape[0]
    return pl.pallas_call(
        _outproj_kernel,
        grid=(n // tm,),
        in_specs=[
            pl.BlockSpec((tm, D_MODEL), lambda i: (i, 0)),
            pl.BlockSpec((tm, D_MODEL), lambda i: (i, 0)),
            pl.BlockSpec((tm, D_MODEL), lambda i: (i, 0)),
            pl.BlockSpec((D_MODEL, D_MODEL), lambda i: (0, 0)),
        ],
        out_specs=pl.BlockSpec((tm, D_MODEL), lambda i: (i, 0)),
        out_shape=jax.ShapeDtypeStruct((n, D_MODEL), F32),
        compiler_params=pltpu.CompilerParams(
            dimension_semantics=("arbitrary",), vmem_limit_bytes=VMEM_LIMIT),
        name="outproj",
    )(a2, g2, x2, w_out)


def _ffn_kernel(h_ref, nw_ref, wg_ref, wu_ref, wd_ref, fw_ref, o_ref, v_ref, *, final_norm):
    j = pl.program_id(1)

    @pl.when(j == 0)
    def _():
        h = h_ref[...]
        v_ref[...] = (h * _rms_scale(h) * nw_ref[...]).astype(BF16)

    v = v_ref[...]
    gate = jnp.dot(v, wg_ref[...], preferred_element_type=F32)
    up = jnp.dot(v, wu_ref[...], preferred_element_type=F32)
    act = (gate * jax.nn.sigmoid(gate) * up).astype(BF16)
    part = jnp.dot(act, wd_ref[...], preferred_element_type=F32)

    @pl.when(j == 0)
    def _():
        o_ref[...] = part

    @pl.when(j > 0)
    def _():
        o_ref[...] += part

    @pl.when(j == pl.num_programs(1) - 1)
    def _():
        y = h_ref[...] + o_ref[...]
        o_ref[...] = y * _rms_scale(y) * fw_ref[...] if final_norm else y


def _ffn(h2, norm_w, w_gate, w_up, w_down, final_w, *, tm, th, final_norm):
    n = h2.shape[0]
    return pl.pallas_call(
        functools.partial(_ffn_kernel, final_norm=final_norm),
        grid=(n // tm, FFN_HIDDEN // th),
        in_specs=[
            pl.BlockSpec((tm, D_MODEL), lambda i, j: (i, 0)),
            pl.BlockSpec((1, D_MODEL), lambda i, j: (0, 0)),
            pl.BlockSpec((D_MODEL, th), lambda i, j: (0, j)),
            pl.BlockSpec((D_MODEL, th), lambda i, j: (0, j)),
            pl.BlockSpec((th, D_MODEL), lambda i, j: (j, 0)),
            pl.BlockSpec((1, D_MODEL), lambda i, j: (0, 0)),
        ],
        out_specs=pl.BlockSpec((tm, D_MODEL), lambda i, j: (i, 0)),
        out_shape=jax.ShapeDtypeStruct((n, D_MODEL), F32),
        scratch_shapes=[pltpu.VMEM((tm, D_MODEL), BF16)],
        compiler_params=pltpu.CompilerParams(
            dimension_semantics=("arbitrary", "arbitrary"), vmem_limit_bytes=VMEM_LIMIT),
        name="ffn",
    )(h2, norm_w, w_gate, w_up, w_down, final_w)


def _split_w_in(w):
    widths = (D_MODEL, KV_WIDTH, KV_WIDTH, GLA_HEADS * GLA_DK, GLA_HEADS * GLA_DK, D_MODEL,
              GLA_GATE_RANK, D_MODEL, D_MODEL, D_MODEL)
    parts, start = [], 0
    for width in widths:
        parts.append(w[:, start:start + width])
        start += width
    aq, ak, av, gq, gk, gv, g_lr, g_r, gate_a, gate_b = parts
    main = jnp.concatenate([aq, gv, g_r, gate_a, gate_b, gq, gk, ak, av], axis=1).astype(BF16)
    lr = jnp.pad(g_lr, ((0, 0), (0, LANES - GLA_GATE_RANK))).astype(BF16)
    return main, lr


def _tile(n, pref):
    return pref if n % pref == 0 else n


def kernel(x, norm1_w, w_in, gla_gate_w2, gla_gate_b, attn_sinks, gla_norm_w, w_out, norm2_w,
           w_ffn_gate, w_ffn_up, w_ffn_down, final_norm_w):
    b, t, d = x.shape
    n = b * t
    depth = w_in.shape[0]
    h2 = x.reshape(n, d)
    for l in range(depth):
        w_main, w_lr = _split_w_in(w_in[l])
        proj, lr = _inproj(h2, norm1_w[l].reshape(1, d), w_main, w_lr, tm=_tile(n, 1024), tn=1280)
        proj3 = proj.reshape(b, t, PROJ_WIDTH)
        lr3 = lr.reshape(b, t, LANES)

        a = _attention(proj3, attn_sinks[l], tq=_tile(t, 512))

        w2 = jnp.pad(gla_gate_w2[l], ((0, LANES - GLA_GATE_RANK), (0, 0))).astype(BF16)
        gb = gla_gate_b[l]
        g = _gla(proj3, lr3, w2, w2.T, gb.reshape(1, -1), gb.reshape(-1, 1),
                 gla_norm_w[l].reshape(1, GLA_DV), tb=_tile(t, 512))

        h2 = _outproj(a.reshape(n, d), g.reshape(n, d), h2, w_out[l].astype(BF16), tm=_tile(n, 512))
        h2 = _ffn(h2, norm2_w[l].reshape(1, d), w_ffn_gate[l].astype(BF16), w_ffn_up[l].astype(BF16),
                  w_ffn_down[l].astype(BF16), final_norm_w.reshape(1, d), tm=_tile(n, 512), th=512,
                  final_norm=l == depth - 1)
    return h2.reshape(b, t, d)
```

```python
import functools

import jax
import jax.numpy as jnp
from jax import lax
from jax.experimental import pallas as pl
from jax.experimental.pallas import tpu as pltpu

F32 = jnp.float32
BF16 = jnp.bfloat16

D_MODEL = 2048
HEAD_DIM = 64
N_Q_HEADS = D_MODEL // HEAD_DIM
N_KV_HEADS = 4
GQA_GROUP = N_Q_HEADS // N_KV_HEADS
KV_WIDTH = N_KV_HEADS * HEAD_DIM
WINDOW = 128

GLA_HEADS = 4
GLA_DK = (D_MODEL // 2) // GLA_HEADS
GLA_DV = D_MODEL // GLA_HEADS
GLA_GATE_RANK = 16
GLA_GATE_NORMALIZER = 16.0
GLA_CHUNK = 64

FFN_HIDDEN = ((8 * D_MODEL // 3 + 255) // 256) * 256
RMS_EPS = 1e-6
MASK_VALUE = -1e30

LANES = 128

COL_AQ = 0
COL_GV = COL_AQ + D_MODEL
COL_GR = COL_GV + D_MODEL
COL_GATE_A = COL_GR + D_MODEL
COL_GATE_B = COL_GATE_A + D_MODEL
COL_GQ = COL_GATE_B + D_MODEL
COL_GK = COL_GQ + GLA_HEADS * GLA_DK
COL_AK = COL_GK + GLA_HEADS * GLA_DK
COL_AV = COL_AK + KV_WIDTH
PROJ_WIDTH = COL_AV + KV_WIDTH

VMEM_LIMIT = 56 * 1024 * 1024

_NT = (((1,), (1,)), ((), ()))
_TN = (((0,), (0,)), ((), ()))


def _rms_scale(x):
    return lax.rsqrt(jnp.mean(x * x, axis=-1, keepdims=True) + RMS_EPS)


def _inproj_kernel(x_ref, nw_ref, w_ref, wlr_ref, proj_ref, lr_ref, u_ref):
    @pl.when(pl.program_id(1) == 0)
    def _():
        x = x_ref[...]
        u = (x * _rms_scale(x) * nw_ref[...]).astype(BF16)
        u_ref[...] = u
        lr_ref[...] = jnp.dot(u, wlr_ref[...], preferred_element_type=F32)

    proj_ref[...] = jnp.dot(u_ref[...], w_ref[...], preferred_element_type=F32).astype(BF16)


def _inproj(x2, norm_w, w_main, w_lr, *, tm, tn):
    n = x2.shape[0]
    return pl.pallas_call(
        _inproj_kernel,
        grid=(n // tm, PROJ_WIDTH // tn),
        in_specs=[
            pl.BlockSpec((tm, D_MODEL), lambda i, j: (i, 0)),
            pl.BlockSpec((1, D_MODEL), lambda i, j: (0, 0)),
            pl.BlockSpec((D_MODEL, tn), lambda i, j: (0, j)),
            pl.BlockSpec((D_MODEL, LANES), lambda i, j: (0, 0)),
        ],
        out_specs=[
            pl.BlockSpec((tm, tn), lambda i, j: (i, j)),
            pl.BlockSpec((tm, LANES), lambda i, j: (i, 0)),
        ],
        out_shape=[
            jax.ShapeDtypeStruct((n, PROJ_WIDTH), BF16),
            jax.ShapeDtypeStruct((n, LANES), F32),
        ],
        scratch_shapes=[pltpu.VMEM((tm, D_MODEL), BF16)],
        compiler_params=pltpu.CompilerParams(
            dimension_semantics=("arbitrary", "arbitrary"), vmem_limit_bytes=VMEM_LIMIT),
        name="inproj",
    )(x2, norm_w, w_main, w_lr)


PAIR = 2 * HEAD_DIM
PAIRS_PER_GROUP = GQA_GROUP // 2
STACK = PAIRS_PER_GROUP * WINDOW


def _attn_qblock(sink_ref, q_ref, k_ref, v_ref, ga_ref, o_ref, row0, k_row0, has_prev):
    rows = pl.ds(row0, WINDOW)
    nk = 2 * WINDOW if has_prev else WINDOW
    krows = pl.ds(k_row0, nk)
    lo = lax.broadcasted_iota(jnp.int32, (1, PAIR), 1) < HEAD_DIM
    top = lax.broadcasted_iota(jnp.int32, (PAIR, 1), 0) < HEAD_DIM
    ki = lax.broadcasted_iota(jnp.int32, (WINDOW, STACK), 0)
    qi = lax.broadcasted_iota(jnp.int32, (WINDOW, STACK), 1) & (WINDOW - 1)
    upper = ki > qi

    def halves(pair_tile, g):
        swapped = pltpu.roll(pair_tile, HEAD_DIM, axis=1)
        own, other = (pair_tile, swapped) if g % 2 == 0 else (swapped, pair_tile)
        zero = jnp.zeros_like(pair_tile)
        return jnp.where(lo, own, zero), jnp.where(lo, zero, other)

    def probs(st, sink_row):
        if has_prev:
            folded = jnp.where(upper, st[:WINDOW], st[WINDOW:])
        else:
            folded = jnp.where(upper, MASK_VALUE, st)
        m = jnp.maximum(jnp.max(folded, axis=0, keepdims=True), sink_row)
        p = jnp.exp(folded - m)
        inv = 1.0 / (jnp.sum(p, axis=0, keepdims=True) + jnp.exp(sink_row - m))
        if has_prev:
            p = jnp.concatenate([jnp.where(upper, p, 0.0), jnp.where(upper, 0.0, p)], axis=0)
        return p.astype(BF16), inv

    for g in range(N_KV_HEADS):
        kv_lanes = slice((g // 2) * PAIR, (g // 2 + 1) * PAIR)
        k_lo, k_hi = halves(k_ref[0, krows, kv_lanes], g)
        v_lo, v_hi = halves(v_ref[0, krows, kv_lanes], g)
        first_pair = g * PAIRS_PER_GROUP
        q = jnp.concatenate(
            [q_ref[0, rows, (first_pair + p) * PAIR:(first_pair + p + 1) * PAIR]
             for p in range(PAIRS_PER_GROUP)], axis=0) * (HEAD_DIM ** -0.5)
        sink_even = jnp.concatenate(
            [jnp.full((1, WINDOW), sink_ref[2 * (first_pair + p)], F32) for p in range(PAIRS_PER_GROUP)], axis=1)
        sink_odd = jnp.concatenate(
            [jnp.full((1, WINDOW), sink_ref[2 * (first_pair + p) + 1], F32) for p in range(PAIRS_PER_GROUP)],
            axis=1)
        p_even, inv_even = probs(lax.dot_general(k_lo, q, _NT, preferred_element_type=F32), sink_even)
        p_odd, inv_odd = probs(lax.dot_general(k_hi, q, _NT, preferred_element_type=F32), sink_odd)
        ot = (lax.dot_general(v_lo, p_even, _TN, preferred_element_type=F32)
              + lax.dot_general(v_hi, p_odd, _TN, preferred_element_type=F32))
        o = (ot * jnp.where(top, inv_even, inv_odd)).T
        for p in range(PAIRS_PER_GROUP):
            cols = slice((first_pair + p) * PAIR, (first_pair + p + 1) * PAIR)
            gate = jax.nn.sigmoid(ga_ref[0, rows, cols].astype(F32))
            o_ref[0, rows, cols] = (gate * o[p * WINDOW:(p + 1) * WINDOW]).astype(BF16)


def _attn_kernel(sink_ref, q_ref, k_ref, v_ref, ga_ref, o_ref, *, tq):
    t = pl.program_id(1)
    block = functools.partial(_attn_qblock, sink_ref, q_ref, k_ref, v_ref, ga_ref, o_ref)

    @pl.when(t == 0)
    def _():
        block(0, 0, has_prev=False)

    @pl.when(t > 0)
    def _():
        block(0, pl.multiple_of(t * tq - WINDOW, WINDOW), has_prev=True)

    def body(blk, carry):
        row0 = pl.multiple_of(blk * WINDOW, WINDOW)
        block(row0, pl.multiple_of(t * tq + row0 - WINDOW, WINDOW), has_prev=True)
        return carry

    lax.fori_loop(1, tq // WINDOW, body, 0)


def _attention(proj3, sinks, *, tq):
    b, t, _ = proj3.shape
    return pl.pallas_call(
        functools.partial(_attn_kernel, tq=tq),
        grid=(b, t // tq),
        in_specs=[
            pl.BlockSpec(memory_space=pltpu.SMEM),
            pl.BlockSpec((1, tq, D_MODEL), lambda i, j: (i, j, COL_AQ // D_MODEL)),
            pl.BlockSpec((1, t, KV_WIDTH), lambda i, j: (i, 0, COL_AK // KV_WIDTH)),
            pl.BlockSpec((1, t, KV_WIDTH), lambda i, j: (i, 0, COL_AV // KV_WIDTH)),
            pl.BlockSpec((1, tq, D_MODEL), lambda i, j: (i, j, COL_GATE_A // D_MODEL)),
        ],
        out_specs=pl.BlockSpec((1, tq, D_MODEL), lambda i, j: (i, j, 0)),
        out_shape=jax.ShapeDtypeStruct((b, t, D_MODEL), BF16),
        compiler_params=pltpu.CompilerParams(
            dimension_semantics=("arbitrary", "arbitrary"), vmem_limit_bytes=VMEM_LIMIT),
        name="swa_attention",
    )(sinks, proj3, proj3, proj3, proj3)


def _log_sigmoid(x):
    return jnp.minimum(x, 0.0) - jnp.log(1.0 + jnp.exp(-jnp.abs(x)))


def _gla_kernel(q_ref, k_ref, v_ref, gr_ref, gb_ref, lr_ref, w2_ref, w2t_ref, b_ref, bt_ref, nw_ref,
                o_ref, s_ref, *, tb):
    @pl.when(pl.program_id(2) == 0)
    def _():
        s_ref[...] = jnp.zeros_like(s_ref)

    lr = lr_ref[0].astype(BF16)
    logit = jnp.dot(lr, w2_ref[...], preferred_element_type=F32) + b_ref[...]
    log_a = _log_sigmoid(logit) * (1.0 / GLA_GATE_NORMALIZER)
    logit_t = lax.dot_general(w2t_ref[...], lr, _NT, preferred_element_type=F32) + bt_ref[...]
    log_a_t = _log_sigmoid(logit_t) * (1.0 / GLA_GATE_NORMALIZER)

    ri = lax.broadcasted_iota(jnp.int32, (GLA_CHUNK, GLA_CHUNK), 0)
    ci = lax.broadcasted_iota(jnp.int32, (GLA_CHUNK, GLA_CHUNK), 1)
    causal = ri >= ci
    tri = causal.astype(BF16)

    for c in range(tb // GLA_CHUNK):
        rows = pl.ds(c * GLA_CHUNK, GLA_CHUNK)
        la = log_a[c * GLA_CHUNK:(c + 1) * GLA_CHUNK, :]
        la_hi = la.astype(BF16)
        la_lo = (la - la_hi.astype(F32)).astype(BF16)
        g = (jnp.dot(tri, la_hi, preferred_element_type=F32)
             + jnp.dot(tri, la_lo, preferred_element_type=F32))
        g_last = g[GLA_CHUNK - 1:GLA_CHUNK, :]
        q = q_ref[0, rows, :].astype(F32)
        k = k_ref[0, rows, :].astype(F32)
        v = v_ref[0, rows, :]
        q_dec = (q * (GLA_DK ** -0.5) * jnp.exp(g)).astype(BF16)
        k_inv = (k * jnp.exp(-g)).astype(BF16)
        k_end = (k * jnp.exp(g_last - g)).astype(BF16)
        att = lax.dot_general(q_dec, k_inv, _NT, preferred_element_type=F32)
        att = jnp.where(causal, att, 0.0).astype(BF16)
        state = s_ref[...]
        o = (jnp.dot(att, v, preferred_element_type=F32)
             + jnp.dot(q_dec, state.astype(BF16), preferred_element_type=F32))
        upd = lax.dot_general(k_end, v, _TN, preferred_element_type=F32)
        decay = jnp.exp(jnp.sum(log_a_t[:, c * GLA_CHUNK:(c + 1) * GLA_CHUNK], axis=-1, keepdims=True))
        s_ref[...] = decay * state + upd

        o = o * _rms_scale(o) * nw_ref[...]
        gr = gr_ref[0, rows, :].astype(F32)
        gb = gb_ref[0, rows, :].astype(F32)
        o = o * (gr * jax.nn.sigmoid(gr)) * jax.nn.sigmoid(gb)
        o_ref[0, rows, :] = o.astype(BF16)


def _gla(proj3, lr3, w2, w2t, bias, bias_t, norm_w, *, tb):
    b, t, _ = proj3.shape
    return pl.pallas_call(
        functools.partial(_gla_kernel, tb=tb),
        grid=(b, GLA_HEADS, t // tb),
        in_specs=[
            pl.BlockSpec((1, tb, GLA_DK), lambda i, h, j: (i, j, COL_GQ // GLA_DK + h)),
            pl.BlockSpec((1, tb, GLA_DK), lambda i, h, j: (i, j, COL_GK // GLA_DK + h)),
            pl.BlockSpec((1, tb, GLA_DV), lambda i, h, j: (i, j, COL_GV // GLA_DV + h)),
            pl.BlockSpec((1, tb, GLA_DV), lambda i, h, j: (i, j, COL_GR // GLA_DV + h)),
            pl.BlockSpec((1, tb, GLA_DV), lambda i, h, j: (i, j, COL_GATE_B // GLA_DV + h)),
            pl.BlockSpec((1, tb, LANES), lambda i, h, j: (i, j, 0)),
            pl.BlockSpec((LANES, GLA_DK), lambda i, h, j: (0, h)),
            pl.BlockSpec((GLA_DK, LANES), lambda i, h, j: (h, 0)),
            pl.BlockSpec((1, GLA_DK), lambda i, h, j: (0, h)),
            pl.BlockSpec((GLA_DK, 1), lambda i, h, j: (h, 0)),
            pl.BlockSpec((1, GLA_DV), lambda i, h, j: (0, 0)),
        ],
        out_specs=pl.BlockSpec((1, tb, GLA_DV), lambda i, h, j: (i, j, h)),
        out_shape=jax.ShapeDtypeStruct((b, t, D_MODEL), BF16),
        scratch_shapes=[pltpu.VMEM((GLA_DK, GLA_DV), F32)],
        compiler_params=pltpu.CompilerParams(
            dimension_semantics=("arbitrary", "arbitrary", "arbitrary"), vmem_limit_bytes=VMEM_LIMIT),
        name="gla",
    )(proj3, proj3, proj3, proj3, proj3, lr3, w2, w2t, bias, bias_t, norm_w)


def _outproj_kernel(a_ref, g_ref, x_ref, w_ref, h_ref):
    merged = a_ref[...] + g_ref[...]
    h_ref[...] = x_ref[...] + jnp.dot(merged, w_ref[...], preferred_element_type=F32)


def _outproj(a2, g2, x2, w_out, *, tm):
    n = x2.shape[0]
    return pl.pallas_call(
        _outproj_kernel,
        grid=(n // tm,),
        in_specs=[
            pl.BlockSpec((tm, D_MODEL), lambda i: (i, 0)),
            pl.BlockSpec((tm, D_MODEL), lambda i: (i, 0)),
            pl.BlockSpec((tm, D_MODEL), lambda i: (i, 0)),
            pl.BlockSpec((D_MODEL, D_MODEL), lambda i: (0, 0)),
        ],
        out_specs=pl.BlockSpec((tm, D_MODEL), lambda i: (i, 0)),
        out_shape=jax.ShapeDtypeStruct((n, D_MODEL), F32),
        compiler_params=pltpu.CompilerParams(
            dimension_semantics=("arbitrary",), vmem_limit_bytes=VMEM_LIMIT),
        name="outproj",
    )(a2, g2, x2, w_out)


def _ffn_kernel(h_ref, nw_ref, wg_ref, wu_ref, wd_ref, fw_ref, o_ref, v_ref, *, final_norm):
    j = pl.program_id(1)

    @pl.when(j == 0)
    def _():
        h = h_ref[...]
        v_ref[...] = (h * _rms_scale(h) * nw_ref[...]).astype(BF16)

    v = v_ref[...]
    gate = jnp.dot(v, wg_ref[...], preferred_element_type=F32)
    up = jnp.dot(v, wu_ref[...], preferred_element_type=F32)
    act = (gate * jax.nn.sigmoid(gate) * up).astype(BF16)
    part = jnp.dot(act, wd_ref[...], preferred_element_type=F32)

    @pl.when(j == 0)
    def _():
        o_ref[...] = part

    @pl.when(j > 0)
    def _():
        o_ref[...] += part

    @pl.when(j == pl.num_programs(1) - 1)
    def _():
        y = h_ref[...] + o_ref[...]
        o_ref[...] = y * _rms_scale(y) * fw_ref[...] if final_norm else y


def _ffn(h2, norm_w, w_gate, w_up, w_down, final_w, *, tm, th, final_norm):
    n = h2.shape[0]
    return pl.pallas_call(
        functools.partial(_ffn_kernel, final_norm=final_norm),
        grid=(n // tm, FFN_HIDDEN // th),
        in_specs=[
            pl.BlockSpec((tm, D_MODEL), lambda i, j: (i, 0)),
            pl.BlockSpec((1, D_MODEL), lambda i, j: (0, 0)),
            pl.BlockSpec((D_MODEL, th), lambda i, j: (0, j)),
            pl.BlockSpec((D_MODEL, th), lambda i, j: (0, j)),
            pl.BlockSpec((th, D_MODEL), lambda i, j: (j, 0)),
            pl.BlockSpec((1, D_MODEL), lambda i, j: (0, 0)),
        ],
        out_specs=pl.BlockSpec((tm, D_MODEL), lambda i, j: (i, 0)),
        out_shape=jax.ShapeDtypeStruct((n, D_MODEL), F32),
        scratch_shapes=[pltpu.VMEM((tm, D_MODEL), BF16)],
        compiler_params=pltpu.CompilerParams(
            dimension_semantics=("arbitrary", "arbitrary"), vmem_limit_bytes=VMEM_LIMIT),
        name="ffn",
    )(h2, norm_w, w_gate, w_up, w_down, final_w)


def _split_w_in(w):
    widths = (D_MODEL, KV_WIDTH, KV_WIDTH, GLA_HEADS * GLA_DK, GLA_HEADS * GLA_DK, D_MODEL,
              GLA_GATE_RANK, D_MODEL, D_MODEL, D_MODEL)
    parts, start = [], 0
    for width in widths:
        parts.append(w[:, start:start + width])
        start += width
    aq, ak, av, gq, gk, gv, g_lr, g_r, gate_a, gate_b = parts
    main = jnp.concatenate([aq, gv, g_r, gate_a, gate_b, gq, gk, ak, av], axis=1).astype(BF16)
    lr = jnp.pad(g_lr, ((0, 0), (0, LANES - GLA_GATE_RANK))).astype(BF16)
    return main, lr


def _tile(n, pref):
    return pref if n % pref == 0 else n


def kernel(x, norm1_w, w_in, gla_gate_w2, gla_gate_b, attn_sinks, gla_norm_w, w_out, norm2_w,
           w_ffn_gate, w_ffn_up, w_ffn_down, final_norm_w):
    b, t, d = x.shape
    n = b * t
    depth = w_in.shape[0]
    h2 = x.reshape(n, d)
    for l in range(depth):
        w_main, w_lr = _split_w_in(w_in[l])
        proj, lr = _inproj(h2, norm1_w[l].reshape(1, d), w_main, w_lr, tm=_tile(n, 1024), tn=1280)
        proj3 = proj.reshape(b, t, PROJ_WIDTH)
        lr3 = lr.reshape(b, t, LANES)

        a = _attention(proj3, attn_sinks[l], tq=_tile(t, 512))

        w2 = jnp.pad(gla_gate_w2[l], ((0, LANES - GLA_GATE_RANK), (0, 0))).astype(BF16)
        gb = gla_gate_b[l]
        g = _gla(proj3, lr3, w2, w2.T, gb.reshape(1, -1), gb.reshape(-1, 1),
                 gla_norm_w[l].reshape(1, GLA_DV), tb=_tile(t, 512))

        h2 = _outproj(a.reshape(n, d), g.reshape(n, d), h2, w_out[l].astype(BF16), tm=_tile(n, 512))
        h2 = _ffn(h2, norm2_w[l].reshape(1, d), w_ffn_gate[l].astype(BF16), w_ffn_up[l].astype(BF16),
                  w_ffn_down[l].astype(BF16), final_norm_w.reshape(1, d), tm=_tile(n, 512), th=512,
                  final_norm=l == depth - 1)
    return h2.reshape(b, t, d)
```

```python
import functools

import jax
import jax.numpy as jnp
from jax import lax
from jax.experimental import pallas as pl
from jax.experimental.pallas import tpu as pltpu

F32 = jnp.float32
BF16 = jnp.bfloat16

D_MODEL = 2048
HEAD_DIM = 64
N_Q_HEADS = D_MODEL // HEAD_DIM
N_KV_HEADS = 4
GQA_GROUP = N_Q_HEADS // N_KV_HEADS
KV_WIDTH = N_KV_HEADS * HEAD_DIM
WINDOW = 128

GLA_HEADS = 4
GLA_DK = (D_MODEL // 2) // GLA_HEADS
GLA_DV = D_MODEL // GLA_HEADS
GLA_GATE_RANK = 16
GLA_GATE_NORMALIZER = 16.0
GLA_CHUNK = 64

FFN_HIDDEN = ((8 * D_MODEL // 3 + 255) // 256) * 256
RMS_EPS = 1e-6
MASK_VALUE = -1e30

LANES = 128

COL_AQ = 0
COL_GV = COL_AQ + D_MODEL
COL_GR = COL_GV + D_MODEL
COL_GATE_A = COL_GR + D_MODEL
COL_GATE_B = COL_GATE_A + D_MODEL
COL_GQ = COL_GATE_B + D_MODEL
COL_GK = COL_GQ + GLA_HEADS * GLA_DK
PROJ_WIDTH = COL_GK + GLA_HEADS * GLA_DK
PROJ_TILE = 1024

VMEM_LIMIT = 56 * 1024 * 1024

_NT = (((1,), (1,)), ((), ()))
_TN = (((0,), (0,)), ((), ()))


def _rms_scale(x):
    return lax.rsqrt(jnp.mean(x * x, axis=-1, keepdims=True) + RMS_EPS)


def _inproj_kernel(x_ref, nw_ref, w_ref, wkv_ref, wlr_ref, proj_ref, kv_ref, lr_ref, u_ref):
    j = pl.program_id(1)

    @pl.when(j == 0)
    def _():
        x = x_ref[...]
        u = (x * _rms_scale(x) * nw_ref[...]).astype(BF16)
        u_ref[...] = u
        kv_ref[...] = jnp.dot(u, wkv_ref[...], preferred_element_type=F32).astype(BF16)
        lr_ref[...] = jnp.dot(u, wlr_ref[...], preferred_element_type=F32)

    def tile():
        return jnp.dot(u_ref[...], w_ref[...], preferred_element_type=F32)

    is_silu = (j >= COL_GR // PROJ_TILE) & (j < COL_GATE_A // PROJ_TILE)
    is_sigmoid = (j >= COL_GATE_A // PROJ_TILE) & (j < COL_GQ // PROJ_TILE)

    @pl.when(is_silu)
    def _():
        y = tile()
        proj_ref[...] = (y * jax.nn.sigmoid(y)).astype(BF16)

    @pl.when(is_sigmoid)
    def _():
        proj_ref[...] = jax.nn.sigmoid(tile()).astype(BF16)

    @pl.when(jnp.logical_not(is_silu | is_sigmoid))
    def _():
        proj_ref[...] = tile().astype(BF16)


def _inproj(x2, norm_w, w_main, w_kv, w_lr, *, tm):
    n = x2.shape[0]
    return pl.pallas_call(
        _inproj_kernel,
        grid=(n // tm, PROJ_WIDTH // PROJ_TILE),
        in_specs=[
            pl.BlockSpec((tm, D_MODEL), lambda i, j: (i, 0)),
            pl.BlockSpec((1, D_MODEL), lambda i, j: (0, 0)),
            pl.BlockSpec((D_MODEL, PROJ_TILE), lambda i, j: (0, j)),
            pl.BlockSpec((D_MODEL, 2 * KV_WIDTH), lambda i, j: (0, 0)),
            pl.BlockSpec((D_MODEL, LANES), lambda i, j: (0, 0)),
        ],
        out_specs=[
            pl.BlockSpec((tm, PROJ_TILE), lambda i, j: (i, j)),
            pl.BlockSpec((tm, 2 * KV_WIDTH), lambda i, j: (i, 0)),
            pl.BlockSpec((tm, LANES), lambda i, j: (i, 0)),
        ],
        out_shape=[
            jax.ShapeDtypeStruct((n, PROJ_WIDTH), BF16),
            jax.ShapeDtypeStruct((n, 2 * KV_WIDTH), BF16),
            jax.ShapeDtypeStruct((n, LANES), F32),
        ],
        scratch_shapes=[pltpu.VMEM((tm, D_MODEL), BF16)],
        compiler_params=pltpu.CompilerParams(
            dimension_semantics=("arbitrary", "arbitrary"), vmem_limit_bytes=VMEM_LIMIT),
        name="inproj",
    )(x2, norm_w, w_main, w_kv, w_lr)


PAIR = 2 * HEAD_DIM
PAIRS_PER_GROUP = GQA_GROUP // 2
STACK = PAIRS_PER_GROUP * WINDOW


def _attn_qblock(sink_ref, q_ref, k_ref, v_ref, ga_ref, o_ref, row0, k_row0, has_prev):
    rows = pl.ds(row0, WINDOW)
    nk = 2 * WINDOW if has_prev else WINDOW
    krows = pl.ds(k_row0, nk)
    lo = lax.broadcasted_iota(jnp.int32, (1, PAIR), 1) < HEAD_DIM
    top = lax.broadcasted_iota(jnp.int32, (PAIR, 1), 0) < HEAD_DIM
    ki = lax.broadcasted_iota(jnp.int32, (WINDOW, STACK), 0)
    qi = lax.broadcasted_iota(jnp.int32, (WINDOW, STACK), 1) & (WINDOW - 1)
    upper = ki > qi

    def halves(pair_tile, g):
        swapped = pltpu.roll(pair_tile, HEAD_DIM, axis=1)
        own, other = (pair_tile, swapped) if g % 2 == 0 else (swapped, pair_tile)
        zero = jnp.zeros_like(pair_tile)
        return jnp.where(lo, own, zero), jnp.where(lo, zero, other)

    def probs(st, sink_row):
        if has_prev:
            folded = jnp.where(upper, st[:WINDOW], st[WINDOW:])
        else:
            folded = jnp.where(upper, MASK_VALUE, st)
        m = jnp.maximum(jnp.max(folded, axis=0, keepdims=True), sink_row)
        p = jnp.exp(folded - m)
        inv = 1.0 / (jnp.sum(p, axis=0, keepdims=True) + jnp.exp(sink_row - m))
        if has_prev:
            p = jnp.concatenate([jnp.where(upper, p, 0.0), jnp.where(upper, 0.0, p)], axis=0)
        return p.astype(BF16), inv

    for g in range(N_KV_HEADS):
        k_lanes = slice((g // 2) * PAIR, (g // 2 + 1) * PAIR)
        v_lanes = slice(KV_WIDTH + (g // 2) * PAIR, KV_WIDTH + (g // 2 + 1) * PAIR)
        k_lo, k_hi = halves(k_ref[0, krows, k_lanes], g)
        v_lo, v_hi = halves(v_ref[0, krows, v_lanes], g)
        first_pair = g * PAIRS_PER_GROUP
        q = jnp.concatenate(
            [q_ref[0, rows, (first_pair + p) * PAIR:(first_pair + p + 1) * PAIR]
             for p in range(PAIRS_PER_GROUP)], axis=0) * (HEAD_DIM ** -0.5)
        sink_even = jnp.concatenate(
            [jnp.full((1, WINDOW), sink_ref[2 * (first_pair + p)], F32) for p in range(PAIRS_PER_GROUP)], axis=1)
        sink_odd = jnp.concatenate(
            [jnp.full((1, WINDOW), sink_ref[2 * (first_pair + p) + 1], F32) for p in range(PAIRS_PER_GROUP)],
            axis=1)
        p_even, inv_even = probs(lax.dot_general(k_lo, q, _NT, preferred_element_type=F32), sink_even)
        p_odd, inv_odd = probs(lax.dot_general(k_hi, q, _NT, preferred_element_type=F32), sink_odd)
        ot = (lax.dot_general(v_lo, p_even, _TN, preferred_element_type=F32)
              + lax.dot_general(v_hi, p_odd, _TN, preferred_element_type=F32))
        o = (ot * jnp.where(top, inv_even, inv_odd)).T
        for p in range(PAIRS_PER_GROUP):
            cols = slice((first_pair + p) * PAIR, (first_pair + p + 1) * PAIR)
            gate = ga_ref[0, rows, cols].astype(F32)
            o_ref[0, rows, cols] = (gate * o[p * WINDOW:(p + 1) * WINDOW]).astype(BF16)


def _attn_kernel(sink_ref, q_ref, kv_ref, ga_ref, o_ref, *, tq):
    t = pl.program_id(1)
    block = functools.partial(_attn_qblock, sink_ref, q_ref, kv_ref, kv_ref, ga_ref, o_ref)

    @pl.when(t == 0)
    def _():
        block(0, 0, has_prev=False)

    @pl.when(t > 0)
    def _():
        block(0, pl.multiple_of(t * tq - WINDOW, WINDOW), has_prev=True)

    def body(blk, carry):
        row0 = pl.multiple_of(blk * WINDOW, WINDOW)
        block(row0, pl.multiple_of(t * tq + row0 - WINDOW, WINDOW), has_prev=True)
        return carry

    lax.fori_loop(1, tq // WINDOW, body, 0)


def _attention(proj3, kv3, sinks, *, tq):
    b, t, _ = proj3.shape
    return pl.pallas_call(
        functools.partial(_attn_kernel, tq=tq),
        grid=(b, t // tq),
        in_specs=[
            pl.BlockSpec(memory_space=pltpu.SMEM),
            pl.BlockSpec((1, tq, D_MODEL), lambda i, j: (i, j, COL_AQ // D_MODEL)),
            pl.BlockSpec((1, t, 2 * KV_WIDTH), lambda i, j: (i, 0, 0)),
            pl.BlockSpec((1, tq, D_MODEL), lambda i, j: (i, j, COL_GATE_A // D_MODEL)),
        ],
        out_specs=pl.BlockSpec((1, tq, D_MODEL), lambda i, j: (i, j, 0)),
        out_shape=jax.ShapeDtypeStruct((b, t, D_MODEL), BF16),
        compiler_params=pltpu.CompilerParams(
            dimension_semantics=("arbitrary", "arbitrary"), vmem_limit_bytes=VMEM_LIMIT),
        name="swa_attention",
    )(sinks, proj3, kv3, proj3)


def _log_sigmoid(x):
    return jnp.minimum(x, 0.0) - jnp.log(1.0 + jnp.exp(-jnp.abs(x)))


def _gla_kernel(q_ref, k_ref, v_ref, gr_ref, gb_ref, lr_ref, w2_ref, b_ref, nw_ref, o_ref, s_ref, *, tb):
    @pl.when(pl.program_id(2) == 0)
    def _():
        s_ref[...] = jnp.zeros_like(s_ref)

    lr = lr_ref[0].astype(BF16)
    logit = jnp.dot(lr, w2_ref[...], preferred_element_type=F32) + b_ref[...]
    log_a = _log_sigmoid(logit) * (1.0 / GLA_GATE_NORMALIZER)
    log_a_t = log_a.T

    ri = lax.broadcasted_iota(jnp.int32, (GLA_CHUNK, GLA_CHUNK), 0)
    ci = lax.broadcasted_iota(jnp.int32, (GLA_CHUNK, GLA_CHUNK), 1)
    causal = ri >= ci
    tri = causal.astype(BF16)

    for c in range(tb // GLA_CHUNK):
        rows = pl.ds(c * GLA_CHUNK, GLA_CHUNK)
        la = log_a[c * GLA_CHUNK:(c + 1) * GLA_CHUNK, :]
        la_hi = la.astype(BF16)
        la_lo = (la - la_hi.astype(F32)).astype(BF16)
        g = (jnp.dot(tri, la_hi, preferred_element_type=F32)
             + jnp.dot(tri, la_lo, preferred_element_type=F32))
        g_last = g[GLA_CHUNK - 1:GLA_CHUNK, :]
        q = q_ref[0, rows, :].astype(F32)
        k = k_ref[0, rows, :].astype(F32)
        v = v_ref[0, rows, :]
        q_dec = (q * (GLA_DK ** -0.5) * jnp.exp(g)).astype(BF16)
        k_inv = (k * jnp.exp(-g)).astype(BF16)
        k_end = (k * jnp.exp(g_last - g)).astype(BF16)
        att = lax.dot_general(q_dec, k_inv, _NT, preferred_element_type=F32)
        att = jnp.where(causal, att, 0.0).astype(BF16)
        state = s_ref[...]
        o = (jnp.dot(att, v, preferred_element_type=F32)
             + jnp.dot(q_dec, state.astype(BF16), preferred_element_type=F32))
        upd = lax.dot_general(k_end, v, _TN, preferred_element_type=F32)
        decay = jnp.exp(jnp.sum(log_a_t[:, c * GLA_CHUNK:(c + 1) * GLA_CHUNK], axis=-1, keepdims=True))
        s_ref[...] = decay * state + upd

        o = o * _rms_scale(o) * nw_ref[...]
        o = o * gr_ref[0, rows, :].astype(F32) * gb_ref[0, rows, :].astype(F32)
        o_ref[0, rows, :] = o.astype(BF16)


def _gla(proj3, lr3, w2, bias, norm_w, *, tb):
    b, t, _ = proj3.shape
    return pl.pallas_call(
        functools.partial(_gla_kernel, tb=tb),
        grid=(b, GLA_HEADS, t // tb),
        in_specs=[
            pl.BlockSpec((1, tb, GLA_DK), lambda i, h, j: (i, j, COL_GQ // GLA_DK + h)),
            pl.BlockSpec((1, tb, GLA_DK), lambda i, h, j: (i, j, COL_GK // GLA_DK + h)),
            pl.BlockSpec((1, tb, GLA_DV), lambda i, h, j: (i, j, COL_GV // GLA_DV + h)),
            pl.BlockSpec((1, tb, GLA_DV), lambda i, h, j: (i, j, COL_GR // GLA_DV + h)),
            pl.BlockSpec((1, tb, GLA_DV), lambda i, h, j: (i, j, COL_GATE_B // GLA_DV + h)),
            pl.BlockSpec((1, tb, LANES), lambda i, h, j: (i, j, 0)),
            pl.BlockSpec((LANES, GLA_DK), lambda i, h, j: (0, h)),
            pl.BlockSpec((1, GLA_DK), lambda i, h, j: (0, h)),
            pl.BlockSpec((1, GLA_DV), lambda i, h, j: (0, 0)),
        ],
        out_specs=pl.BlockSpec((1, tb, GLA_DV), lambda i, h, j: (i, j, h)),
        out_shape=jax.ShapeDtypeStruct((b, t, D_MODEL), BF16),
        scratch_shapes=[pltpu.VMEM((GLA_DK, GLA_DV), F32)],
        compiler_params=pltpu.CompilerParams(
            dimension_semantics=("arbitrary", "arbitrary", "arbitrary"), vmem_limit_bytes=VMEM_LIMIT),
        name="gla",
    )(proj3, proj3, proj3, proj3, proj3, lr3, w2, bias, norm_w)


def _outproj_kernel(a_ref, g_ref, x_ref, w_ref, h_ref):
    merged = a_ref[...] + g_ref[...]
    h_ref[...] = x_ref[...] + jnp.dot(merged, w_ref[...], preferred_element_type=F32)


def _outproj(a2, g2, x2, w_out, *, tm):
    n = x2.shape[0]
    return pl.pallas_call(
        _outproj_kernel,
        grid=(n // tm,),
        in_specs=[
            pl.BlockSpec((tm, D_MODEL), lambda i: (i, 0)),
            pl.BlockSpec((tm, D_MODEL), lambda i: (i, 0)),
            pl.BlockSpec((tm, D_MODEL), lambda i: (i, 0)),
            pl.BlockSpec((D_MODEL, D_MODEL), lambda i: (0, 0)),
        ],
        out_specs=pl.BlockSpec((tm, D_MODEL), lambda i: (i, 0)),
        out_shape=jax.ShapeDtypeStruct((n, D_MODEL), F32),
        compiler_params=pltpu.CompilerParams(
            dimension_semantics=("arbitrary",), vmem_limit_bytes=VMEM_LIMIT),
        name="outproj",
    )(a2, g2, x2, w_out)


def _ffn_kernel(h_ref, nw_ref, wg_ref, wu_ref, wd_ref, fw_ref, o_ref, v_ref, *, final_norm):
    j = pl.program_id(1)

    @pl.when(j == 0)
    def _():
        h = h_ref[...]
        v_ref[...] = (h * _rms_scale(h) * nw_ref[...]).astype(BF16)
        o_ref[...] = h

    v = v_ref[...]
    gate = jnp.dot(v, wg_ref[...], preferred_element_type=F32)
    up = jnp.dot(v, wu_ref[...], preferred_element_type=F32)
    act = (gate * jax.nn.sigmoid(gate) * up).astype(BF16)
    o_ref[...] += jnp.dot(act, wd_ref[...], preferred_element_type=F32)

    if final_norm:
        @pl.when(j == pl.num_programs(1) - 1)
        def _():
            y = o_ref[...]
            o_ref[...] = y * _rms_scale(y) * fw_ref[...]


def _ffn(h2, norm_w, w_gate, w_up, w_down, final_w, *, tm, th, final_norm):
    n = h2.shape[0]
    return pl.pallas_call(
        functools.partial(_ffn_kernel, final_norm=final_norm),
        grid=(n // tm, FFN_HIDDEN // th),
        in_specs=[
            pl.BlockSpec((tm, D_MODEL), lambda i, j: (i, 0)),
            pl.BlockSpec((1, D_MODEL), lambda i, j: (0, 0)),
            pl.BlockSpec((D_MODEL, th), lambda i, j: (0, j)),
            pl.BlockSpec((D_MODEL, th), lambda i, j: (0, j)),
            pl.BlockSpec((th, D_MODEL), lambda i, j: (j, 0)),
            pl.BlockSpec((1, D_MODEL), lambda i, j: (0, 0)),
        ],
        out_specs=pl.BlockSpec((tm, D_MODEL), lambda i, j: (i, 0)),
        out_shape=jax.ShapeDtypeStruct((n, D_MODEL), F32),
        scratch_shapes=[pltpu.VMEM((tm, D_MODEL), BF16)],
        compiler_params=pltpu.CompilerParams(
            dimension_semantics=("arbitrary", "arbitrary"), vmem_limit_bytes=VMEM_LIMIT),
        name="ffn",
    )(h2, norm_w, w_gate, w_up, w_down, final_w)


def _split_w_in(w):
    widths = (D_MODEL, KV_WIDTH, KV_WIDTH, GLA_HEADS * GLA_DK, GLA_HEADS * GLA_DK, D_MODEL,
              GLA_GATE_RANK, D_MODEL, D_MODEL, D_MODEL)
    parts, start = [], 0
    for width in widths:
        parts.append(w[:, start:start + width])
        start += width
    aq, ak, av, gq, gk, gv, g_lr, g_r, gate_a, gate_b = parts
    main = jnp.concatenate([aq, gv, g_r, gate_a, gate_b, gq, gk], axis=1).astype(BF16)
    kv = jnp.concatenate([ak, av], axis=1).astype(BF16)
    lr = jnp.pad(g_lr, ((0, 0), (0, LANES - GLA_GATE_RANK))).astype(BF16)
    return main, kv, lr


def _tile(n, pref):
    return pref if n % pref == 0 else n


def kernel(x, norm1_w, w_in, gla_gate_w2, gla_gate_b, attn_sinks, gla_norm_w, w_out, norm2_w,
           w_ffn_gate, w_ffn_up, w_ffn_down, final_norm_w):
    b, t, d = x.shape
    n = b * t
    depth = w_in.shape[0]
    h2 = x.reshape(n, d)
    for l in range(depth):
        w_main, w_kv, w_lr = _split_w_in(w_in[l])
        proj, kv, lr = _inproj(h2, norm1_w[l].reshape(1, d), w_main, w_kv, w_lr, tm=_tile(n, 1024))
        proj3 = proj.reshape(b, t, PROJ_WIDTH)

        a = _attention(proj3, kv.reshape(b, t, 2 * KV_WIDTH), attn_sinks[l], tq=_tile(t, 512))

        w2 = jnp.pad(gla_gate_w2[l], ((0, LANES - GLA_GATE_RANK), (0, 0))).astype(BF16)
        g = _gla(proj3, lr.reshape(b, t, LANES), w2, gla_gate_b[l].reshape(1, -1),
                 gla_norm_w[l].reshape(1, GLA_DV), tb=_tile(t, 512))

        h2 = _outproj(a.reshape(n, d), g.reshape(n, d), h2, w_out[l].astype(BF16), tm=_tile(n, 512))
        h2 = _ffn(h2, norm2_w[l].reshape(1, d), w_ffn_gate[l].astype(BF16), w_ffn_up[l].astype(BF16),
                  w_ffn_down[l].astype(BF16), final_norm_w.reshape(1, d), tm=_tile(n, 512), th=512,
                  final_norm=l == depth - 1)
    return h2.reshape(b, t, d)
```

```python
import functools

import jax
import jax.numpy as jnp
from jax import lax
from jax.experimental import pallas as pl
from jax.experimental.pallas import tpu as pltpu

F32 = jnp.float32
BF16 = jnp.bfloat16

D_MODEL = 2048
HEAD_DIM = 64
N_Q_HEADS = D_MODEL // HEAD_DIM
N_KV_HEADS = 4
GQA_GROUP = N_Q_HEADS // N_KV_HEADS
KV_WIDTH = N_KV_HEADS * HEAD_DIM
WINDOW = 128

GLA_HEADS = 4
GLA_DK = (D_MODEL // 2) // GLA_HEADS
GLA_DV = D_MODEL // GLA_HEADS
GLA_GATE_RANK = 16
GLA_GATE_NORMALIZER = 16.0
GLA_CHUNK = 64

FFN_HIDDEN = ((8 * D_MODEL // 3 + 255) // 256) * 256
RMS_EPS = 1e-6
MASK_VALUE = -1e30

LANES = 128

COL_AQ = 0
COL_GV = COL_AQ + D_MODEL
COL_GR = COL_GV + D_MODEL
COL_GATE_A = COL_GR + D_MODEL
COL_GATE_B = COL_GATE_A + D_MODEL
COL_GQ = COL_GATE_B + D_MODEL
COL_GK = COL_GQ + GLA_HEADS * GLA_DK
PROJ_WIDTH = COL_GK + GLA_HEADS * GLA_DK
PROJ_TILE = 1024

VMEM_LIMIT = 56 * 1024 * 1024

_NT = (((1,), (1,)), ((), ()))
_TN = (((0,), (0,)), ((), ()))


def _rms_scale(x):
    return lax.rsqrt(jnp.mean(x * x, axis=-1, keepdims=True) + RMS_EPS)


def _inproj_kernel(x_ref, nw_ref, w_ref, wkv_ref, wlr_ref, proj_ref, kv_ref, lr_ref, u_ref):
    j = pl.program_id(1)

    @pl.when(j == 0)
    def _():
        x = x_ref[...]
        u = (x * _rms_scale(x) * nw_ref[...]).astype(BF16)
        u_ref[...] = u
        kv_ref[...] = jnp.dot(u, wkv_ref[...], preferred_element_type=F32).astype(BF16)
        lr_ref[...] = jnp.dot(u, wlr_ref[...], preferred_element_type=F32)

    def tile():
        return jnp.dot(u_ref[...], w_ref[...], preferred_element_type=F32)

    is_silu = (j >= COL_GR // PROJ_TILE) & (j < COL_GATE_A // PROJ_TILE)
    is_sigmoid = (j >= COL_GATE_A // PROJ_TILE) & (j < COL_GQ // PROJ_TILE)

    @pl.when(is_silu)
    def _():
        y = tile()
        proj_ref[...] = (y * jax.nn.sigmoid(y)).astype(BF16)

    @pl.when(is_sigmoid)
    def _():
        proj_ref[...] = jax.nn.sigmoid(tile()).astype(BF16)

    @pl.when(jnp.logical_not(is_silu | is_sigmoid))
    def _():
        proj_ref[...] = tile().astype(BF16)


def _inproj(x2, norm_w, w_main, w_kv, w_lr, *, tm):
    n = x2.shape[0]
    return pl.pallas_call(
        _inproj_kernel,
        grid=(n // tm, PROJ_WIDTH // PROJ_TILE),
        in_specs=[
            pl.BlockSpec((tm, D_MODEL), lambda i, j: (i, 0)),
            pl.BlockSpec((1, D_MODEL), lambda i, j: (0, 0)),
            pl.BlockSpec((D_MODEL, PROJ_TILE), lambda i, j: (0, j)),
            pl.BlockSpec((D_MODEL, 2 * KV_WIDTH), lambda i, j: (0, 0)),
            pl.BlockSpec((D_MODEL, LANES), lambda i, j: (0, 0)),
        ],
        out_specs=[
            pl.BlockSpec((tm, PROJ_TILE), lambda i, j: (i, j)),
            pl.BlockSpec((tm, 2 * KV_WIDTH), lambda i, j: (i, 0)),
            pl.BlockSpec((tm, LANES), lambda i, j: (i, 0)),
        ],
        out_shape=[
            jax.ShapeDtypeStruct((n, PROJ_WIDTH), BF16),
            jax.ShapeDtypeStruct((n, 2 * KV_WIDTH), BF16),
            jax.ShapeDtypeStruct((n, LANES), F32),
        ],
        scratch_shapes=[pltpu.VMEM((tm, D_MODEL), BF16)],
        compiler_params=pltpu.CompilerParams(
            dimension_semantics=("arbitrary", "arbitrary"), vmem_limit_bytes=VMEM_LIMIT),
        name="inproj",
    )(x2, norm_w, w_main, w_kv, w_lr)


PAIR = 2 * HEAD_DIM
PAIRS_PER_GROUP = GQA_GROUP // 2
STACK = PAIRS_PER_GROUP * WINDOW


def _attn_qblock(sink_ref, q_ref, k_ref, v_ref, ga_ref, o_ref, row0, k_row0, has_prev):
    rows = pl.ds(row0, WINDOW)
    nk = 2 * WINDOW if has_prev else WINDOW
    krows = pl.ds(k_row0, nk)
    lo = lax.broadcasted_iota(jnp.int32, (1, PAIR), 1) < HEAD_DIM
    top = lax.broadcasted_iota(jnp.int32, (PAIR, 1), 0) < HEAD_DIM
    ki = lax.broadcasted_iota(jnp.int32, (WINDOW, STACK), 0)
    qi = lax.broadcasted_iota(jnp.int32, (WINDOW, STACK), 1) & (WINDOW - 1)
    upper = ki > qi

    def halves(pair_tile, g):
        swapped = pltpu.roll(pair_tile, HEAD_DIM, axis=1)
        own, other = (pair_tile, swapped) if g % 2 == 0 else (swapped, pair_tile)
        zero = jnp.zeros_like(pair_tile)
        return jnp.where(lo, own, zero), jnp.where(lo, zero, other)

    def probs(st, sink_row):
        if has_prev:
            folded = jnp.where(upper, st[:WINDOW], st[WINDOW:])
        else:
            folded = jnp.where(upper, MASK_VALUE, st)
        m = jnp.maximum(jnp.max(folded, axis=0, keepdims=True), sink_row)
        p = jnp.exp(folded - m)
        inv = 1.0 / (jnp.sum(p, axis=0, keepdims=True) + jnp.exp(sink_row - m))
        if has_prev:
            p = jnp.concatenate([jnp.where(upper, p, 0.0), jnp.where(upper, 0.0, p)], axis=0)
        return p.astype(BF16), inv

    def scores(g):
        k_lanes = slice((g // 2) * PAIR, (g // 2 + 1) * PAIR)
        k_lo, k_hi = halves(k_ref[0, krows, k_lanes], g)
        first_pair = g * PAIRS_PER_GROUP
        q = jnp.concatenate(
            [q_ref[0, rows, (first_pair + p) * PAIR:(first_pair + p + 1) * PAIR]
             for p in range(PAIRS_PER_GROUP)], axis=0) * (HEAD_DIM ** -0.5)
        return (lax.dot_general(k_lo, q, _NT, preferred_element_type=F32),
                lax.dot_general(k_hi, q, _NT, preferred_element_type=F32))

    def weighted_values(g, st_even, st_odd):
        v_lanes = slice(KV_WIDTH + (g // 2) * PAIR, KV_WIDTH + (g // 2 + 1) * PAIR)
        v_lo, v_hi = halves(v_ref[0, krows, v_lanes], g)
        first_pair = g * PAIRS_PER_GROUP
        sink_even = jnp.concatenate(
            [jnp.full((1, WINDOW), sink_ref[2 * (first_pair + p)], F32) for p in range(PAIRS_PER_GROUP)], axis=1)
        sink_odd = jnp.concatenate(
            [jnp.full((1, WINDOW), sink_ref[2 * (first_pair + p) + 1], F32) for p in range(PAIRS_PER_GROUP)],
            axis=1)
        p_even, inv_even = probs(st_even, sink_even)
        p_odd, inv_odd = probs(st_odd, sink_odd)
        ot = (lax.dot_general(v_lo, p_even, _TN, preferred_element_type=F32)
              + lax.dot_general(v_hi, p_odd, _TN, preferred_element_type=F32))
        return ot, jnp.where(top, inv_even, inv_odd)

    def finish(g, ot, inv):
        o = (ot * inv).T
        first_pair = g * PAIRS_PER_GROUP
        for p in range(PAIRS_PER_GROUP):
            cols = slice((first_pair + p) * PAIR, (first_pair + p + 1) * PAIR)
            gate = ga_ref[0, rows, cols].astype(F32)
            o_ref[0, rows, cols] = (gate * o[p * WINDOW:(p + 1) * WINDOW]).astype(BF16)

    st_next = scores(0)
    unfinished = None
    for g in range(N_KV_HEADS):
        st_even, st_odd = st_next
        if g + 1 < N_KV_HEADS:
            st_next = scores(g + 1)
        if unfinished is not None:
            finish(*unfinished)
        unfinished = (g,) + weighted_values(g, st_even, st_odd)
    finish(*unfinished)


def _attn_kernel(sink_ref, q_ref, kv_ref, ga_ref, o_ref, *, tq):
    t = pl.program_id(1)
    block = functools.partial(_attn_qblock, sink_ref, q_ref, kv_ref, kv_ref, ga_ref, o_ref)

    @pl.when(t == 0)
    def _():
        block(0, 0, has_prev=False)

    @pl.when(t > 0)
    def _():
        block(0, pl.multiple_of(t * tq - WINDOW, WINDOW), has_prev=True)

    def body(blk, carry):
        row0 = pl.multiple_of(blk * WINDOW, WINDOW)
        block(row0, pl.multiple_of(t * tq + row0 - WINDOW, WINDOW), has_prev=True)
        return carry

    lax.fori_loop(1, tq // WINDOW, body, 0)


def _attention(proj3, kv3, sinks, *, tq):
    b, t, _ = proj3.shape
    return pl.pallas_call(
        functools.partial(_attn_kernel, tq=tq),
        grid=(b, t // tq),
        in_specs=[
            pl.BlockSpec(memory_space=pltpu.SMEM),
            pl.BlockSpec((1, tq, D_MODEL), lambda i, j: (i, j, COL_AQ // D_MODEL)),
            pl.BlockSpec((1, t, 2 * KV_WIDTH), lambda i, j: (i, 0, 0)),
            pl.BlockSpec((1, tq, D_MODEL), lambda i, j: (i, j, COL_GATE_A // D_MODEL)),
        ],
        out_specs=pl.BlockSpec((1, tq, D_MODEL), lambda i, j: (i, j, 0)),
        out_shape=jax.ShapeDtypeStruct((b, t, D_MODEL), BF16),
        compiler_params=pltpu.CompilerParams(
            dimension_semantics=("arbitrary", "arbitrary"), vmem_limit_bytes=VMEM_LIMIT),
        name="swa_attention",
    )(sinks, proj3, kv3, proj3)


def _log_sigmoid(x):
    return jnp.minimum(x, 0.0) - jnp.log(1.0 + jnp.exp(-jnp.abs(x)))


GLA_SUPER = 2 * GLA_CHUNK


def _split_bf16(x):
    hi = x.astype(BF16)
    return hi, (x - hi.astype(F32)).astype(BF16)


def _gla_kernel(q_ref, k_ref, v_ref, gr_ref, gb_ref, lr_ref, w2_ref, b_ref, nw_ref, o_ref, s_ref, *, tb):
    @pl.when(pl.program_id(2) == 0)
    def _():
        s_ref[...] = jnp.zeros_like(s_ref)

    lr = lr_ref[0].astype(BF16)
    logit = jnp.dot(lr, w2_ref[...], preferred_element_type=F32) + b_ref[...]
    log_a = _log_sigmoid(logit) * (1.0 / GLA_GATE_NORMALIZER)

    ri = lax.broadcasted_iota(jnp.int32, (GLA_SUPER, GLA_SUPER), 0)
    ci = lax.broadcasted_iota(jnp.int32, (GLA_SUPER, GLA_SUPER), 1)
    causal = ri >= ci
    cum = (causal & ((ri // GLA_CHUNK) == (ci // GLA_CHUNK))).astype(BF16)
    later_and_all = jnp.concatenate([(ri > ci).astype(BF16), jnp.ones((GLA_SUPER, GLA_SUPER), BF16)], axis=1)
    in_a = lax.broadcasted_iota(jnp.int32, (GLA_SUPER, 1), 0) < GLA_CHUNK

    def intra(c):
        rows = pl.ds(c * GLA_SUPER, GLA_SUPER)
        la = log_a[c * GLA_SUPER:(c + 1) * GLA_SUPER, :]
        la_hi, la_lo = _split_bf16(la)
        g = (jnp.dot(cum, la_hi, preferred_element_type=F32)
             + jnp.dot(cum, la_lo, preferred_element_type=F32))
        lat_hi, lat_lo = _split_bf16(la.T)
        tail = (jnp.dot(lat_hi, later_and_all, preferred_element_type=F32)
                + jnp.dot(lat_lo, later_and_all, preferred_element_type=F32))
        q = q_ref[0, rows, :].astype(F32)
        k = k_ref[0, rows, :].astype(F32)
        v = v_ref[0, rows, :]
        q_dec = q * (GLA_DK ** -0.5) * jnp.exp(g)
        carry = jnp.exp(g[GLA_CHUNK - 1:GLA_CHUNK, :])
        q_carry = jnp.where(in_a, q_dec, q_dec * carry).astype(BF16)
        k_inv = k * jnp.exp(-g)
        k_inv_a = jnp.where(in_a, k_inv, 0.0).astype(BF16)
        k_inv_b = jnp.where(in_a, 0.0, k_inv).astype(BF16)
        att = (lax.dot_general(q_carry, k_inv_a, _NT, preferred_element_type=F32)
               + lax.dot_general(q_dec.astype(BF16), k_inv_b, _NT, preferred_element_type=F32))
        att = jnp.where(causal, att, 0.0).astype(BF16)
        o_intra = jnp.dot(att, v, preferred_element_type=F32)
        k_end_t = (k.T * jnp.exp(tail[:, :GLA_SUPER])).astype(BF16)
        upd = jnp.dot(k_end_t, v, preferred_element_type=F32)
        decay = jnp.exp(tail[:, GLA_SUPER:])
        return q_carry, o_intra, upd, decay

    def inter(c, q_carry, o_intra, upd, decay):
        rows = pl.ds(c * GLA_SUPER, GLA_SUPER)
        state = s_ref[...]
        o = o_intra + jnp.dot(q_carry, state.astype(BF16), preferred_element_type=F32)
        s_ref[...] = jnp.concatenate([decay] * (GLA_DV // GLA_SUPER), axis=1) * state + upd
        o = o * _rms_scale(o) * nw_ref[...]
        o = o * gr_ref[0, rows, :].astype(F32) * gb_ref[0, rows, :].astype(F32)
        o_ref[0, rows, :] = o.astype(BF16)

    n = tb // GLA_SUPER
    ahead = intra(0)
    for c in range(n):
        current = ahead
        if c + 1 < n:
            ahead = intra(c + 1)
        inter(c, *current)


def _gla(proj3, lr3, w2, bias, norm_w, *, tb):
    b, t, _ = proj3.shape
    return pl.pallas_call(
        functools.partial(_gla_kernel, tb=tb),
        grid=(b, GLA_HEADS, t // tb),
        in_specs=[
            pl.BlockSpec((1, tb, GLA_DK), lambda i, h, j: (i, j, COL_GQ // GLA_DK + h)),
            pl.BlockSpec((1, tb, GLA_DK), lambda i, h, j: (i, j, COL_GK // GLA_DK + h)),
            pl.BlockSpec((1, tb, GLA_DV), lambda i, h, j: (i, j, COL_GV // GLA_DV + h)),
            pl.BlockSpec((1, tb, GLA_DV), lambda i, h, j: (i, j, COL_GR // GLA_DV + h)),
            pl.BlockSpec((1, tb, GLA_DV), lambda i, h, j: (i, j, COL_GATE_B // GLA_DV + h)),
            pl.BlockSpec((1, tb, LANES), lambda i, h, j: (i, j, 0)),
            pl.BlockSpec((LANES, GLA_DK), lambda i, h, j: (0, h)),
            pl.BlockSpec((1, GLA_DK), lambda i, h, j: (0, h)),
            pl.BlockSpec((1, GLA_DV), lambda i, h, j: (0, 0)),
        ],
        out_specs=pl.BlockSpec((1, tb, GLA_DV), lambda i, h, j: (i, j, h)),
        out_shape=jax.ShapeDtypeStruct((b, t, D_MODEL), BF16),
        scratch_shapes=[pltpu.VMEM((GLA_DK, GLA_DV), F32)],
        compiler_params=pltpu.CompilerParams(
            dimension_semantics=("arbitrary", "arbitrary", "arbitrary"), vmem_limit_bytes=VMEM_LIMIT),
        name="gla",
    )(proj3, proj3, proj3, proj3, proj3, lr3, w2, bias, norm_w)


def _outproj_kernel(a_ref, g_ref, x_ref, w_ref, h_ref):
    merged = a_ref[...] + g_ref[...]
    h_ref[...] = x_ref[...] + jnp.dot(merged, w_ref[...], preferred_element_type=F32)


def _outproj(a2, g2, x2, w_out, *, tm):
    n = x2.shape[0]
    return pl.pallas_call(
        _outproj_kernel,
        grid=(n // tm,),
        in_specs=[
            pl.BlockSpec((tm, D_MODEL), lambda i: (i, 0)),
            pl.BlockSpec((tm, D_MODEL), lambda i: (i, 0)),
            pl.BlockSpec((tm, D_MODEL), lambda i: (i, 0)),
            pl.BlockSpec((D_MODEL, D_MODEL), lambda i: (0, 0)),
        ],
        out_specs=pl.BlockSpec((tm, D_MODEL), lambda i: (i, 0)),
        out_shape=jax.ShapeDtypeStruct((n, D_MODEL), F32),
        compiler_params=pltpu.CompilerParams(
            dimension_semantics=("arbitrary",), vmem_limit_bytes=VMEM_LIMIT),
        name="outproj",
    )(a2, g2, x2, w_out)


def _ffn_kernel(h_ref, nw_ref, wg_ref, wu_ref, wd_ref, fw_ref, o_ref, v_ref, *, final_norm):
    j = pl.program_id(1)

    @pl.when(j == 0)
    def _():
        h = h_ref[...]
        v_ref[...] = (h * _rms_scale(h) * nw_ref[...]).astype(BF16)
        o_ref[...] = h

    v = v_ref[...]
    gate = jnp.dot(v, wg_ref[...], preferred_element_type=F32)
    up = jnp.dot(v, wu_ref[...], preferred_element_type=F32)
    act = (gate * jax.nn.sigmoid(gate) * up).astype(BF16)
    o_ref[...] += jnp.dot(act, wd_ref[...], preferred_element_type=F32)

    if final_norm:
        @pl.when(j == pl.num_programs(1) - 1)
        def _():
            y = o_ref[...]
            o_ref[...] = y * _rms_scale(y) * fw_ref[...]


def _ffn(h2, norm_w, w_gate, w_up, w_down, final_w, *, tm, th, final_norm):
    n = h2.shape[0]
    return pl.pallas_call(
        functools.partial(_ffn_kernel, final_norm=final_norm),
        grid=(n // tm, FFN_HIDDEN // th),
        in_specs=[
            pl.BlockSpec((tm, D_MODEL), lambda i, j: (i, 0)),
            pl.BlockSpec((1, D_MODEL), lambda i, j: (0, 0)),
            pl.BlockSpec((D_MODEL, th), lambda i, j: (0, j)),
            pl.BlockSpec((D_MODEL, th), lambda i, j: (0, j)),
            pl.BlockSpec((th, D_MODEL), lambda i, j: (j, 0)),
            pl.BlockSpec((1, D_MODEL), lambda i, j: (0, 0)),
        ],
        out_specs=pl.BlockSpec((tm, D_MODEL), lambda i, j: (i, 0)),
        out_shape=jax.ShapeDtypeStruct((n, D_MODEL), F32),
        scratch_shapes=[pltpu.VMEM((tm, D_MODEL), BF16)],
        compiler_params=pltpu.CompilerParams(
            dimension_semantics=("arbitrary", "arbitrary"), vmem_limit_bytes=VMEM_LIMIT),
        name="ffn",
    )(h2, norm_w, w_gate, w_up, w_down, final_w)


def _split_w_in(w):
    widths = (D_MODEL, KV_WIDTH, KV_WIDTH, GLA_HEADS * GLA_DK, GLA_HEADS * GLA_DK, D_MODEL,
              GLA_GATE_RANK, D_MODEL, D_MODEL, D_MODEL)
    parts, start = [], 0
    for width in widths:
        parts.append(w[:, start:start + width])
        start += width
    aq, ak, av, gq, gk, gv, g_lr, g_r, gate_a, gate_b = parts
    main = jnp.concatenate([aq, gv, g_r, gate_a, gate_b, gq, gk], axis=1).astype(BF16)
    kv = jnp.concatenate([ak, av], axis=1).astype(BF16)
    lr = jnp.pad(g_lr, ((0, 0), (0, LANES - GLA_GATE_RANK))).astype(BF16)
    return main, kv, lr


def _tile(n, pref):
    return pref if n % pref == 0 else n


def kernel(x, norm1_w, w_in, gla_gate_w2, gla_gate_b, attn_sinks, gla_norm_w, w_out, norm2_w,
           w_ffn_gate, w_ffn_up, w_ffn_down, final_norm_w):
    b, t, d = x.shape
    n = b * t
    depth = w_in.shape[0]
    h2 = x.reshape(n, d)
    for l in range(depth):
        w_main, w_kv, w_lr = _split_w_in(w_in[l])
        proj, kv, lr = _inproj(h2, norm1_w[l].reshape(1, d), w_main, w_kv, w_lr, tm=_tile(n, 1024))
        proj3 = proj.reshape(b, t, PROJ_WIDTH)

        a = _attention(proj3, kv.reshape(b, t, 2 * KV_WIDTH), attn_sinks[l], tq=_tile(t, 512))

        w2 = jnp.pad(gla_gate_w2[l], ((0, LANES - GLA_GATE_RANK), (0, 0))).astype(BF16)
        g = _gla(proj3, lr.reshape(b, t, LANES), w2, gla_gate_b[l].reshape(1, -1),
                 gla_norm_w[l].reshape(1, GLA_DV), tb=_tile(t, 512))

        h2 = _outproj(a.reshape(n, d), g.reshape(n, d), h2, w_out[l].astype(BF16), tm=_tile(n, 512))
        h2 = _ffn(h2, norm2_w[l].reshape(1, d), w_ffn_gate[l].astype(BF16), w_ffn_up[l].astype(BF16),
                  w_ffn_down[l].astype(BF16), final_norm_w.reshape(1, d), tm=_tile(n, 512), th=512,
                  final_norm=l == depth - 1)
    return h2.reshape(b, t, d)
```

```python
import functools

import jax
import jax.numpy as jnp
from jax import lax
from jax.experimental import pallas as pl
from jax.experimental.pallas import tpu as pltpu

F32 = jnp.float32
BF16 = jnp.bfloat16

D_MODEL = 2048
HEAD_DIM = 64
N_Q_HEADS = D_MODEL // HEAD_DIM
N_KV_HEADS = 4
GQA_GROUP = N_Q_HEADS // N_KV_HEADS
KV_WIDTH = N_KV_HEADS * HEAD_DIM
WINDOW = 128

GLA_HEADS = 4
GLA_DK = (D_MODEL // 2) // GLA_HEADS
GLA_DV = D_MODEL // GLA_HEADS
GLA_GATE_RANK = 16
GLA_GATE_NORMALIZER = 16.0
GLA_CHUNK = 64

FFN_HIDDEN = ((8 * D_MODEL // 3 + 255) // 256) * 256
RMS_EPS = 1e-6
MASK_VALUE = -1e30

LANES = 128

COL_AQ = 0
COL_GV = COL_AQ + D_MODEL
COL_GR = COL_GV + D_MODEL
COL_GATE_A = COL_GR + D_MODEL
COL_GATE_B = COL_GATE_A + D_MODEL
COL_GQ = COL_GATE_B + D_MODEL
COL_GK = COL_GQ + GLA_HEADS * GLA_DK
PROJ_WIDTH = COL_GK + GLA_HEADS * GLA_DK
PROJ_TILE = 2048

VMEM_LIMIT = 56 * 1024 * 1024

_NT = (((1,), (1,)), ((), ()))
_TN = (((0,), (0,)), ((), ()))


def _rms_scale(x):
    return lax.rsqrt(jnp.mean(x * x, axis=-1, keepdims=True) + RMS_EPS)


def _inproj_kernel(x_ref, nw_ref, w_ref, wkv_ref, wlr_ref, proj_ref, kv_ref, lr_ref, u_ref):
    j = pl.program_id(1)

    @pl.when(j == 0)
    def _():
        x = x_ref[...]
        u = (x * _rms_scale(x) * nw_ref[...]).astype(BF16)
        u_ref[...] = u
        kv_ref[...] = jnp.dot(u, wkv_ref[...], preferred_element_type=F32).astype(BF16)
        lr_ref[...] = jnp.dot(u, wlr_ref[...], preferred_element_type=F32)

    def tile():
        return jnp.dot(u_ref[...], w_ref[...], preferred_element_type=F32)

    is_silu = (j >= COL_GR // PROJ_TILE) & (j < COL_GATE_A // PROJ_TILE)
    is_sigmoid = (j >= COL_GATE_A // PROJ_TILE) & (j < COL_GQ // PROJ_TILE)

    @pl.when(is_silu)
    def _():
        y = tile()
        proj_ref[...] = (y * jax.nn.sigmoid(y)).astype(BF16)

    @pl.when(is_sigmoid)
    def _():
        proj_ref[...] = jax.nn.sigmoid(tile()).astype(BF16)

    @pl.when(jnp.logical_not(is_silu | is_sigmoid))
    def _():
        proj_ref[...] = tile().astype(BF16)


def _inproj(x2, norm_w, w_main, w_kv, w_lr, *, tm):
    n = x2.shape[0]
    return pl.pallas_call(
        _inproj_kernel,
        grid=(n // tm, PROJ_WIDTH // PROJ_TILE),
        in_specs=[
            pl.BlockSpec((tm, D_MODEL), lambda i, j: (i, 0)),
            pl.BlockSpec((1, D_MODEL), lambda i, j: (0, 0)),
            pl.BlockSpec((D_MODEL, PROJ_TILE), lambda i, j: (0, j)),
            pl.BlockSpec((D_MODEL, 2 * KV_WIDTH), lambda i, j: (0, 0)),
            pl.BlockSpec((D_MODEL, LANES), lambda i, j: (0, 0)),
        ],
        out_specs=[
            pl.BlockSpec((tm, PROJ_TILE), lambda i, j: (i, j)),
            pl.BlockSpec((tm, 2 * KV_WIDTH), lambda i, j: (i, 0)),
            pl.BlockSpec((tm, LANES), lambda i, j: (i, 0)),
        ],
        out_shape=[
            jax.ShapeDtypeStruct((n, PROJ_WIDTH), BF16),
            jax.ShapeDtypeStruct((n, 2 * KV_WIDTH), BF16),
            jax.ShapeDtypeStruct((n, LANES), F32),
        ],
        scratch_shapes=[pltpu.VMEM((tm, D_MODEL), BF16)],
        compiler_params=pltpu.CompilerParams(
            dimension_semantics=("arbitrary", "arbitrary"), vmem_limit_bytes=VMEM_LIMIT),
        name="inproj",
    )(x2, norm_w, w_main, w_kv, w_lr)


PAIR = 2 * HEAD_DIM
PAIRS_PER_GROUP = GQA_GROUP // 2
STACK = PAIRS_PER_GROUP * WINDOW


def _attn_qblock(sink_ref, q_ref, k_ref, v_ref, ga_ref, o_ref, row0, k_row0, has_prev):
    rows = pl.ds(row0, WINDOW)
    nk = 2 * WINDOW if has_prev else WINDOW
    krows = pl.ds(k_row0, nk)
    lo = lax.broadcasted_iota(jnp.int32, (1, PAIR), 1) < HEAD_DIM
    top = lax.broadcasted_iota(jnp.int32, (PAIR, 1), 0) < HEAD_DIM
    ki = lax.broadcasted_iota(jnp.int32, (WINDOW, STACK), 0)
    qi = lax.broadcasted_iota(jnp.int32, (WINDOW, STACK), 1) & (WINDOW - 1)
    upper = ki > qi

    def halves(pair_tile, g):
        swapped = pltpu.roll(pair_tile, HEAD_DIM, axis=1)
        own, other = (pair_tile, swapped) if g % 2 == 0 else (swapped, pair_tile)
        zero = jnp.zeros_like(pair_tile)
        return jnp.where(lo, own, zero), jnp.where(lo, zero, other)

    def probs(st, sink_row):
        if has_prev:
            folded = jnp.where(upper, st[:WINDOW], st[WINDOW:])
        else:
            folded = jnp.where(upper, MASK_VALUE, st)
        m = jnp.maximum(jnp.max(folded, axis=0, keepdims=True), sink_row)
        p = jnp.exp(folded - m)
        inv = 1.0 / (jnp.sum(p, axis=0, keepdims=True) + jnp.exp(sink_row - m))
        if has_prev:
            p = jnp.concatenate([jnp.where(upper, p, 0.0), jnp.where(upper, 0.0, p)], axis=0)
        return p.astype(BF16), inv

    def scores(g):
        k_lanes = slice((g // 2) * PAIR, (g // 2 + 1) * PAIR)
        k_lo, k_hi = halves(k_ref[0, krows, k_lanes], g)
        first_pair = g * PAIRS_PER_GROUP
        q = jnp.concatenate(
            [q_ref[0, rows, (first_pair + p) * PAIR:(first_pair + p + 1) * PAIR]
             for p in range(PAIRS_PER_GROUP)], axis=0) * (HEAD_DIM ** -0.5)
        return (lax.dot_general(k_lo, q, _NT, preferred_element_type=F32),
                lax.dot_general(k_hi, q, _NT, preferred_element_type=F32))

    def weighted_values(g, st_even, st_odd):
        v_lanes = slice(KV_WIDTH + (g // 2) * PAIR, KV_WIDTH + (g // 2 + 1) * PAIR)
        v_lo, v_hi = halves(v_ref[0, krows, v_lanes], g)
        first_pair = g * PAIRS_PER_GROUP
        sink_even = jnp.concatenate(
            [jnp.full((1, WINDOW), sink_ref[2 * (first_pair + p)], F32) for p in range(PAIRS_PER_GROUP)], axis=1)
        sink_odd = jnp.concatenate(
            [jnp.full((1, WINDOW), sink_ref[2 * (first_pair + p) + 1], F32) for p in range(PAIRS_PER_GROUP)],
            axis=1)
        p_even, inv_even = probs(st_even, sink_even)
        p_odd, inv_odd = probs(st_odd, sink_odd)
        ot = (lax.dot_general(v_lo, p_even, _TN, preferred_element_type=F32)
              + lax.dot_general(v_hi, p_odd, _TN, preferred_element_type=F32))
        return ot, jnp.where(top, inv_even, inv_odd)

    def finish(g, ot, inv):
        o = (ot * inv).T
        first_pair = g * PAIRS_PER_GROUP
        for p in range(PAIRS_PER_GROUP):
            cols = slice((first_pair + p) * PAIR, (first_pair + p + 1) * PAIR)
            gate = ga_ref[0, rows, cols].astype(F32)
            o_ref[0, rows, cols] = (gate * o[p * WINDOW:(p + 1) * WINDOW]).astype(BF16)

    st_next = scores(0)
    unfinished = None
    for g in range(N_KV_HEADS):
        st_even, st_odd = st_next
        if g + 1 < N_KV_HEADS:
            st_next = scores(g + 1)
        if unfinished is not None:
            finish(*unfinished)
        unfinished = (g,) + weighted_values(g, st_even, st_odd)
    finish(*unfinished)


def _attn_kernel(sink_ref, q_ref, kv_ref, ga_ref, o_ref, *, tq):
    t = pl.program_id(1)
    block = functools.partial(_attn_qblock, sink_ref, q_ref, kv_ref, kv_ref, ga_ref, o_ref)

    @pl.when(t == 0)
    def _():
        block(0, 0, has_prev=False)

    @pl.when(t > 0)
    def _():
        block(0, pl.multiple_of(t * tq - WINDOW, WINDOW), has_prev=True)

    def body(blk, carry):
        row0 = pl.multiple_of(blk * WINDOW, WINDOW)
        block(row0, pl.multiple_of(t * tq + row0 - WINDOW, WINDOW), has_prev=True)
        return carry

    lax.fori_loop(1, tq // WINDOW, body, 0)


def _attention(proj3, kv3, sinks, *, tq):
    b, t, _ = proj3.shape
    return pl.pallas_call(
        functools.partial(_attn_kernel, tq=tq),
        grid=(b, t // tq),
        in_specs=[
            pl.BlockSpec(memory_space=pltpu.SMEM),
            pl.BlockSpec((1, tq, D_MODEL), lambda i, j: (i, j, COL_AQ // D_MODEL)),
            pl.BlockSpec((1, t, 2 * KV_WIDTH), lambda i, j: (i, 0, 0)),
            pl.BlockSpec((1, tq, D_MODEL), lambda i, j: (i, j, COL_GATE_A // D_MODEL)),
        ],
        out_specs=pl.BlockSpec((1, tq, D_MODEL), lambda i, j: (i, j, 0)),
        out_shape=jax.ShapeDtypeStruct((b, t, D_MODEL), BF16),
        compiler_params=pltpu.CompilerParams(
            dimension_semantics=("arbitrary", "arbitrary"), vmem_limit_bytes=VMEM_LIMIT),
        name="swa_attention",
    )(sinks, proj3, kv3, proj3)


def _log_sigmoid(x):
    return jnp.minimum(x, 0.0) - jnp.log(1.0 + jnp.exp(-jnp.abs(x)))


GLA_SUPER = 2 * GLA_CHUNK


def _split_bf16(x):
    hi = x.astype(BF16)
    return hi, (x - hi.astype(F32)).astype(BF16)


def _gla_kernel(q_ref, k_ref, v_ref, gr_ref, gb_ref, lr_ref, w2_ref, b_ref, nw_ref, o_ref, s_ref, *, tb):
    @pl.when(pl.program_id(2) == 0)
    def _():
        s_ref[...] = jnp.zeros_like(s_ref)

    lr = lr_ref[0].astype(BF16)
    logit = jnp.dot(lr, w2_ref[...], preferred_element_type=F32) + b_ref[...]
    log_a = _log_sigmoid(logit) * (1.0 / GLA_GATE_NORMALIZER)

    ri = lax.broadcasted_iota(jnp.int32, (GLA_SUPER, GLA_SUPER), 0)
    ci = lax.broadcasted_iota(jnp.int32, (GLA_SUPER, GLA_SUPER), 1)
    causal = ri >= ci
    cum = (causal & ((ri // GLA_CHUNK) == (ci // GLA_CHUNK))).astype(BF16)
    later_and_all = jnp.concatenate([(ri > ci).astype(BF16), jnp.ones((GLA_SUPER, GLA_SUPER), BF16)], axis=1)
    in_a = lax.broadcasted_iota(jnp.int32, (GLA_SUPER, 1), 0) < GLA_CHUNK

    def intra(c):
        rows = pl.ds(c * GLA_SUPER, GLA_SUPER)
        la = log_a[c * GLA_SUPER:(c + 1) * GLA_SUPER, :]
        la_hi, la_lo = _split_bf16(la)
        g = (jnp.dot(cum, la_hi, preferred_element_type=F32)
             + jnp.dot(cum, la_lo, preferred_element_type=F32))
        lat_hi, lat_lo = _split_bf16(la.T)
        tail = (jnp.dot(lat_hi, later_and_all, preferred_element_type=F32)
                + jnp.dot(lat_lo, later_and_all, preferred_element_type=F32))
        q = q_ref[0, rows, :].astype(F32)
        k = k_ref[0, rows, :].astype(F32)
        v = v_ref[0, rows, :]
        q_dec = q * (GLA_DK ** -0.5) * jnp.exp(g)
        carry = jnp.exp(g[GLA_CHUNK - 1:GLA_CHUNK, :])
        q_carry = jnp.where(in_a, q_dec, q_dec * carry).astype(BF16)
        k_inv = k * jnp.exp(-g)
        k_inv_a = jnp.where(in_a, k_inv, 0.0).astype(BF16)
        k_inv_b = jnp.where(in_a, 0.0, k_inv).astype(BF16)
        att = (lax.dot_general(q_carry, k_inv_a, _NT, preferred_element_type=F32)
               + lax.dot_general(q_dec.astype(BF16), k_inv_b, _NT, preferred_element_type=F32))
        att = jnp.where(causal, att, 0.0).astype(BF16)
        o_intra = jnp.dot(att, v, preferred_element_type=F32)
        k_end_t = (k.T * jnp.exp(tail[:, :GLA_SUPER])).astype(BF16)
        upd = jnp.dot(k_end_t, v, preferred_element_type=F32)
        decay = jnp.exp(tail[:, GLA_SUPER:])
        return q_carry, o_intra, upd, decay

    def inter(c, q_carry, o_intra, upd, decay):
        rows = pl.ds(c * GLA_SUPER, GLA_SUPER)
        state = s_ref[...]
        o = o_intra + jnp.dot(q_carry, state.astype(BF16), preferred_element_type=F32)
        s_ref[...] = jnp.concatenate([decay] * (GLA_DV // GLA_SUPER), axis=1) * state + upd
        o = o * _rms_scale(o) * nw_ref[...]
        o = o * gr_ref[0, rows, :].astype(F32) * gb_ref[0, rows, :].astype(F32)
        o_ref[0, rows, :] = o.astype(BF16)

    n = tb // GLA_SUPER
    ahead = intra(0)
    for c in range(n):
        current = ahead
        if c + 1 < n:
            ahead = intra(c + 1)
        inter(c, *current)


def _gla(proj3, lr3, w2, bias, norm_w, *, tb):
    b, t, _ = proj3.shape
    return pl.pallas_call(
        functools.partial(_gla_kernel, tb=tb),
        grid=(b, GLA_HEADS, t // tb),
        in_specs=[
            pl.BlockSpec((1, tb, GLA_DK), lambda i, h, j: (i, j, COL_GQ // GLA_DK + h)),
            pl.BlockSpec((1, tb, GLA_DK), lambda i, h, j: (i, j, COL_GK // GLA_DK + h)),
            pl.BlockSpec((1, tb, GLA_DV), lambda i, h, j: (i, j, COL_GV // GLA_DV + h)),
            pl.BlockSpec((1, tb, GLA_DV), lambda i, h, j: (i, j, COL_GR // GLA_DV + h)),
            pl.BlockSpec((1, tb, GLA_DV), lambda i, h, j: (i, j, COL_GATE_B // GLA_DV + h)),
            pl.BlockSpec((1, tb, LANES), lambda i, h, j: (i, j, 0)),
            pl.BlockSpec((LANES, GLA_DK), lambda i, h, j: (0, h)),
            pl.BlockSpec((1, GLA_DK), lambda i, h, j: (0, h)),
            pl.BlockSpec((1, GLA_DV), lambda i, h, j: (0, 0)),
        ],
        out_specs=pl.BlockSpec((1, tb, GLA_DV), lambda i, h, j: (i, j, h)),
        out_shape=jax.ShapeDtypeStruct((b, t, D_MODEL), BF16),
        scratch_shapes=[pltpu.VMEM((GLA_DK, GLA_DV), F32)],
        compiler_params=pltpu.CompilerParams(
            dimension_semantics=("arbitrary", "arbitrary", "arbitrary"), vmem_limit_bytes=VMEM_LIMIT),
        name="gla",
    )(proj3, proj3, proj3, proj3, proj3, lr3, w2, bias, norm_w)


def _outproj_kernel(a_ref, g_ref, x_ref, w_ref, h_ref):
    merged = a_ref[...] + g_ref[...]
    h_ref[...] = x_ref[...] + jnp.dot(merged, w_ref[...], preferred_element_type=F32)


def _outproj(a2, g2, x2, w_out, *, tm):
    n = x2.shape[0]
    return pl.pallas_call(
        _outproj_kernel,
        grid=(n // tm,),
        in_specs=[
            pl.BlockSpec((tm, D_MODEL), lambda i: (i, 0)),
            pl.BlockSpec((tm, D_MODEL), lambda i: (i, 0)),
            pl.BlockSpec((tm, D_MODEL), lambda i: (i, 0)),
            pl.BlockSpec((D_MODEL, D_MODEL), lambda i: (0, 0)),
        ],
        out_specs=pl.BlockSpec((tm, D_MODEL), lambda i: (i, 0)),
        out_shape=jax.ShapeDtypeStruct((n, D_MODEL), F32),
        compiler_params=pltpu.CompilerParams(
            dimension_semantics=("arbitrary",), vmem_limit_bytes=VMEM_LIMIT),
        name="outproj",
    )(a2, g2, x2, w_out)


def _ffn_kernel(h_ref, nw_ref, wg_ref, wu_ref, wd_ref, fw_ref, o_ref, v_ref, *, final_norm):
    j = pl.program_id(1)

    @pl.when(j == 0)
    def _():
        h = h_ref[...]
        v_ref[...] = (h * _rms_scale(h) * nw_ref[...]).astype(BF16)
        o_ref[...] = h

    v = v_ref[...]
    gate = jnp.dot(v, wg_ref[...], preferred_element_type=F32)
    up = jnp.dot(v, wu_ref[...], preferred_element_type=F32)
    act = (gate * jax.nn.sigmoid(gate) * up).astype(BF16)
    o_ref[...] += jnp.dot(act, wd_ref[...], preferred_element_type=F32)

    if final_norm:
        @pl.when(j == pl.num_programs(1) - 1)
        def _():
            y = o_ref[...]
            o_ref[...] = y * _rms_scale(y) * fw_ref[...]


def _ffn(h2, norm_w, w_gate, w_up, w_down, final_w, *, tm, th, final_norm):
    n = h2.shape[0]
    return pl.pallas_call(
        functools.partial(_ffn_kernel, final_norm=final_norm),
        grid=(n // tm, FFN_HIDDEN // th),
        in_specs=[
            pl.BlockSpec((tm, D_MODEL), lambda i, j: (i, 0)),
            pl.BlockSpec((1, D_MODEL), lambda i, j: (0, 0)),
            pl.BlockSpec((D_MODEL, th), lambda i, j: (0, j)),
            pl.BlockSpec((D_MODEL, th), lambda i, j: (0, j)),
            pl.BlockSpec((th, D_MODEL), lambda i, j: (j, 0)),
            pl.BlockSpec((1, D_MODEL), lambda i, j: (0, 0)),
        ],
        out_specs=pl.BlockSpec((tm, D_MODEL), lambda i, j: (i, 0)),
        out_shape=jax.ShapeDtypeStruct((n, D_MODEL), F32),
        scratch_shapes=[pltpu.VMEM((tm, D_MODEL), BF16)],
        compiler_params=pltpu.CompilerParams(
            dimension_semantics=("arbitrary", "arbitrary"), vmem_limit_bytes=VMEM_LIMIT),
        name="ffn",
    )(h2, norm_w, w_gate, w_up, w_down, final_w)


_IN_WIDTHS = (D_MODEL, KV_WIDTH, KV_WIDTH, GLA_HEADS * GLA_DK, GLA_HEADS * GLA_DK, D_MODEL,
              GLA_GATE_RANK, D_MODEL, D_MODEL, D_MODEL)
(IN_AQ, IN_AK, IN_AV, IN_GQ, IN_GK, IN_GV, IN_LR, IN_GR, IN_GATE_A, IN_GATE_B, D_IN) = (
    sum(_IN_WIDTHS[:i]) for i in range(len(_IN_WIDTHS) + 1))


def _regroup_kernel(w_ref, main_ref, kv_ref, lr_ref):
    def put(dst, src, width):
        main_ref[:, dst:dst + width] = w_ref[0, :, src:src + width].astype(BF16)

    put(COL_AQ, IN_AQ, D_MODEL)
    put(COL_GV, IN_GV, D_MODEL)
    put(COL_GR, IN_GR, 3 * D_MODEL)
    put(COL_GQ, IN_GQ, 2 * GLA_HEADS * GLA_DK)
    kv_ref[...] = w_ref[0, :, IN_AK:IN_AK + 2 * KV_WIDTH].astype(BF16)
    lr_ref[...] = jnp.zeros_like(lr_ref)
    lr_ref[:, :GLA_GATE_RANK] = w_ref[0, :, IN_LR:IN_LR + GLA_GATE_RANK].astype(BF16)


def _regroup_w_in(w_in, layer, *, tr):
    return pl.pallas_call(
        _regroup_kernel,
        grid=(D_MODEL // tr,),
        in_specs=[pl.BlockSpec((1, tr, D_IN), lambda i: (layer, i, 0))],
        out_specs=[
            pl.BlockSpec((tr, PROJ_WIDTH), lambda i: (i, 0)),
            pl.BlockSpec((tr, 2 * KV_WIDTH), lambda i: (i, 0)),
            pl.BlockSpec((tr, LANES), lambda i: (i, 0)),
        ],
        out_shape=[
            jax.ShapeDtypeStruct((D_MODEL, PROJ_WIDTH), BF16),
            jax.ShapeDtypeStruct((D_MODEL, 2 * KV_WIDTH), BF16),
            jax.ShapeDtypeStruct((D_MODEL, LANES), BF16),
        ],
        compiler_params=pltpu.CompilerParams(
            dimension_semantics=("arbitrary",), vmem_limit_bytes=VMEM_LIMIT),
        name="regroup_w_in",
    )(w_in)


def _tile(n, pref):
    return pref if n % pref == 0 else n


def kernel(x, norm1_w, w_in, gla_gate_w2, gla_gate_b, attn_sinks, gla_norm_w, w_out, norm2_w,
           w_ffn_gate, w_ffn_up, w_ffn_down, final_norm_w):
    b, t, d = x.shape
    n = b * t
    depth = w_in.shape[0]
    h2 = x.reshape(n, d)
    for l in range(depth):
        w_main, w_kv, w_lr = _regroup_w_in(w_in, l, tr=128)
        proj, kv, lr = _inproj(h2, norm1_w[l].reshape(1, d), w_main, w_kv, w_lr, tm=_tile(n, 1024))
        proj3 = proj.reshape(b, t, PROJ_WIDTH)

        a = _attention(proj3, kv.reshape(b, t, 2 * KV_WIDTH), attn_sinks[l], tq=_tile(t, 512))

        w2 = jnp.pad(gla_gate_w2[l], ((0, LANES - GLA_GATE_RANK), (0, 0))).astype(BF16)
        g = _gla(proj3, lr.reshape(b, t, LANES), w2, gla_gate_b[l].reshape(1, -1),
                 gla_norm_w[l].reshape(1, GLA_DV), tb=_tile(t, 512))

        h2 = _outproj(a.reshape(n, d), g.reshape(n, d), h2, w_out[l].astype(BF16), tm=_tile(n, 512))
        h2 = _ffn(h2, norm2_w[l].reshape(1, d), w_ffn_gate[l].astype(BF16), w_ffn_up[l].astype(BF16),
                  w_ffn_down[l].astype(BF16), final_norm_w.reshape(1, d), tm=_tile(n, 1024), th=512,
                  final_norm=l == depth - 1)
    return h2.reshape(b, t, d)
```

```python
import functools

import jax
import jax.numpy as jnp
from jax import lax
from jax.experimental import pallas as pl
from jax.experimental.pallas import tpu as pltpu

F32 = jnp.float32
BF16 = jnp.bfloat16

D_MODEL = 2048
HEAD_DIM = 64
N_Q_HEADS = D_MODEL // HEAD_DIM
N_KV_HEADS = 4
GQA_GROUP = N_Q_HEADS // N_KV_HEADS
KV_WIDTH = N_KV_HEADS * HEAD_DIM
WINDOW = 128

GLA_HEADS = 4
GLA_DK = (D_MODEL // 2) // GLA_HEADS
GLA_DV = D_MODEL // GLA_HEADS
GLA_GATE_RANK = 16
GLA_GATE_NORMALIZER = 16.0
GLA_CHUNK = 64

FFN_HIDDEN = ((8 * D_MODEL // 3 + 255) // 256) * 256
RMS_EPS = 1e-6
MASK_VALUE = -1e30

LANES = 128

COL_AQ = 0
COL_GV = COL_AQ + D_MODEL
COL_GR = COL_GV + D_MODEL
COL_GATE_A = COL_GR + D_MODEL
COL_GATE_B = COL_GATE_A + D_MODEL
COL_GQ = COL_GATE_B + D_MODEL
COL_GK = COL_GQ + GLA_HEADS * GLA_DK
PROJ_WIDTH = COL_GK + GLA_HEADS * GLA_DK
PROJ_TILE = 2048
COL_KV = PROJ_WIDTH

VMEM_LIMIT = 56 * 1024 * 1024

_NT = (((1,), (1,)), ((), ()))
_TN = (((0,), (0,)), ((), ()))


def _rms_scale(x):
    return lax.rsqrt(jnp.mean(x * x, axis=-1, keepdims=True) + RMS_EPS)


def _inproj_kernel(x_ref, nw_ref, w_ref, wkv_ref, wlr_ref, proj_ref, kv_ref, lr_ref, u_ref):
    j = pl.program_id(1)

    @pl.when(j == 0)
    def _():
        x = x_ref[...]
        u = (x * _rms_scale(x) * nw_ref[...]).astype(BF16)
        u_ref[...] = u
        kv_ref[...] = lax.dot_general(u, wkv_ref[...], _NT, preferred_element_type=F32).astype(BF16)
        w_lr = wlr_ref[...].astype(BF16)
        w_lr = jnp.concatenate([w_lr, jnp.zeros((LANES - GLA_GATE_RANK, D_MODEL), BF16)], axis=0)
        lr_ref[...] = lax.dot_general(u, w_lr, _NT, preferred_element_type=F32)

    def tile():
        return lax.dot_general(u_ref[...], w_ref[...], _NT, preferred_element_type=F32)

    is_silu = (j >= COL_GR // PROJ_TILE) & (j < COL_GATE_A // PROJ_TILE)
    is_sigmoid = (j >= COL_GATE_A // PROJ_TILE) & (j < COL_GQ // PROJ_TILE)

    @pl.when(is_silu)
    def _():
        y = tile()
        proj_ref[...] = (y * jax.nn.sigmoid(y)).astype(BF16)

    @pl.when(is_sigmoid)
    def _():
        proj_ref[...] = jax.nn.sigmoid(tile()).astype(BF16)

    @pl.when(jnp.logical_not(is_silu | is_sigmoid))
    def _():
        proj_ref[...] = tile().astype(BF16)


def _inproj(x2, norm_w, w_t, w_lr_t, *, tm):
    n = x2.shape[0]
    return pl.pallas_call(
        _inproj_kernel,
        grid=(n // tm, PROJ_WIDTH // PROJ_TILE),
        in_specs=[
            pl.BlockSpec((tm, D_MODEL), lambda i, j: (i, 0)),
            pl.BlockSpec((1, D_MODEL), lambda i, j: (0, 0)),
            pl.BlockSpec((PROJ_TILE, D_MODEL), lambda i, j: (j, 0)),
            pl.BlockSpec((2 * KV_WIDTH, D_MODEL), lambda i, j: (COL_KV // (2 * KV_WIDTH), 0)),
            pl.BlockSpec((GLA_GATE_RANK, D_MODEL), lambda i, j: (0, 0)),
        ],
        out_specs=[
            pl.BlockSpec((tm, PROJ_TILE), lambda i, j: (i, j)),
            pl.BlockSpec((tm, 2 * KV_WIDTH), lambda i, j: (i, 0)),
            pl.BlockSpec((tm, LANES), lambda i, j: (i, 0)),
        ],
        out_shape=[
            jax.ShapeDtypeStruct((n, PROJ_WIDTH), BF16),
            jax.ShapeDtypeStruct((n, 2 * KV_WIDTH), BF16),
            jax.ShapeDtypeStruct((n, LANES), F32),
        ],
        scratch_shapes=[pltpu.VMEM((tm, D_MODEL), BF16)],
        compiler_params=pltpu.CompilerParams(
            dimension_semantics=("arbitrary", "arbitrary"), vmem_limit_bytes=VMEM_LIMIT),
        name="inproj",
    )(x2, norm_w, w_t, w_t, w_lr_t)


PAIR = 2 * HEAD_DIM
PAIRS_PER_GROUP = GQA_GROUP // 2
STACK = PAIRS_PER_GROUP * WINDOW


def _attn_qblock(sink_ref, q_ref, k_ref, v_ref, ga_ref, o_ref, row0, k_row0, has_prev):
    rows = pl.ds(row0, WINDOW)
    nk = 2 * WINDOW if has_prev else WINDOW
    krows = pl.ds(k_row0, nk)
    lo = lax.broadcasted_iota(jnp.int32, (1, PAIR), 1) < HEAD_DIM
    top = lax.broadcasted_iota(jnp.int32, (PAIR, 1), 0) < HEAD_DIM
    ki = lax.broadcasted_iota(jnp.int32, (WINDOW, STACK), 0)
    qi = lax.broadcasted_iota(jnp.int32, (WINDOW, STACK), 1) & (WINDOW - 1)
    upper = ki > qi

    def halves(pair_tile, g):
        swapped = pltpu.roll(pair_tile, HEAD_DIM, axis=1)
        own, other = (pair_tile, swapped) if g % 2 == 0 else (swapped, pair_tile)
        zero = jnp.zeros_like(pair_tile)
        return jnp.where(lo, own, zero), jnp.where(lo, zero, other)

    def probs(st, sink_row):
        if has_prev:
            folded = jnp.where(upper, st[:WINDOW], st[WINDOW:])
        else:
            folded = jnp.where(upper, MASK_VALUE, st)
        m = jnp.maximum(jnp.max(folded, axis=0, keepdims=True), sink_row)
        p = jnp.exp(folded - m)
        inv = 1.0 / (jnp.sum(p, axis=0, keepdims=True) + jnp.exp(sink_row - m))
        if has_prev:
            p = jnp.concatenate([jnp.where(upper, p, 0.0), jnp.where(upper, 0.0, p)], axis=0)
        return p.astype(BF16), inv

    def scores(g):
        k_lanes = slice((g // 2) * PAIR, (g // 2 + 1) * PAIR)
        k_lo, k_hi = halves(k_ref[0, krows, k_lanes], g)
        first_pair = g * PAIRS_PER_GROUP
        q = jnp.concatenate(
            [q_ref[0, rows, (first_pair + p) * PAIR:(first_pair + p + 1) * PAIR]
             for p in range(PAIRS_PER_GROUP)], axis=0) * (HEAD_DIM ** -0.5)
        return (lax.dot_general(k_lo, q, _NT, preferred_element_type=F32),
                lax.dot_general(k_hi, q, _NT, preferred_element_type=F32))

    def weighted_values(g, st_even, st_odd):
        v_lanes = slice(KV_WIDTH + (g // 2) * PAIR, KV_WIDTH + (g // 2 + 1) * PAIR)
        v_lo, v_hi = halves(v_ref[0, krows, v_lanes], g)
        first_pair = g * PAIRS_PER_GROUP
        sink_even = jnp.concatenate(
            [jnp.full((1, WINDOW), sink_ref[2 * (first_pair + p)], F32) for p in range(PAIRS_PER_GROUP)], axis=1)
        sink_odd = jnp.concatenate(
            [jnp.full((1, WINDOW), sink_ref[2 * (first_pair + p) + 1], F32) for p in range(PAIRS_PER_GROUP)],
            axis=1)
        p_even, inv_even = probs(st_even, sink_even)
        p_odd, inv_odd = probs(st_odd, sink_odd)
        ot = (lax.dot_general(v_lo, p_even, _TN, preferred_element_type=F32)
              + lax.dot_general(v_hi, p_odd, _TN, preferred_element_type=F32))
        return ot, jnp.where(top, inv_even, inv_odd)

    def finish(g, ot, inv):
        o = (ot * inv).T
        first_pair = g * PAIRS_PER_GROUP
        for p in range(PAIRS_PER_GROUP):
            cols = slice((first_pair + p) * PAIR, (first_pair + p + 1) * PAIR)
            gate = ga_ref[0, rows, cols].astype(F32)
            o_ref[0, rows, cols] = (gate * o[p * WINDOW:(p + 1) * WINDOW]).astype(BF16)

    st_next = scores(0)
    unfinished = None
    for g in range(N_KV_HEADS):
        st_even, st_odd = st_next
        if g + 1 < N_KV_HEADS:
            st_next = scores(g + 1)
        if unfinished is not None:
            finish(*unfinished)
        unfinished = (g,) + weighted_values(g, st_even, st_odd)
    finish(*unfinished)


def _attn_kernel(sink_ref, q_ref, kv_ref, ga_ref, o_ref, *, tq):
    t = pl.program_id(1)
    block = functools.partial(_attn_qblock, sink_ref, q_ref, kv_ref, kv_ref, ga_ref, o_ref)

    @pl.when(t == 0)
    def _():
        block(0, 0, has_prev=False)

    @pl.when(t > 0)
    def _():
        block(0, pl.multiple_of(t * tq - WINDOW, WINDOW), has_prev=True)

    def body(blk, carry):
        row0 = pl.multiple_of(blk * WINDOW, WINDOW)
        block(row0, pl.multiple_of(t * tq + row0 - WINDOW, WINDOW), has_prev=True)
        return carry

    lax.fori_loop(1, tq // WINDOW, body, 0)


def _attention(proj3, kv3, sinks, *, tq):
    b, t, _ = proj3.shape
    return pl.pallas_call(
        functools.partial(_attn_kernel, tq=tq),
        grid=(b, t // tq),
        in_specs=[
            pl.BlockSpec(memory_space=pltpu.SMEM),
            pl.BlockSpec((1, tq, D_MODEL), lambda i, j: (i, j, COL_AQ // D_MODEL)),
            pl.BlockSpec((1, t, 2 * KV_WIDTH), lambda i, j: (i, 0, 0)),
            pl.BlockSpec((1, tq, D_MODEL), lambda i, j: (i, j, COL_GATE_A // D_MODEL)),
        ],
        out_specs=pl.BlockSpec((1, tq, D_MODEL), lambda i, j: (i, j, 0)),
        out_shape=jax.ShapeDtypeStruct((b, t, D_MODEL), BF16),
        compiler_params=pltpu.CompilerParams(
            dimension_semantics=("arbitrary", "arbitrary"), vmem_limit_bytes=VMEM_LIMIT),
        name="swa_attention",
    )(sinks, proj3, kv3, proj3)


def _log_sigmoid(x):
    return jnp.minimum(x, 0.0) - jnp.log(1.0 + jnp.exp(-jnp.abs(x)))


GLA_SUPER = 2 * GLA_CHUNK


def _split_bf16(x):
    hi = x.astype(BF16)
    return hi, (x - hi.astype(F32)).astype(BF16)


def _gla_kernel(q_ref, k_ref, v_ref, gr_ref, gb_ref, lr_ref, w2_ref, b_ref, nw_ref, o_ref, s_ref, *, tb):
    @pl.when(pl.program_id(2) == 0)
    def _():
        s_ref[...] = jnp.zeros_like(s_ref)

    lr = lr_ref[0].astype(BF16)
    logit = jnp.dot(lr, w2_ref[...], preferred_element_type=F32) + b_ref[...]
    log_a = _log_sigmoid(logit) * (1.0 / GLA_GATE_NORMALIZER)

    ri = lax.broadcasted_iota(jnp.int32, (GLA_SUPER, GLA_SUPER), 0)
    ci = lax.broadcasted_iota(jnp.int32, (GLA_SUPER, GLA_SUPER), 1)
    causal = ri >= ci
    cum = (causal & ((ri // GLA_CHUNK) == (ci // GLA_CHUNK))).astype(BF16)
    later_and_all = jnp.concatenate([(ri > ci).astype(BF16), jnp.ones((GLA_SUPER, GLA_SUPER), BF16)], axis=1)
    in_a = lax.broadcasted_iota(jnp.int32, (GLA_SUPER, 1), 0) < GLA_CHUNK

    def intra(c):
        rows = pl.ds(c * GLA_SUPER, GLA_SUPER)
        la = log_a[c * GLA_SUPER:(c + 1) * GLA_SUPER, :]
        la_hi, la_lo = _split_bf16(la)
        g = (jnp.dot(cum, la_hi, preferred_element_type=F32)
             + jnp.dot(cum, la_lo, preferred_element_type=F32))
        lat_hi, lat_lo = _split_bf16(la.T)
        tail = (jnp.dot(lat_hi, later_and_all, preferred_element_type=F32)
                + jnp.dot(lat_lo, later_and_all, preferred_element_type=F32))
        q = q_ref[0, rows, :].astype(F32)
        k = k_ref[0, rows, :].astype(F32)
        v = v_ref[0, rows, :]
        q_dec = q * (GLA_DK ** -0.5) * jnp.exp(g)
        carry = jnp.exp(g[GLA_CHUNK - 1:GLA_CHUNK, :])
        q_carry = jnp.where(in_a, q_dec, q_dec * carry).astype(BF16)
        k_inv = k * jnp.exp(-g)
        k_inv_a = jnp.where(in_a, k_inv, 0.0).astype(BF16)
        k_inv_b = jnp.where(in_a, 0.0, k_inv).astype(BF16)
        att = (lax.dot_general(q_carry, k_inv_a, _NT, preferred_element_type=F32)
               + lax.dot_general(q_dec.astype(BF16), k_inv_b, _NT, preferred_element_type=F32))
        att = jnp.where(causal, att, 0.0).astype(BF16)
        o_intra = jnp.dot(att, v, preferred_element_type=F32)
        k_end_t = (k.T * jnp.exp(tail[:, :GLA_SUPER])).astype(BF16)
        upd = jnp.dot(k_end_t, v, preferred_element_type=F32)
        decay = jnp.exp(tail[:, GLA_SUPER:])
        return q_carry, o_intra, upd, decay

    def inter(c, q_carry, o_intra, upd, decay):
        rows = pl.ds(c * GLA_SUPER, GLA_SUPER)
        state = s_ref[...]
        o = o_intra + jnp.dot(q_carry, state.astype(BF16), preferred_element_type=F32)
        s_ref[...] = jnp.concatenate([decay] * (GLA_DV // GLA_SUPER), axis=1) * state + upd
        o = o * _rms_scale(o) * nw_ref[...]
        o = o * gr_ref[0, rows, :].astype(F32) * gb_ref[0, rows, :].astype(F32)
        o_ref[0, rows, :] = o.astype(BF16)

    n = tb // GLA_SUPER
    ahead = intra(0)
    for c in range(n):
        current = ahead
        if c + 1 < n:
            ahead = intra(c + 1)
        inter(c, *current)


def _gla(proj3, lr3, w2, bias, norm_w, *, tb):
    b, t, _ = proj3.shape
    return pl.pallas_call(
        functools.partial(_gla_kernel, tb=tb),
        grid=(b, GLA_HEADS, t // tb),
        in_specs=[
            pl.BlockSpec((1, tb, GLA_DK), lambda i, h, j: (i, j, COL_GQ // GLA_DK + h)),
            pl.BlockSpec((1, tb, GLA_DK), lambda i, h, j: (i, j, COL_GK // GLA_DK + h)),
            pl.BlockSpec((1, tb, GLA_DV), lambda i, h, j: (i, j, COL_GV // GLA_DV + h)),
            pl.BlockSpec((1, tb, GLA_DV), lambda i, h, j: (i, j, COL_GR // GLA_DV + h)),
            pl.BlockSpec((1, tb, GLA_DV), lambda i, h, j: (i, j, COL_GATE_B // GLA_DV + h)),
            pl.BlockSpec((1, tb, LANES), lambda i, h, j: (i, j, 0)),
            pl.BlockSpec((LANES, GLA_DK), lambda i, h, j: (0, h)),
            pl.BlockSpec((1, GLA_DK), lambda i, h, j: (0, h)),
            pl.BlockSpec((1, GLA_DV), lambda i, h, j: (0, 0)),
        ],
        out_specs=pl.BlockSpec((1, tb, GLA_DV), lambda i, h, j: (i, j, h)),
        out_shape=jax.ShapeDtypeStruct((b, t, D_MODEL), BF16),
        scratch_shapes=[pltpu.VMEM((GLA_DK, GLA_DV), F32)],
        compiler_params=pltpu.CompilerParams(
            dimension_semantics=("arbitrary", "arbitrary", "arbitrary"), vmem_limit_bytes=VMEM_LIMIT),
        name="gla",
    )(proj3, proj3, proj3, proj3, proj3, lr3, w2, bias, norm_w)


def _outproj_kernel(a_ref, g_ref, x_ref, w_ref, h_ref):
    merged = a_ref[...] + g_ref[...]
    h_ref[...] = x_ref[...] + jnp.dot(merged, w_ref[...], preferred_element_type=F32)


def _outproj(a2, g2, x2, w_out, *, tm):
    n = x2.shape[0]
    return pl.pallas_call(
        _outproj_kernel,
        grid=(n // tm,),
        in_specs=[
            pl.BlockSpec((tm, D_MODEL), lambda i: (i, 0)),
            pl.BlockSpec((tm, D_MODEL), lambda i: (i, 0)),
            pl.BlockSpec((tm, D_MODEL), lambda i: (i, 0)),
            pl.BlockSpec((D_MODEL, D_MODEL), lambda i: (0, 0)),
        ],
        out_specs=pl.BlockSpec((tm, D_MODEL), lambda i: (i, 0)),
        out_shape=jax.ShapeDtypeStruct((n, D_MODEL), F32),
        compiler_params=pltpu.CompilerParams(
            dimension_semantics=("arbitrary",), vmem_limit_bytes=VMEM_LIMIT),
        name="outproj",
    )(a2, g2, x2, w_out)


def _ffn_kernel(h_ref, nw_ref, wg_ref, wu_ref, wd_ref, fw_ref, o_ref, v_ref, *, final_norm):
    j = pl.program_id(1)

    @pl.when(j == 0)
    def _():
        h = h_ref[...]
        v_ref[...] = (h * _rms_scale(h) * nw_ref[...]).astype(BF16)
        o_ref[...] = h

    v = v_ref[...]
    gate = jnp.dot(v, wg_ref[...], preferred_element_type=F32)
    up = jnp.dot(v, wu_ref[...], preferred_element_type=F32)
    act = (gate * jax.nn.sigmoid(gate) * up).astype(BF16)
    o_ref[...] += jnp.dot(act, wd_ref[...], preferred_element_type=F32)

    if final_norm:
        @pl.when(j == pl.num_programs(1) - 1)
        def _():
            y = o_ref[...]
            o_ref[...] = y * _rms_scale(y) * fw_ref[...]


def _ffn(h2, norm_w, w_gate, w_up, w_down, final_w, *, tm, th, final_norm):
    n = h2.shape[0]
    return pl.pallas_call(
        functools.partial(_ffn_kernel, final_norm=final_norm),
        grid=(n // tm, FFN_HIDDEN // th),
        in_specs=[
            pl.BlockSpec((tm, D_MODEL), lambda i, j: (i, 0)),
            pl.BlockSpec((1, D_MODEL), lambda i, j: (0, 0)),
            pl.BlockSpec((D_MODEL, th), lambda i, j: (0, j)),
            pl.BlockSpec((D_MODEL, th), lambda i, j: (0, j)),
            pl.BlockSpec((th, D_MODEL), lambda i, j: (j, 0)),
            pl.BlockSpec((1, D_MODEL), lambda i, j: (0, 0)),
        ],
        out_specs=pl.BlockSpec((tm, D_MODEL), lambda i, j: (i, 0)),
        out_shape=jax.ShapeDtypeStruct((n, D_MODEL), F32),
        scratch_shapes=[pltpu.VMEM((tm, D_MODEL), BF16)],
        compiler_params=pltpu.CompilerParams(
            dimension_semantics=("arbitrary", "arbitrary"), vmem_limit_bytes=VMEM_LIMIT),
        name="ffn",
    )(h2, norm_w, w_gate, w_up, w_down, final_w)


_IN_WIDTHS = (D_MODEL, KV_WIDTH, KV_WIDTH, GLA_HEADS * GLA_DK, GLA_HEADS * GLA_DK, D_MODEL,
              GLA_GATE_RANK, D_MODEL, D_MODEL, D_MODEL)
(IN_AQ, IN_AK, IN_AV, IN_GQ, IN_GK, IN_GV, IN_LR, IN_GR, IN_GATE_A, IN_GATE_B, D_IN) = (
    sum(_IN_WIDTHS[:i]) for i in range(len(_IN_WIDTHS) + 1))


REGROUP_ROWS = 2 * KV_WIDTH


def _regroup_source_row(i):
    row = i * REGROUP_ROWS
    src = row - COL_AQ + IN_AQ
    src = jnp.where(row >= COL_GV, row - COL_GV + IN_GV, src)
    src = jnp.where(row >= COL_GR, row - COL_GR + IN_GR, src)
    src = jnp.where(row >= COL_GQ, row - COL_GQ + IN_GQ, src)
    src = jnp.where(row >= COL_KV, row - COL_KV + IN_AK, src)
    return pl.multiple_of(src, GLA_GATE_RANK)


def _regroup_kernel(wt_ref, o_ref):
    o_ref[...] = wt_ref[0].astype(BF16)


def _regroup_w_in(w_in_t, layer):
    rows = COL_KV + 2 * KV_WIDTH
    return pl.pallas_call(
        _regroup_kernel,
        grid=(rows // REGROUP_ROWS,),
        in_specs=[pl.BlockSpec((pl.Element(1), pl.Element(REGROUP_ROWS), pl.Element(D_MODEL)),
                               lambda i: (layer, _regroup_source_row(i), 0))],
        out_specs=pl.BlockSpec((REGROUP_ROWS, D_MODEL), lambda i: (i, 0)),
        out_shape=jax.ShapeDtypeStruct((rows, D_MODEL), BF16),
        compiler_params=pltpu.CompilerParams(
            dimension_semantics=("arbitrary",), vmem_limit_bytes=VMEM_LIMIT),
        name="regroup_w_in",
    )(w_in_t)


def _tile(n, pref):
    return pref if n % pref == 0 else n


def kernel(x, norm1_w, w_in, gla_gate_w2, gla_gate_b, attn_sinks, gla_norm_w, w_out, norm2_w,
           w_ffn_gate, w_ffn_up, w_ffn_down, final_norm_w):
    b, t, d = x.shape
    n = b * t
    depth = w_in.shape[0]
    h2 = x.reshape(n, d)
    for l in range(depth):
        w_in_t = jnp.swapaxes(w_in, 1, 2)
        proj, kv, lr = _inproj(h2, norm1_w[l].reshape(1, d), _regroup_w_in(w_in_t, l),
                               w_in_t[l, IN_LR:IN_LR + GLA_GATE_RANK], tm=_tile(n, 1024))
        proj3 = proj.reshape(b, t, PROJ_WIDTH)

        a = _attention(proj3, kv.reshape(b, t, 2 * KV_WIDTH), attn_sinks[l], tq=_tile(t, 512))

        w2 = jnp.pad(gla_gate_w2[l], ((0, LANES - GLA_GATE_RANK), (0, 0))).astype(BF16)
        g = _gla(proj3, lr.reshape(b, t, LANES), w2, gla_gate_b[l].reshape(1, -1),
                 gla_norm_w[l].reshape(1, GLA_DV), tb=_tile(t, 512))

        h2 = _outproj(a.reshape(n, d), g.reshape(n, d), h2, w_out[l].astype(BF16), tm=_tile(n, 512))
        h2 = _ffn(h2, norm2_w[l].reshape(1, d), w_ffn_gate[l].astype(BF16), w_ffn_up[l].astype(BF16),
                  w_ffn_down[l].astype(BF16), final_norm_w.reshape(1, d), tm=_tile(n, 1024), th=512,
                  final_norm=l == depth - 1)
    return h2.reshape(b, t, d)
```

```python
import functools

import jax
import jax.numpy as jnp
from jax import lax
from jax.experimental import pallas as pl
from jax.experimental.pallas import tpu as pltpu

F32 = jnp.float32
BF16 = jnp.bfloat16

D_MODEL = 2048
HEAD_DIM = 64
N_Q_HEADS = D_MODEL // HEAD_DIM
N_KV_HEADS = 4
GQA_GROUP = N_Q_HEADS // N_KV_HEADS
KV_WIDTH = N_KV_HEADS * HEAD_DIM
WINDOW = 128

GLA_HEADS = 4
GLA_DK = (D_MODEL // 2) // GLA_HEADS
GLA_DV = D_MODEL // GLA_HEADS
GLA_GATE_RANK = 16
GLA_GATE_NORMALIZER = 16.0
GLA_CHUNK = 64

FFN_HIDDEN = ((8 * D_MODEL // 3 + 255) // 256) * 256
RMS_EPS = 1e-6
MASK_VALUE = -1e30

LANES = 128

COL_AQ = 0
COL_GV = COL_AQ + D_MODEL
COL_GR = COL_GV + D_MODEL
COL_GATE_A = COL_GR + D_MODEL
COL_GATE_B = COL_GATE_A + D_MODEL
COL_GQ = COL_GATE_B + D_MODEL
COL_GK = COL_GQ + GLA_HEADS * GLA_DK
PROJ_WIDTH = COL_GK + GLA_HEADS * GLA_DK
PROJ_TILE = 2048
COL_KV = PROJ_WIDTH

VMEM_LIMIT = 56 * 1024 * 1024

_NT = (((1,), (1,)), ((), ()))
_TN = (((0,), (0,)), ((), ()))


def _rms_scale(x):
    return lax.rsqrt(jnp.mean(x * x, axis=-1, keepdims=True) + RMS_EPS)


def _inproj_kernel(x_ref, nw_ref, w_ref, wkv_ref, wlr_ref, proj_ref, kv_ref, lr_ref, u_ref):
    j = pl.program_id(1)

    @pl.when(j == 0)
    def _():
        x = x_ref[...]
        u = (x * _rms_scale(x) * nw_ref[...]).astype(BF16)
        u_ref[...] = u
        kv_ref[...] = lax.dot_general(u, wkv_ref[...], _NT, preferred_element_type=F32).astype(BF16)
        w_lr = wlr_ref[...].astype(BF16)
        w_lr = jnp.concatenate([w_lr, jnp.zeros((LANES - GLA_GATE_RANK, D_MODEL), BF16)], axis=0)
        lr_ref[...] = lax.dot_general(u, w_lr, _NT, preferred_element_type=F32)

    def tile():
        return lax.dot_general(u_ref[...], w_ref[...], _NT, preferred_element_type=F32)

    is_silu = (j >= COL_GR // PROJ_TILE) & (j < COL_GATE_A // PROJ_TILE)
    is_sigmoid = (j >= COL_GATE_A // PROJ_TILE) & (j < COL_GQ // PROJ_TILE)

    @pl.when(is_silu)
    def _():
        y = tile()
        proj_ref[...] = (y * jax.nn.sigmoid(y)).astype(BF16)

    @pl.when(is_sigmoid)
    def _():
        proj_ref[...] = jax.nn.sigmoid(tile()).astype(BF16)

    @pl.when(jnp.logical_not(is_silu | is_sigmoid))
    def _():
        proj_ref[...] = tile().astype(BF16)


def _inproj(x2, norm_w, w_t, w_lr_t, *, tm):
    n = x2.shape[0]
    return pl.pallas_call(
        _inproj_kernel,
        grid=(n // tm, PROJ_WIDTH // PROJ_TILE),
        in_specs=[
            pl.BlockSpec((tm, D_MODEL), lambda i, j: (i, 0)),
            pl.BlockSpec((1, D_MODEL), lambda i, j: (0, 0)),
            pl.BlockSpec((PROJ_TILE, D_MODEL), lambda i, j: (j, 0)),
            pl.BlockSpec((2 * KV_WIDTH, D_MODEL), lambda i, j: (COL_KV // (2 * KV_WIDTH), 0)),
            pl.BlockSpec((GLA_GATE_RANK, D_MODEL), lambda i, j: (0, 0)),
        ],
        out_specs=[
            pl.BlockSpec((tm, PROJ_TILE), lambda i, j: (i, j)),
            pl.BlockSpec((tm, 2 * KV_WIDTH), lambda i, j: (i, 0)),
            pl.BlockSpec((tm, LANES), lambda i, j: (i, 0)),
        ],
        out_shape=[
            jax.ShapeDtypeStruct((n, PROJ_WIDTH), BF16),
            jax.ShapeDtypeStruct((n, 2 * KV_WIDTH), BF16),
            jax.ShapeDtypeStruct((n, LANES), F32),
        ],
        scratch_shapes=[pltpu.VMEM((tm, D_MODEL), BF16)],
        compiler_params=pltpu.CompilerParams(
            dimension_semantics=("arbitrary", "arbitrary"), vmem_limit_bytes=VMEM_LIMIT),
        name="inproj",
    )(x2, norm_w, w_t, w_t, w_lr_t)


PAIR = 2 * HEAD_DIM
PAIRS_PER_GROUP = GQA_GROUP // 2
STACK = PAIRS_PER_GROUP * WINDOW


def _attn_qblock(sink_ref, q_ref, k_ref, v_ref, ga_ref, o_ref, row0, k_row0, has_prev):
    rows = pl.ds(row0, WINDOW)
    nk = 2 * WINDOW if has_prev else WINDOW
    krows = pl.ds(k_row0, nk)
    lo = lax.broadcasted_iota(jnp.int32, (1, PAIR), 1) < HEAD_DIM
    top = lax.broadcasted_iota(jnp.int32, (PAIR, 1), 0) < HEAD_DIM
    ki = lax.broadcasted_iota(jnp.int32, (WINDOW, STACK), 0)
    qi = lax.broadcasted_iota(jnp.int32, (WINDOW, STACK), 1) & (WINDOW - 1)
    upper = ki > qi

    def halves(pair_tile, g):
        swapped = pltpu.roll(pair_tile, HEAD_DIM, axis=1)
        own, other = (pair_tile, swapped) if g % 2 == 0 else (swapped, pair_tile)
        zero = jnp.zeros_like(pair_tile)
        return jnp.where(lo, own, zero), jnp.where(lo, zero, other)

    def probs(st, sink_row):
        if has_prev:
            folded = jnp.where(upper, st[:WINDOW], st[WINDOW:])
        else:
            folded = jnp.where(upper, MASK_VALUE, st)
        m = jnp.maximum(jnp.max(folded, axis=0, keepdims=True), sink_row)
        p = jnp.exp(folded - m)
        inv = 1.0 / (jnp.sum(p, axis=0, keepdims=True) + jnp.exp(sink_row - m))
        if has_prev:
            p = jnp.concatenate([jnp.where(upper, p, 0.0), jnp.where(upper, 0.0, p)], axis=0)
        return p.astype(BF16), inv

    def scores(g):
        k_lanes = slice((g // 2) * PAIR, (g // 2 + 1) * PAIR)
        k_lo, k_hi = halves(k_ref[0, krows, k_lanes], g)
        first_pair = g * PAIRS_PER_GROUP
        q = jnp.concatenate(
            [q_ref[0, rows, (first_pair + p) * PAIR:(first_pair + p + 1) * PAIR]
             for p in range(PAIRS_PER_GROUP)], axis=0) * (HEAD_DIM ** -0.5)
        return (lax.dot_general(k_lo, q, _NT, preferred_element_type=F32),
                lax.dot_general(k_hi, q, _NT, preferred_element_type=F32))

    def weighted_values(g, st_even, st_odd):
        v_lanes = slice(KV_WIDTH + (g // 2) * PAIR, KV_WIDTH + (g // 2 + 1) * PAIR)
        v_lo, v_hi = halves(v_ref[0, krows, v_lanes], g)
        first_pair = g * PAIRS_PER_GROUP
        sink_even = jnp.concatenate(
            [jnp.full((1, WINDOW), sink_ref[2 * (first_pair + p)], F32) for p in range(PAIRS_PER_GROUP)], axis=1)
        sink_odd = jnp.concatenate(
            [jnp.full((1, WINDOW), sink_ref[2 * (first_pair + p) + 1], F32) for p in range(PAIRS_PER_GROUP)],
            axis=1)
        p_even, inv_even = probs(st_even, sink_even)
        p_odd, inv_odd = probs(st_odd, sink_odd)
        ot = (lax.dot_general(v_lo, p_even, _TN, preferred_element_type=F32)
              + lax.dot_general(v_hi, p_odd, _TN, preferred_element_type=F32))
        return ot, jnp.where(top, inv_even, inv_odd)

    def finish(g, ot, inv):
        o = (ot * inv).T
        first_pair = g * PAIRS_PER_GROUP
        for p in range(PAIRS_PER_GROUP):
            cols = slice((first_pair + p) * PAIR, (first_pair + p + 1) * PAIR)
            gate = ga_ref[0, rows, cols].astype(F32)
            o_ref[0, rows, cols] = (gate * o[p * WINDOW:(p + 1) * WINDOW]).astype(BF16)

    st_next = scores(0)
    unfinished = None
    for g in range(N_KV_HEADS):
        st_even, st_odd = st_next
        if g + 1 < N_KV_HEADS:
            st_next = scores(g + 1)
        if unfinished is not None:
            finish(*unfinished)
        unfinished = (g,) + weighted_values(g, st_even, st_odd)
    finish(*unfinished)


def _attn_kernel(sink_ref, q_ref, kv_ref, ga_ref, *rest, tq, n_cast):
    cast_in, cast_out, o_ref = rest[:n_cast], rest[n_cast:2 * n_cast], rest[2 * n_cast]
    for src, dst in zip(cast_in, cast_out):
        dst[...] = src[0].astype(BF16)

    t = pl.program_id(1)
    block = functools.partial(_attn_qblock, sink_ref, q_ref, kv_ref, kv_ref, ga_ref, o_ref)

    @pl.when(t == 0)
    def _():
        block(0, 0, has_prev=False)

    @pl.when(t > 0)
    def _():
        block(0, pl.multiple_of(t * tq - WINDOW, WINDOW), has_prev=True)

    def body(blk, carry):
        row0 = pl.multiple_of(blk * WINDOW, WINDOW)
        block(row0, pl.multiple_of(t * tq + row0 - WINDOW, WINDOW), has_prev=True)
        return carry

    lax.fori_loop(1, tq // WINDOW, body, 0)


def _attention(proj3, kv3, sinks, layer, weights, *, tq):
    b, t, _ = proj3.shape
    steps = b * (t // tq)
    cast_in, cast_out, cast_shapes = [], [], []
    for w in weights:
        _, rows, cols = w.shape
        assert rows % (16 * steps) == 0, (rows, steps)
        cast_in.append(pl.BlockSpec((1, rows // steps, cols), lambda i, j: (layer, i * (t // tq) + j, 0)))
        cast_out.append(pl.BlockSpec((rows // steps, cols), lambda i, j: (i * (t // tq) + j, 0)))
        cast_shapes.append(jax.ShapeDtypeStruct((rows, cols), BF16))
    return pl.pallas_call(
        functools.partial(_attn_kernel, tq=tq, n_cast=len(weights)),
        grid=(b, t // tq),
        in_specs=[
            pl.BlockSpec(memory_space=pltpu.SMEM),
            pl.BlockSpec((1, tq, D_MODEL), lambda i, j: (i, j, COL_AQ // D_MODEL)),
            pl.BlockSpec((1, t, 2 * KV_WIDTH), lambda i, j: (i, 0, 0)),
            pl.BlockSpec((1, tq, D_MODEL), lambda i, j: (i, j, COL_GATE_A // D_MODEL)),
        ] + cast_in,
        out_specs=cast_out + [pl.BlockSpec((1, tq, D_MODEL), lambda i, j: (i, j, 0))],
        out_shape=cast_shapes + [jax.ShapeDtypeStruct((b, t, D_MODEL), BF16)],
        compiler_params=pltpu.CompilerParams(
            dimension_semantics=("arbitrary", "arbitrary"), vmem_limit_bytes=VMEM_LIMIT),
        name="swa_attention",
    )(sinks, proj3, kv3, proj3, *weights)


def _log_sigmoid(x):
    return jnp.minimum(x, 0.0) - jnp.log(1.0 + jnp.exp(-jnp.abs(x)))


GLA_SUPER = 2 * GLA_CHUNK


def _split_bf16(x):
    hi = x.astype(BF16)
    return hi, (x - hi.astype(F32)).astype(BF16)


def _gla_kernel(q_ref, k_ref, v_ref, gr_ref, gb_ref, lr_ref, w2_ref, b_ref, nw_ref, o_ref, s_ref, *, tb):
    @pl.when(pl.program_id(2) == 0)
    def _():
        s_ref[...] = jnp.zeros_like(s_ref)

    lr = lr_ref[0].astype(BF16)
    logit = jnp.dot(lr, w2_ref[...], preferred_element_type=F32) + b_ref[...]
    log_a = _log_sigmoid(logit) * (1.0 / GLA_GATE_NORMALIZER)

    ri = lax.broadcasted_iota(jnp.int32, (GLA_SUPER, GLA_SUPER), 0)
    ci = lax.broadcasted_iota(jnp.int32, (GLA_SUPER, GLA_SUPER), 1)
    causal = ri >= ci
    cum = (causal & ((ri // GLA_CHUNK) == (ci // GLA_CHUNK))).astype(BF16)
    later_and_all = jnp.concatenate([(ri > ci).astype(BF16), jnp.ones((GLA_SUPER, GLA_SUPER), BF16)], axis=1)
    in_a = lax.broadcasted_iota(jnp.int32, (GLA_SUPER, 1), 0) < GLA_CHUNK

    def intra(c):
        rows = pl.ds(c * GLA_SUPER, GLA_SUPER)
        la = log_a[c * GLA_SUPER:(c + 1) * GLA_SUPER, :]
        la_hi, la_lo = _split_bf16(la)
        g = (jnp.dot(cum, la_hi, preferred_element_type=F32)
             + jnp.dot(cum, la_lo, preferred_element_type=F32))
        lat_hi, lat_lo = _split_bf16(la.T)
        tail = (jnp.dot(lat_hi, later_and_all, preferred_element_type=F32)
                + jnp.dot(lat_lo, later_and_all, preferred_element_type=F32))
        q = q_ref[0, rows, :].astype(F32)
        k = k_ref[0, rows, :].astype(F32)
        v = v_ref[0, rows, :]
        q_dec = q * (GLA_DK ** -0.5) * jnp.exp(g)
        carry = jnp.exp(g[GLA_CHUNK - 1:GLA_CHUNK, :])
        q_carry = jnp.where(in_a, q_dec, q_dec * carry).astype(BF16)
        k_inv = k * jnp.exp(-g)
        k_inv_a = jnp.where(in_a, k_inv, 0.0).astype(BF16)
        k_inv_b = jnp.where(in_a, 0.0, k_inv).astype(BF16)
        att = (lax.dot_general(q_carry, k_inv_a, _NT, preferred_element_type=F32)
               + lax.dot_general(q_dec.astype(BF16), k_inv_b, _NT, preferred_element_type=F32))
        att = jnp.where(causal, att, 0.0).astype(BF16)
        o_intra = jnp.dot(att, v, preferred_element_type=F32)
        k_end_t = (k.T * jnp.exp(tail[:, :GLA_SUPER])).astype(BF16)
        upd = jnp.dot(k_end_t, v, preferred_element_type=F32)
        decay = jnp.exp(tail[:, GLA_SUPER:])
        return q_carry, o_intra, upd, decay

    def inter(c, q_carry, o_intra, upd, decay):
        rows = pl.ds(c * GLA_SUPER, GLA_SUPER)
        state = s_ref[...]
        o = o_intra + jnp.dot(q_carry, state.astype(BF16), preferred_element_type=F32)
        s_ref[...] = jnp.concatenate([decay] * (GLA_DV // GLA_SUPER), axis=1) * state + upd
        o = o * _rms_scale(o) * nw_ref[...]
        o = o * gr_ref[0, rows, :].astype(F32) * gb_ref[0, rows, :].astype(F32)
        o_ref[0, rows, :] = o.astype(BF16)

    n = tb // GLA_SUPER
    ahead = intra(0)
    for c in range(n):
        current = ahead
        if c + 1 < n:
            ahead = intra(c + 1)
        inter(c, *current)


def _gla(proj3, lr3, w2, bias, norm_w, *, tb):
    b, t, _ = proj3.shape
    return pl.pallas_call(
        functools.partial(_gla_kernel, tb=tb),
        grid=(b, GLA_HEADS, t // tb),
        in_specs=[
            pl.BlockSpec((1, tb, GLA_DK), lambda i, h, j: (i, j, COL_GQ // GLA_DK + h)),
            pl.BlockSpec((1, tb, GLA_DK), lambda i, h, j: (i, j, COL_GK // GLA_DK + h)),
            pl.BlockSpec((1, tb, GLA_DV), lambda i, h, j: (i, j, COL_GV // GLA_DV + h)),
            pl.BlockSpec((1, tb, GLA_DV), lambda i, h, j: (i, j, COL_GR // GLA_DV + h)),
            pl.BlockSpec((1, tb, GLA_DV), lambda i, h, j: (i, j, COL_GATE_B // GLA_DV + h)),
            pl.BlockSpec((1, tb, LANES), lambda i, h, j: (i, j, 0)),
            pl.BlockSpec((LANES, GLA_DK), lambda i, h, j: (0, h)),
            pl.BlockSpec((1, GLA_DK), lambda i, h, j: (0, h)),
            pl.BlockSpec((1, GLA_DV), lambda i, h, j: (0, 0)),
        ],
        out_specs=pl.BlockSpec((1, tb, GLA_DV), lambda i, h, j: (i, j, h)),
        out_shape=jax.ShapeDtypeStruct((b, t, D_MODEL), BF16),
        scratch_shapes=[pltpu.VMEM((GLA_DK, GLA_DV), F32)],
        compiler_params=pltpu.CompilerParams(
            dimension_semantics=("arbitrary", "arbitrary", "arbitrary"), vmem_limit_bytes=VMEM_LIMIT),
        name="gla",
    )(proj3, proj3, proj3, proj3, proj3, lr3, w2, bias, norm_w)


def _outproj_kernel(a_ref, g_ref, x_ref, w_ref, h_ref):
    merged = a_ref[...] + g_ref[...]
    h_ref[...] = x_ref[...] + jnp.dot(merged, w_ref[...], preferred_element_type=F32)


def _outproj(a2, g2, x2, w_out, *, tm):
    n = x2.shape[0]
    return pl.pallas_call(
        _outproj_kernel,
        grid=(n // tm,),
        in_specs=[
            pl.BlockSpec((tm, D_MODEL), lambda i: (i, 0)),
            pl.BlockSpec((tm, D_MODEL), lambda i: (i, 0)),
            pl.BlockSpec((tm, D_MODEL), lambda i: (i, 0)),
            pl.BlockSpec((D_MODEL, D_MODEL), lambda i: (0, 0)),
        ],
        out_specs=pl.BlockSpec((tm, D_MODEL), lambda i: (i, 0)),
        out_shape=jax.ShapeDtypeStruct((n, D_MODEL), F32),
        compiler_params=pltpu.CompilerParams(
            dimension_semantics=("arbitrary",), vmem_limit_bytes=VMEM_LIMIT),
        name="outproj",
    )(a2, g2, x2, w_out)


def _ffn_kernel(h_ref, nw_ref, wg_ref, wu_ref, wd_ref, fw_ref, o_ref, v_ref, *, final_norm):
    j = pl.program_id(1)

    @pl.when(j == 0)
    def _():
        h = h_ref[...]
        v_ref[...] = (h * _rms_scale(h) * nw_ref[...]).astype(BF16)
        o_ref[...] = h

    v = v_ref[...]
    gate = jnp.dot(v, wg_ref[...], preferred_element_type=F32)
    up = jnp.dot(v, wu_ref[...], preferred_element_type=F32)
    act = (gate * jax.nn.sigmoid(gate) * up).astype(BF16)
    o_ref[...] += jnp.dot(act, wd_ref[...], preferred_element_type=F32)

    if final_norm:
        @pl.when(j == pl.num_programs(1) - 1)
        def _():
            y = o_ref[...]
            o_ref[...] = y * _rms_scale(y) * fw_ref[...]


def _ffn(h2, norm_w, w_gate, w_up, w_down, final_w, *, tm, th, final_norm):
    n = h2.shape[0]
    return pl.pallas_call(
        functools.partial(_ffn_kernel, final_norm=final_norm),
        grid=(n // tm, FFN_HIDDEN // th),
        in_specs=[
            pl.BlockSpec((tm, D_MODEL), lambda i, j: (i, 0)),
            pl.BlockSpec((1, D_MODEL), lambda i, j: (0, 0)),
            pl.BlockSpec((D_MODEL, th), lambda i, j: (0, j)),
            pl.BlockSpec((D_MODEL, th), lambda i, j: (0, j)),
            pl.BlockSpec((th, D_MODEL), lambda i, j: (j, 0)),
            pl.BlockSpec((1, D_MODEL), lambda i, j: (0, 0)),
        ],
        out_specs=pl.BlockSpec((tm, D_MODEL), lambda i, j: (i, 0)),
        out_shape=jax.ShapeDtypeStruct((n, D_MODEL), F32),
        scratch_shapes=[pltpu.VMEM((tm, D_MODEL), BF16)],
        compiler_params=pltpu.CompilerParams(
            dimension_semantics=("arbitrary", "arbitrary"), vmem_limit_bytes=VMEM_LIMIT),
        name="ffn",
    )(h2, norm_w, w_gate, w_up, w_down, final_w)


_IN_WIDTHS = (D_MODEL, KV_WIDTH, KV_WIDTH, GLA_HEADS * GLA_DK, GLA_HEADS * GLA_DK, D_MODEL,
              GLA_GATE_RANK, D_MODEL, D_MODEL, D_MODEL)
(IN_AQ, IN_AK, IN_AV, IN_GQ, IN_GK, IN_GV, IN_LR, IN_GR, IN_GATE_A, IN_GATE_B, D_IN) = (
    sum(_IN_WIDTHS[:i]) for i in range(len(_IN_WIDTHS) + 1))


REGROUP_ROWS = 2 * KV_WIDTH


def _regroup_source_row(i):
    row = i * REGROUP_ROWS
    src = row - COL_AQ + IN_AQ
    src = jnp.where(row >= COL_GV, row - COL_GV + IN_GV, src)
    src = jnp.where(row >= COL_GR, row - COL_GR + IN_GR, src)
    src = jnp.where(row >= COL_GQ, row - COL_GQ + IN_GQ, src)
    src = jnp.where(row >= COL_KV, row - COL_KV + IN_AK, src)
    return pl.multiple_of(src, GLA_GATE_RANK)


def _regroup_kernel(wt_ref, o_ref):
    o_ref[...] = wt_ref[0].astype(BF16)


def _regroup_w_in(w_in_t, layer):
    rows = COL_KV + 2 * KV_WIDTH
    return pl.pallas_call(
        _regroup_kernel,
        grid=(rows // REGROUP_ROWS,),
        in_specs=[pl.BlockSpec((pl.Element(1), pl.Element(REGROUP_ROWS), pl.Element(D_MODEL)),
                               lambda i: (layer, _regroup_source_row(i), 0))],
        out_specs=pl.BlockSpec((REGROUP_ROWS, D_MODEL), lambda i: (i, 0)),
        out_shape=jax.ShapeDtypeStruct((rows, D_MODEL), BF16),
        compiler_params=pltpu.CompilerParams(
            dimension_semantics=("arbitrary",), vmem_limit_bytes=VMEM_LIMIT),
        name="regroup_w_in",
    )(w_in_t)


def _tile(n, pref):
    return pref if n % pref == 0 else n


def kernel(x, norm1_w, w_in, gla_gate_w2, gla_gate_b, attn_sinks, gla_norm_w, w_out, norm2_w,
           w_ffn_gate, w_ffn_up, w_ffn_down, final_norm_w):
    b, t, d = x.shape
    n = b * t
    depth = w_in.shape[0]
    h2 = x.reshape(n, d)
    for l in range(depth):
        w_in_t = jnp.swapaxes(w_in, 1, 2)
        proj, kv, lr = _inproj(h2, norm1_w[l].reshape(1, d), _regroup_w_in(w_in_t, l),
                               w_in_t[l, IN_LR:IN_LR + GLA_GATE_RANK], tm=_tile(n, 1024))
        proj3 = proj.reshape(b, t, PROJ_WIDTH)

        wo, wg, wu, wd, a = _attention(proj3, kv.reshape(b, t, 2 * KV_WIDTH), attn_sinks[l], l,
                                       (w_out, w_ffn_gate, w_ffn_up, w_ffn_down), tq=_tile(t, 512))

        w2 = jnp.pad(gla_gate_w2[l], ((0, LANES - GLA_GATE_RANK), (0, 0))).astype(BF16)
        g = _gla(proj3, lr.reshape(b, t, LANES), w2, gla_gate_b[l].reshape(1, -1),
                 gla_norm_w[l].reshape(1, GLA_DV), tb=_tile(t, 512))

        h2 = _outproj(a.reshape(n, d), g.reshape(n, d), h2, wo, tm=_tile(n, 512))
        h2 = _ffn(h2, norm2_w[l].reshape(1, d), wg, wu, wd, final_norm_w.reshape(1, d),
                  tm=_tile(n, 1024), th=512, final_norm=l == depth - 1)
    return h2.reshape(b, t, d)
```

```python
import functools

import jax
import jax.numpy as jnp
from jax import lax
from jax.experimental import pallas as pl
from jax.experimental.pallas import tpu as pltpu

F32 = jnp.float32
BF16 = jnp.bfloat16

D_MODEL = 2048
HEAD_DIM = 64
N_Q_HEADS = D_MODEL // HEAD_DIM
N_KV_HEADS = 4
GQA_GROUP = N_Q_HEADS // N_KV_HEADS
KV_WIDTH = N_KV_HEADS * HEAD_DIM
WINDOW = 128

GLA_HEADS = 4
GLA_DK = (D_MODEL // 2) // GLA_HEADS
GLA_DV = D_MODEL // GLA_HEADS
GLA_GATE_RANK = 16
GLA_GATE_NORMALIZER = 16.0
GLA_CHUNK = 64

FFN_HIDDEN = ((8 * D_MODEL // 3 + 255) // 256) * 256
RMS_EPS = 1e-6
MASK_VALUE = -1e30

LANES = 128

COL_AQ = 0
COL_GV = COL_AQ + D_MODEL
COL_GR = COL_GV + D_MODEL
COL_GATE_A = COL_GR + D_MODEL
COL_GATE_B = COL_GATE_A + D_MODEL
COL_GQ = COL_GATE_B + D_MODEL
COL_GK = COL_GQ + GLA_HEADS * GLA_DK
PROJ_WIDTH = COL_GK + GLA_HEADS * GLA_DK
PROJ_TILE = 2048
COL_KV = PROJ_WIDTH

VMEM_LIMIT = 56 * 1024 * 1024

_NT = (((1,), (1,)), ((), ()))
_TN = (((0,), (0,)), ((), ()))


def _rms_scale(x):
    return lax.rsqrt(jnp.mean(x * x, axis=-1, keepdims=True) + RMS_EPS)


def _inproj_kernel(x_ref, nw_ref, w_ref, wkv_ref, wlr_ref, proj_ref, kv_ref, lr_ref, u_ref):
    j = pl.program_id(1)

    @pl.when(j == 0)
    def _():
        x = x_ref[...]
        u = (x * _rms_scale(x) * nw_ref[...]).astype(BF16)
        u_ref[...] = u
        kv_ref[...] = lax.dot_general(u, wkv_ref[...], _NT, preferred_element_type=F32).astype(BF16)
        w_lr = wlr_ref[...].astype(BF16)
        w_lr = jnp.concatenate([w_lr, jnp.zeros((LANES - GLA_GATE_RANK, D_MODEL), BF16)], axis=0)
        lr_ref[...] = lax.dot_general(u, w_lr, _NT, preferred_element_type=F32)

    def tile():
        return lax.dot_general(u_ref[...], w_ref[...], _NT, preferred_element_type=F32)

    is_silu = (j >= COL_GR // PROJ_TILE) & (j < COL_GATE_A // PROJ_TILE)
    is_sigmoid = (j >= COL_GATE_A // PROJ_TILE) & (j < COL_GQ // PROJ_TILE)

    @pl.when(is_silu)
    def _():
        y = tile()
        proj_ref[...] = (y * jax.nn.sigmoid(y)).astype(BF16)

    @pl.when(is_sigmoid)
    def _():
        proj_ref[...] = jax.nn.sigmoid(tile()).astype(BF16)

    @pl.when(jnp.logical_not(is_silu | is_sigmoid))
    def _():
        proj_ref[...] = tile().astype(BF16)


def _inproj(x2, norm_w, w_t, w_lr_t, *, tm):
    n = x2.shape[0]
    return pl.pallas_call(
        _inproj_kernel,
        grid=(n // tm, PROJ_WIDTH // PROJ_TILE),
        in_specs=[
            pl.BlockSpec((tm, D_MODEL), lambda i, j: (i, 0)),
            pl.BlockSpec((1, D_MODEL), lambda i, j: (0, 0)),
            pl.BlockSpec((PROJ_TILE, D_MODEL), lambda i, j: (j, 0)),
            pl.BlockSpec((2 * KV_WIDTH, D_MODEL), lambda i, j: (COL_KV // (2 * KV_WIDTH), 0)),
            pl.BlockSpec((GLA_GATE_RANK, D_MODEL), lambda i, j: (0, 0)),
        ],
        out_specs=[
            pl.BlockSpec((tm, PROJ_TILE), lambda i, j: (i, j)),
            pl.BlockSpec((tm, 2 * KV_WIDTH), lambda i, j: (i, 0)),
            pl.BlockSpec((tm, LANES), lambda i, j: (i, 0)),
        ],
        out_shape=[
            jax.ShapeDtypeStruct((n, PROJ_WIDTH), BF16),
            jax.ShapeDtypeStruct((n, 2 * KV_WIDTH), BF16),
            jax.ShapeDtypeStruct((n, LANES), F32),
        ],
        scratch_shapes=[pltpu.VMEM((tm, D_MODEL), BF16)],
        compiler_params=pltpu.CompilerParams(
            dimension_semantics=("arbitrary", "arbitrary"), vmem_limit_bytes=VMEM_LIMIT),
        name="inproj",
    )(x2, norm_w, w_t, w_t, w_lr_t)


PAIR = 2 * HEAD_DIM
PAIRS_PER_GROUP = GQA_GROUP // 2
STACK = PAIRS_PER_GROUP * WINDOW


def _attn_blocks(sink_ref, q_ref, kv_ref, ga_ref, o_ref, blocks):
    lo = lax.broadcasted_iota(jnp.int32, (1, PAIR), 1) < HEAD_DIM
    top = lax.broadcasted_iota(jnp.int32, (PAIR, 1), 0) < HEAD_DIM
    ki = lax.broadcasted_iota(jnp.int32, (WINDOW, STACK), 0)
    qi = lax.broadcasted_iota(jnp.int32, (WINDOW, STACK), 1) & (WINDOW - 1)
    upper = ki > qi

    def rows(blk):
        return pl.ds(blocks[blk][0], WINDOW)

    def krows(blk):
        _, k_row0, has_prev = blocks[blk]
        return pl.ds(k_row0, 2 * WINDOW if has_prev else WINDOW)

    def halves(pair_tile, g):
        swapped = pltpu.roll(pair_tile, HEAD_DIM, axis=1)
        own, other = (pair_tile, swapped) if g % 2 == 0 else (swapped, pair_tile)
        zero = jnp.zeros_like(pair_tile)
        return jnp.where(lo, own, zero), jnp.where(lo, zero, other)

    def probs(st, sink_row, has_prev):
        if has_prev:
            folded = jnp.where(upper, st[:WINDOW], st[WINDOW:])
        else:
            folded = jnp.where(upper, MASK_VALUE, st)
        m = jnp.maximum(jnp.max(folded, axis=0, keepdims=True), sink_row)
        p = jnp.exp(folded - m)
        inv = 1.0 / (jnp.sum(p, axis=0, keepdims=True) + jnp.exp(sink_row - m))
        if has_prev:
            p = jnp.concatenate([jnp.where(upper, p, 0.0), jnp.where(upper, 0.0, p)], axis=0)
        return p.astype(BF16), inv

    def scores(blk, g):
        k_lanes = slice((g // 2) * PAIR, (g // 2 + 1) * PAIR)
        k_lo, k_hi = halves(kv_ref[0, krows(blk), k_lanes], g)
        first_pair = g * PAIRS_PER_GROUP
        q = jnp.concatenate(
            [q_ref[0, rows(blk), (first_pair + p) * PAIR:(first_pair + p + 1) * PAIR]
             for p in range(PAIRS_PER_GROUP)], axis=0) * (HEAD_DIM ** -0.5)
        return (lax.dot_general(k_lo, q, _NT, preferred_element_type=F32),
                lax.dot_general(k_hi, q, _NT, preferred_element_type=F32))

    def weighted_values(blk, g, st_even, st_odd):
        v_lanes = slice(KV_WIDTH + (g // 2) * PAIR, KV_WIDTH + (g // 2 + 1) * PAIR)
        v_lo, v_hi = halves(kv_ref[0, krows(blk), v_lanes], g)
        first_pair = g * PAIRS_PER_GROUP
        sink_even = jnp.concatenate(
            [jnp.full((1, WINDOW), sink_ref[2 * (first_pair + p)], F32) for p in range(PAIRS_PER_GROUP)], axis=1)
        sink_odd = jnp.concatenate(
            [jnp.full((1, WINDOW), sink_ref[2 * (first_pair + p) + 1], F32) for p in range(PAIRS_PER_GROUP)],
            axis=1)
        p_even, inv_even = probs(st_even, sink_even, blocks[blk][2])
        p_odd, inv_odd = probs(st_odd, sink_odd, blocks[blk][2])
        ot = (lax.dot_general(v_lo, p_even, _TN, preferred_element_type=F32)
              + lax.dot_general(v_hi, p_odd, _TN, preferred_element_type=F32))
        return ot, jnp.where(top, inv_even, inv_odd)

    def finish(blk, g, ot, inv):
        o = (ot * inv).T
        first_pair = g * PAIRS_PER_GROUP
        for p in range(PAIRS_PER_GROUP):
            cols = slice((first_pair + p) * PAIR, (first_pair + p + 1) * PAIR)
            gate = ga_ref[0, rows(blk), cols].astype(F32)
            o_ref[0, rows(blk), cols] = (gate * o[p * WINDOW:(p + 1) * WINDOW]).astype(BF16)

    items = [(blk, g) for blk in range(len(blocks)) for g in range(N_KV_HEADS)]
    st_next = scores(*items[0])
    unfinished = None
    for n, item in enumerate(items):
        st_even, st_odd = st_next
        if n + 1 < len(items):
            st_next = scores(*items[n + 1])
        if unfinished is not None:
            finish(*unfinished)
        unfinished = item + weighted_values(*item, st_even, st_odd)
    finish(*unfinished)


def _attn_kernel(sink_ref, q_ref, kv_ref, ga_ref, *rest, tq, n_cast):
    cast_in, cast_out, o_ref = rest[:n_cast], rest[n_cast:2 * n_cast], rest[2 * n_cast]
    for src, dst in zip(cast_in, cast_out):
        dst[...] = src[0].astype(BF16)

    t = pl.program_id(1)
    later = [(blk * WINDOW, pl.multiple_of(t * tq + (blk - 1) * WINDOW, WINDOW), True)
             for blk in range(1, tq // WINDOW)]

    @pl.when(t == 0)
    def _():
        _attn_blocks(sink_ref, q_ref, kv_ref, ga_ref, o_ref, [(0, 0, False)] + later)

    @pl.when(t > 0)
    def _():
        first = (0, pl.multiple_of(t * tq - WINDOW, WINDOW), True)
        _attn_blocks(sink_ref, q_ref, kv_ref, ga_ref, o_ref, [first] + later)


def _attention(proj3, kv3, sinks, layer, weights, *, tq):
    b, t, _ = proj3.shape
    steps = b * (t // tq)
    cast_in, cast_out, cast_shapes = [], [], []
    for w in weights:
        _, rows, cols = w.shape
        assert rows % (16 * steps) == 0, (rows, steps)
        cast_in.append(pl.BlockSpec((1, rows // steps, cols), lambda i, j: (layer, i * (t // tq) + j, 0)))
        cast_out.append(pl.BlockSpec((rows // steps, cols), lambda i, j: (i * (t // tq) + j, 0)))
        cast_shapes.append(jax.ShapeDtypeStruct((rows, cols), BF16))
    return pl.pallas_call(
        functools.partial(_attn_kernel, tq=tq, n_cast=len(weights)),
        grid=(b, t // tq),
        in_specs=[
            pl.BlockSpec(memory_space=pltpu.SMEM),
            pl.BlockSpec((1, tq, D_MODEL), lambda i, j: (i, j, COL_AQ // D_MODEL)),
            pl.BlockSpec((1, t, 2 * KV_WIDTH), lambda i, j: (i, 0, 0)),
            pl.BlockSpec((1, tq, D_MODEL), lambda i, j: (i, j, COL_GATE_A // D_MODEL)),
        ] + cast_in,
        out_specs=cast_out + [pl.BlockSpec((1, tq, D_MODEL), lambda i, j: (i, j, 0))],
        out_shape=cast_shapes + [jax.ShapeDtypeStruct((b, t, D_MODEL), BF16)],
        compiler_params=pltpu.CompilerParams(
            dimension_semantics=("arbitrary", "arbitrary"), vmem_limit_bytes=VMEM_LIMIT),
        name="swa_attention",
    )(sinks, proj3, kv3, proj3, *weights)


def _log_sigmoid(x):
    return jnp.minimum(x, 0.0) - jnp.log(1.0 + jnp.exp(-jnp.abs(x)))


GLA_SUPER = 2 * GLA_CHUNK


def _split_bf16(x):
    hi = x.astype(BF16)
    return hi, (x - hi.astype(F32)).astype(BF16)


def _gla_kernel(q_ref, k_ref, v_ref, gr_ref, gb_ref, lr_ref, w2_ref, b_ref, nw_ref, o_ref, s_ref, *, tb):
    @pl.when(pl.program_id(2) == 0)
    def _():
        s_ref[...] = jnp.zeros_like(s_ref)

    lr = lr_ref[0].astype(BF16)
    logit = jnp.dot(lr, w2_ref[...], preferred_element_type=F32) + b_ref[...]
    log_a = _log_sigmoid(logit) * (1.0 / GLA_GATE_NORMALIZER)

    ri = lax.broadcasted_iota(jnp.int32, (GLA_SUPER, GLA_SUPER), 0)
    ci = lax.broadcasted_iota(jnp.int32, (GLA_SUPER, GLA_SUPER), 1)
    causal = ri >= ci
    cum = (causal & ((ri // GLA_CHUNK) == (ci // GLA_CHUNK))).astype(BF16)
    later_and_all = jnp.concatenate([(ri > ci).astype(BF16), jnp.ones((GLA_SUPER, GLA_SUPER), BF16)], axis=1)
    in_a = lax.broadcasted_iota(jnp.int32, (GLA_SUPER, 1), 0) < GLA_CHUNK

    def decays(c):
        rows = pl.ds(c * GLA_SUPER, GLA_SUPER)
        la = log_a[c * GLA_SUPER:(c + 1) * GLA_SUPER, :]
        la_hi, la_lo = _split_bf16(la)
        g = (jnp.dot(cum, la_hi, preferred_element_type=F32)
             + jnp.dot(cum, la_lo, preferred_element_type=F32))
        lat_hi, lat_lo = _split_bf16(la.T)
        tail = (jnp.dot(lat_hi, later_and_all, preferred_element_type=F32)
                + jnp.dot(lat_lo, later_and_all, preferred_element_type=F32))
        q = q_ref[0, rows, :].astype(F32)
        k = k_ref[0, rows, :].astype(F32)
        q_dec = q * (GLA_DK ** -0.5) * jnp.exp(g)
        carry = jnp.exp(g[GLA_CHUNK - 1:GLA_CHUNK, :])
        q_carry = jnp.where(in_a, q_dec, q_dec * carry).astype(BF16)
        k_inv = k * jnp.exp(-g)
        k_inv_a = jnp.where(in_a, k_inv, 0.0).astype(BF16)
        k_inv_b = jnp.where(in_a, 0.0, k_inv).astype(BF16)
        k_end_t = (k.T * jnp.exp(tail[:, :GLA_SUPER])).astype(BF16)
        decay = jnp.exp(tail[:, GLA_SUPER:])
        return q_carry, q_dec.astype(BF16), k_inv_a, k_inv_b, k_end_t, decay

    def intra(c, q_carry, q_dec, k_inv_a, k_inv_b, k_end_t, decay):
        v = v_ref[0, pl.ds(c * GLA_SUPER, GLA_SUPER), :]
        att = (lax.dot_general(q_carry, k_inv_a, _NT, preferred_element_type=F32)
               + lax.dot_general(q_dec, k_inv_b, _NT, preferred_element_type=F32))
        att = jnp.where(causal, att, 0.0).astype(BF16)
        o_intra = jnp.dot(att, v, preferred_element_type=F32)
        upd = jnp.dot(k_end_t, v, preferred_element_type=F32)
        return q_carry, o_intra, upd, decay

    def inter(c, q_carry, o_intra, upd, decay):
        rows = pl.ds(c * GLA_SUPER, GLA_SUPER)
        state = s_ref[...]
        o = o_intra + jnp.dot(q_carry, state.astype(BF16), preferred_element_type=F32)
        s_ref[...] = jnp.concatenate([decay] * (GLA_DV // GLA_SUPER), axis=1) * state + upd
        o = o * _rms_scale(o) * nw_ref[...]
        o = o * gr_ref[0, rows, :].astype(F32) * gb_ref[0, rows, :].astype(F32)
        o_ref[0, rows, :] = o.astype(BF16)

    n = tb // GLA_SUPER
    staged = {}
    for step in range(n + 2):
        if step < n:
            staged[step] = decays(step)
        if 0 <= step - 1 < n:
            staged[step - 1] = intra(step - 1, *staged[step - 1])
        if 0 <= step - 2 < n:
            inter(step - 2, *staged.pop(step - 2))


def _gla(proj3, lr3, w2, bias, norm_w, *, tb):
    b, t, _ = proj3.shape
    return pl.pallas_call(
        functools.partial(_gla_kernel, tb=tb),
        grid=(b, GLA_HEADS, t // tb),
        in_specs=[
            pl.BlockSpec((1, tb, GLA_DK), lambda i, h, j: (i, j, COL_GQ // GLA_DK + h)),
            pl.BlockSpec((1, tb, GLA_DK), lambda i, h, j: (i, j, COL_GK // GLA_DK + h)),
            pl.BlockSpec((1, tb, GLA_DV), lambda i, h, j: (i, j, COL_GV // GLA_DV + h)),
            pl.BlockSpec((1, tb, GLA_DV), lambda i, h, j: (i, j, COL_GR // GLA_DV + h)),
            pl.BlockSpec((1, tb, GLA_DV), lambda i, h, j: (i, j, COL_GATE_B // GLA_DV + h)),
            pl.BlockSpec((1, tb, LANES), lambda i, h, j: (i, j, 0)),
            pl.BlockSpec((LANES, GLA_DK), lambda i, h, j: (0, h)),
            pl.BlockSpec((1, GLA_DK), lambda i, h, j: (0, h)),
            pl.BlockSpec((1, GLA_DV), lambda i, h, j: (0, 0)),
        ],
        out_specs=pl.BlockSpec((1, tb, GLA_DV), lambda i, h, j: (i, j, h)),
        out_shape=jax.ShapeDtypeStruct((b, t, D_MODEL), BF16),
        scratch_shapes=[pltpu.VMEM((GLA_DK, GLA_DV), F32)],
        compiler_params=pltpu.CompilerParams(
            dimension_semantics=("arbitrary", "arbitrary", "arbitrary"), vmem_limit_bytes=VMEM_LIMIT),
        name="gla",
    )(proj3, proj3, proj3, proj3, proj3, lr3, w2, bias, norm_w)


def _outproj_kernel(a_ref, g_ref, x_ref, w_ref, h_ref):
    merged = a_ref[...] + g_ref[...]
    h_ref[...] = x_ref[...] + jnp.dot(merged, w_ref[...], preferred_element_type=F32)


def _outproj(a2, g2, x2, w_out, *, tm):
    n = x2.shape[0]
    return pl.pallas_call(
        _outproj_kernel,
        grid=(n // tm,),
        in_specs=[
            pl.BlockSpec((tm, D_MODEL), lambda i: (i, 0)),
            pl.BlockSpec((tm, D_MODEL), lambda i: (i, 0)),
            pl.BlockSpec((tm, D_MODEL), lambda i: (i, 0)),
            pl.BlockSpec((D_MODEL, D_MODEL), lambda i: (0, 0)),
        ],
        out_specs=pl.BlockSpec((tm, D_MODEL), lambda i: (i, 0)),
        out_shape=jax.ShapeDtypeStruct((n, D_MODEL), F32),
        compiler_params=pltpu.CompilerParams(
            dimension_semantics=("arbitrary",), vmem_limit_bytes=VMEM_LIMIT),
        name="outproj",
    )(a2, g2, x2, w_out)


def _ffn_kernel(h_ref, nw_ref, wg_ref, wu_ref, wd_ref, fw_ref, o_ref, v_ref, *, final_norm):
    j = pl.program_id(1)

    @pl.when(j == 0)
    def _():
        h = h_ref[...]
        v_ref[...] = (h * _rms_scale(h) * nw_ref[...]).astype(BF16)
        o_ref[...] = h

    v = v_ref[...]
    gate = jnp.dot(v, wg_ref[...], preferred_element_type=F32)
    up = jnp.dot(v, wu_ref[...], preferred_element_type=F32)
    act = (gate * jax.nn.sigmoid(gate) * up).astype(BF16)
    o_ref[...] += jnp.dot(act, wd_ref[...], preferred_element_type=F32)

    if final_norm:
        @pl.when(j == pl.num_programs(1) - 1)
        def _():
            y = o_ref[...]
            o_ref[...] = y * _rms_scale(y) * fw_ref[...]


def _ffn(h2, norm_w, w_gate, w_up, w_down, final_w, *, tm, th, final_norm):
    n = h2.shape[0]
    return pl.pallas_call(
        functools.partial(_ffn_kernel, final_norm=final_norm),
        grid=(n // tm, FFN_HIDDEN // th),
        in_specs=[
            pl.BlockSpec((tm, D_MODEL), lambda i, j: (i, 0)),
            pl.BlockSpec((1, D_MODEL), lambda i, j: (0, 0)),
            pl.BlockSpec((D_MODEL, th), lambda i, j: (0, j)),
            pl.BlockSpec((D_MODEL, th), lambda i, j: (0, j)),
            pl.BlockSpec((th, D_MODEL), lambda i, j: (j, 0)),
            pl.BlockSpec((1, D_MODEL), lambda i, j: (0, 0)),
        ],
        out_specs=pl.BlockSpec((tm, D_MODEL), lambda i, j: (i, 0)),
        out_shape=jax.ShapeDtypeStruct((n, D_MODEL), F32),
        scratch_shapes=[pltpu.VMEM((tm, D_MODEL), BF16)],
        compiler_params=pltpu.CompilerParams(
            dimension_semantics=("arbitrary", "arbitrary"), vmem_limit_bytes=VMEM_LIMIT),
        name="ffn",
    )(h2, norm_w, w_gate, w_up, w_down, final_w)


_IN_WIDTHS = (D_MODEL, KV_WIDTH, KV_WIDTH, GLA_HEADS * GLA_DK, GLA_HEADS * GLA_DK, D_MODEL,
              GLA_GATE_RANK, D_MODEL, D_MODEL, D_MODEL)
(IN_AQ, IN_AK, IN_AV, IN_GQ, IN_GK, IN_GV, IN_LR, IN_GR, IN_GATE_A, IN_GATE_B, D_IN) = (
    sum(_IN_WIDTHS[:i]) for i in range(len(_IN_WIDTHS) + 1))


REGROUP_ROWS = 2 * KV_WIDTH


def _regroup_source_row(i):
    row = i * REGROUP_ROWS
    src = row - COL_AQ + IN_AQ
    src = jnp.where(row >= COL_GV, row - COL_GV + IN_GV, src)
    src = jnp.where(row >= COL_GR, row - COL_GR + IN_GR, src)
    src = jnp.where(row >= COL_GQ, row - COL_GQ + IN_GQ, src)
    src = jnp.where(row >= COL_KV, row - COL_KV + IN_AK, src)
    return pl.multiple_of(src, GLA_GATE_RANK)


def _regroup_kernel(wt_ref, o_ref):
    o_ref[...] = wt_ref[0].astype(BF16)


def _regroup_w_in(w_in_t, layer):
    rows = COL_KV + 2 * KV_WIDTH
    return pl.pallas_call(
        _regroup_kernel,
        grid=(rows // REGROUP_ROWS,),
        in_specs=[pl.BlockSpec((pl.Element(1), pl.Element(REGROUP_ROWS), pl.Element(D_MODEL)),
                               lambda i: (layer, _regroup_source_row(i), 0))],
        out_specs=pl.BlockSpec((REGROUP_ROWS, D_MODEL), lambda i: (i, 0)),
        out_shape=jax.ShapeDtypeStruct((rows, D_MODEL), BF16),
        compiler_params=pltpu.CompilerParams(
            dimension_semantics=("arbitrary",), vmem_limit_bytes=VMEM_LIMIT),
        name="regroup_w_in",
    )(w_in_t)


def _tile(n, pref):
    return pref if n % pref == 0 else n


def kernel(x, norm1_w, w_in, gla_gate_w2, gla_gate_b, attn_sinks, gla_norm_w, w_out, norm2_w,
           w_ffn_gate, w_ffn_up, w_ffn_down, final_norm_w):
    b, t, d = x.shape
    n = b * t
    depth = w_in.shape[0]
    h2 = x.reshape(n, d)
    for l in range(depth):
        w_in_t = jnp.swapaxes(w_in, 1, 2)
        proj, kv, lr = _inproj(h2, norm1_w[l].reshape(1, d), _regroup_w_in(w_in_t, l),
                               w_in_t[l, IN_LR:IN_LR + GLA_GATE_RANK], tm=_tile(n, 1024))
        proj3 = proj.reshape(b, t, PROJ_WIDTH)

        wo, wg, wu, wd, a = _attention(proj3, kv.reshape(b, t, 2 * KV_WIDTH), attn_sinks[l], l,
                                       (w_out, w_ffn_gate, w_ffn_up, w_ffn_down), tq=_tile(t, 512))

        w2 = jnp.pad(gla_gate_w2[l], ((0, LANES - GLA_GATE_RANK), (0, 0))).astype(BF16)
        g = _gla(proj3, lr.reshape(b, t, LANES), w2, gla_gate_b[l].reshape(1, -1),
                 gla_norm_w[l].reshape(1, GLA_DV), tb=_tile(t, 2048))

        h2 = _outproj(a.reshape(n, d), g.reshape(n, d), h2, wo, tm=_tile(n, 512))
        h2 = _ffn(h2, norm2_w[l].reshape(1, d), wg, wu, wd, final_norm_w.reshape(1, d),
                  tm=_tile(n, 1024), th=512, final_norm=l == depth - 1)
    return h2.reshape(b, t, d)
```

```python
import functools

import jax
import jax.numpy as jnp
from jax import lax
from jax.experimental import pallas as pl
from jax.experimental.pallas import tpu as pltpu

F32 = jnp.float32
BF16 = jnp.bfloat16

D_MODEL = 2048
HEAD_DIM = 64
N_Q_HEADS = D_MODEL // HEAD_DIM
N_KV_HEADS = 4
GQA_GROUP = N_Q_HEADS // N_KV_HEADS
KV_WIDTH = N_KV_HEADS * HEAD_DIM
WINDOW = 128

GLA_HEADS = 4
GLA_DK = (D_MODEL // 2) // GLA_HEADS
GLA_DV = D_MODEL // GLA_HEADS
GLA_GATE_RANK = 16
GLA_GATE_NORMALIZER = 16.0
GLA_CHUNK = 64

FFN_HIDDEN = ((8 * D_MODEL // 3 + 255) // 256) * 256
RMS_EPS = 1e-6
MASK_VALUE = -1e30

LANES = 128

COL_AQ = 0
COL_GV = COL_AQ + D_MODEL
COL_GR = COL_GV + D_MODEL
COL_GATE_A = COL_GR + D_MODEL
COL_GATE_B = COL_GATE_A + D_MODEL
COL_GQ = COL_GATE_B + D_MODEL
COL_GK = COL_GQ + GLA_HEADS * GLA_DK
PROJ_WIDTH = COL_GK + GLA_HEADS * GLA_DK
PROJ_TILE = 2048
COL_KV = PROJ_WIDTH

VMEM_LIMIT = 56 * 1024 * 1024

_NT = (((1,), (1,)), ((), ()))
_TN = (((0,), (0,)), ((), ()))


def _rms_scale(x):
    return lax.rsqrt(jnp.mean(x * x, axis=-1, keepdims=True) + RMS_EPS)


def _sigmoid(x):
    return 0.5 * jnp.tanh(0.5 * x) + 0.5


def _inproj_kernel(x_ref, nw_ref, w_ref, wkv_ref, wlr_ref, proj_ref, kv_ref, lr_ref, u_ref):
    j = pl.program_id(1)

    @pl.when(j == 0)
    def _():
        x = x_ref[...]
        u = (x * _rms_scale(x) * nw_ref[...]).astype(BF16)
        u_ref[...] = u
        kv_ref[...] = lax.dot_general(u, wkv_ref[...], _NT, preferred_element_type=F32).astype(BF16)
        w_lr = wlr_ref[...].astype(BF16)
        w_lr = jnp.concatenate([w_lr, jnp.zeros((LANES - GLA_GATE_RANK, D_MODEL), BF16)], axis=0)
        lr_ref[...] = lax.dot_general(u, w_lr, _NT, preferred_element_type=F32)

    def tile():
        return lax.dot_general(u_ref[...], w_ref[...], _NT, preferred_element_type=F32)

    is_silu = (j >= COL_GR // PROJ_TILE) & (j < COL_GATE_A // PROJ_TILE)
    is_sigmoid = (j >= COL_GATE_A // PROJ_TILE) & (j < COL_GQ // PROJ_TILE)

    @pl.when(is_silu)
    def _():
        y = tile()
        proj_ref[...] = (y * _sigmoid(y)).astype(BF16)

    @pl.when(is_sigmoid)
    def _():
        proj_ref[...] = _sigmoid(tile()).astype(BF16)

    @pl.when(jnp.logical_not(is_silu | is_sigmoid))
    def _():
        proj_ref[...] = tile().astype(BF16)


def _inproj(x2, norm_w, w_t, w_lr_t, *, tm):
    n = x2.shape[0]
    return pl.pallas_call(
        _inproj_kernel,
        grid=(n // tm, PROJ_WIDTH // PROJ_TILE),
        in_specs=[
            pl.BlockSpec((tm, D_MODEL), lambda i, j: (i, 0)),
            pl.BlockSpec((1, D_MODEL), lambda i, j: (0, 0)),
            pl.BlockSpec((PROJ_TILE, D_MODEL), lambda i, j: (j, 0)),
            pl.BlockSpec((2 * KV_WIDTH, D_MODEL), lambda i, j: (COL_KV // (2 * KV_WIDTH), 0)),
            pl.BlockSpec((GLA_GATE_RANK, D_MODEL), lambda i, j: (0, 0)),
        ],
        out_specs=[
            pl.BlockSpec((tm, PROJ_TILE), lambda i, j: (i, j)),
            pl.BlockSpec((tm, 2 * KV_WIDTH), lambda i, j: (i, 0)),
            pl.BlockSpec((tm, LANES), lambda i, j: (i, 0)),
        ],
        out_shape=[
            jax.ShapeDtypeStruct((n, PROJ_WIDTH), BF16),
            jax.ShapeDtypeStruct((n, 2 * KV_WIDTH), BF16),
            jax.ShapeDtypeStruct((n, LANES), F32),
        ],
        scratch_shapes=[pltpu.VMEM((tm, D_MODEL), BF16)],
        compiler_params=pltpu.CompilerParams(
            dimension_semantics=("arbitrary", "arbitrary"), vmem_limit_bytes=VMEM_LIMIT),
        name="inproj",
    )(x2, norm_w, w_t, w_t, w_lr_t)


PAIR = 2 * HEAD_DIM
PAIRS_PER_GROUP = GQA_GROUP // 2
STACK = PAIRS_PER_GROUP * WINDOW


def _attn_blocks(sink_ref, q_ref, kv_ref, ga_ref, o_ref, blocks):
    lo = lax.broadcasted_iota(jnp.int32, (1, PAIR), 1) < HEAD_DIM
    top = lax.broadcasted_iota(jnp.int32, (PAIR, 1), 0) < HEAD_DIM
    ki = lax.broadcasted_iota(jnp.int32, (WINDOW, STACK), 0)
    qi = lax.broadcasted_iota(jnp.int32, (WINDOW, STACK), 1) & (WINDOW - 1)
    upper = ki > qi

    def rows(blk):
        return pl.ds(blocks[blk][0], WINDOW)

    def krows(blk):
        _, k_row0, has_prev = blocks[blk]
        return pl.ds(k_row0, 2 * WINDOW if has_prev else WINDOW)

    def halves(pair_tile, g):
        swapped = pltpu.roll(pair_tile, HEAD_DIM, axis=1)
        own, other = (pair_tile, swapped) if g % 2 == 0 else (swapped, pair_tile)
        zero = jnp.zeros_like(pair_tile)
        return jnp.where(lo, own, zero), jnp.where(lo, zero, other)

    def probs(st, sink_row, has_prev):
        if has_prev:
            folded = jnp.where(upper, st[:WINDOW], st[WINDOW:])
        else:
            folded = jnp.where(upper, MASK_VALUE, st)
        m = jnp.maximum(jnp.max(folded, axis=0, keepdims=True), sink_row)
        p = jnp.exp(folded - m)
        inv = 1.0 / (jnp.sum(p, axis=0, keepdims=True) + jnp.exp(sink_row - m))
        if has_prev:
            p = jnp.concatenate([jnp.where(upper, p, 0.0), jnp.where(upper, 0.0, p)], axis=0)
        return p.astype(BF16), inv

    def scores(blk, g):
        k_lanes = slice((g // 2) * PAIR, (g // 2 + 1) * PAIR)
        k_lo, k_hi = halves(kv_ref[0, krows(blk), k_lanes], g)
        first_pair = g * PAIRS_PER_GROUP
        q = jnp.concatenate(
            [q_ref[0, rows(blk), (first_pair + p) * PAIR:(first_pair + p + 1) * PAIR]
             for p in range(PAIRS_PER_GROUP)], axis=0) * (HEAD_DIM ** -0.5)
        return (lax.dot_general(k_lo, q, _NT, preferred_element_type=F32),
                lax.dot_general(k_hi, q, _NT, preferred_element_type=F32))

    def weighted_values(blk, g, st_even, st_odd):
        v_lanes = slice(KV_WIDTH + (g // 2) * PAIR, KV_WIDTH + (g // 2 + 1) * PAIR)
        v_lo, v_hi = halves(kv_ref[0, krows(blk), v_lanes], g)
        first_pair = g * PAIRS_PER_GROUP
        sink_even = jnp.concatenate(
            [jnp.full((1, WINDOW), sink_ref[2 * (first_pair + p)], F32) for p in range(PAIRS_PER_GROUP)], axis=1)
        sink_odd = jnp.concatenate(
            [jnp.full((1, WINDOW), sink_ref[2 * (first_pair + p) + 1], F32) for p in range(PAIRS_PER_GROUP)],
            axis=1)
        p_even, inv_even = probs(st_even, sink_even, blocks[blk][2])
        p_odd, inv_odd = probs(st_odd, sink_odd, blocks[blk][2])
        ot = (lax.dot_general(v_lo, p_even, _TN, preferred_element_type=F32)
              + lax.dot_general(v_hi, p_odd, _TN, preferred_element_type=F32))
        return ot, jnp.where(top, inv_even, inv_odd)

    def finish(blk, g, ot, inv):
        o = (ot * inv).T
        first_pair = g * PAIRS_PER_GROUP
        for p in range(PAIRS_PER_GROUP):
            cols = slice((first_pair + p) * PAIR, (first_pair + p + 1) * PAIR)
            gate = ga_ref[0, rows(blk), cols].astype(F32)
            o_ref[0, rows(blk), cols] = (gate * o[p * WINDOW:(p + 1) * WINDOW]).astype(BF16)

    items = [(blk, g) for blk in range(len(blocks)) for g in range(N_KV_HEADS)]
    st_next = scores(*items[0])
    unfinished = None
    for n, item in enumerate(items):
        st_even, st_odd = st_next
        if n + 1 < len(items):
            st_next = scores(*items[n + 1])
        if unfinished is not None:
            finish(*unfinished)
        unfinished = item + weighted_values(*item, st_even, st_odd)
    finish(*unfinished)


def _attn_kernel(sink_ref, q_ref, kv_ref, ga_ref, *rest, tq, n_cast):
    cast_in, cast_out, o_ref = rest[:n_cast], rest[n_cast:2 * n_cast], rest[2 * n_cast]
    for src, dst in zip(cast_in, cast_out):
        dst[...] = src[0].astype(BF16)

    t = pl.program_id(1)
    later = [(blk * WINDOW, pl.multiple_of(t * tq + (blk - 1) * WINDOW, WINDOW), True)
             for blk in range(1, tq // WINDOW)]

    @pl.when(t == 0)
    def _():
        _attn_blocks(sink_ref, q_ref, kv_ref, ga_ref, o_ref, [(0, 0, False)] + later)

    @pl.when(t > 0)
    def _():
        first = (0, pl.multiple_of(t * tq - WINDOW, WINDOW), True)
        _attn_blocks(sink_ref, q_ref, kv_ref, ga_ref, o_ref, [first] + later)


def _attention(proj3, kv3, sinks, layer, weights, *, tq):
    b, t, _ = proj3.shape
    steps = b * (t // tq)
    cast_in, cast_out, cast_shapes = [], [], []
    for w in weights:
        _, rows, cols = w.shape
        assert rows % (16 * steps) == 0, (rows, steps)
        cast_in.append(pl.BlockSpec((1, rows // steps, cols), lambda i, j: (layer, i * (t // tq) + j, 0)))
        cast_out.append(pl.BlockSpec((rows // steps, cols), lambda i, j: (i * (t // tq) + j, 0)))
        cast_shapes.append(jax.ShapeDtypeStruct((rows, cols), BF16))
    return pl.pallas_call(
        functools.partial(_attn_kernel, tq=tq, n_cast=len(weights)),
        grid=(b, t // tq),
        in_specs=[
            pl.BlockSpec(memory_space=pltpu.SMEM),
            pl.BlockSpec((1, tq, D_MODEL), lambda i, j: (i, j, COL_AQ // D_MODEL)),
            pl.BlockSpec((1, t, 2 * KV_WIDTH), lambda i, j: (i, 0, 0)),
            pl.BlockSpec((1, tq, D_MODEL), lambda i, j: (i, j, COL_GATE_A // D_MODEL)),
        ] + cast_in,
        out_specs=cast_out + [pl.BlockSpec((1, tq, D_MODEL), lambda i, j: (i, j, 0))],
        out_shape=cast_shapes + [jax.ShapeDtypeStruct((b, t, D_MODEL), BF16)],
        compiler_params=pltpu.CompilerParams(
            dimension_semantics=("arbitrary", "arbitrary"), vmem_limit_bytes=VMEM_LIMIT),
        name="swa_attention",
    )(sinks, proj3, kv3, proj3, *weights)


def _log_sigmoid(x):
    return jnp.minimum(x, 0.0) - jnp.log(1.0 + jnp.exp(-jnp.abs(x)))


GLA_SUPER = 2 * GLA_CHUNK


def _split_bf16(x):
    hi = x.astype(BF16)
    return hi, (x - hi.astype(F32)).astype(BF16)


def _gla_kernel(q_ref, k_ref, v_ref, gr_ref, gb_ref, lr_ref, w2_ref, b_ref, nw_ref, o_ref, s_ref, *, tb):
    @pl.when(pl.program_id(2) == 0)
    def _():
        s_ref[...] = jnp.zeros_like(s_ref)

    lr = lr_ref[0].astype(BF16)
    logit = jnp.dot(lr, w2_ref[...], preferred_element_type=F32) + b_ref[...]
    log_a = _log_sigmoid(logit) * (1.0 / GLA_GATE_NORMALIZER)

    ri = lax.broadcasted_iota(jnp.int32, (GLA_SUPER, GLA_SUPER), 0)
    ci = lax.broadcasted_iota(jnp.int32, (GLA_SUPER, GLA_SUPER), 1)
    causal = ri >= ci
    cum = (causal & ((ri // GLA_CHUNK) == (ci // GLA_CHUNK))).astype(BF16)
    later_and_all = jnp.concatenate([(ri > ci).astype(BF16), jnp.ones((GLA_SUPER, GLA_SUPER), BF16)], axis=1)
    in_a = lax.broadcasted_iota(jnp.int32, (GLA_SUPER, 1), 0) < GLA_CHUNK

    def decays(c):
        rows = pl.ds(c * GLA_SUPER, GLA_SUPER)
        la = log_a[c * GLA_SUPER:(c + 1) * GLA_SUPER, :]
        la_hi, la_lo = _split_bf16(la)
        g = (jnp.dot(cum, la_hi, preferred_element_type=F32)
             + jnp.dot(cum, la_lo, preferred_element_type=F32))
        lat_hi, lat_lo = _split_bf16(la.T)
        tail = (jnp.dot(lat_hi, later_and_all, preferred_element_type=F32)
                + jnp.dot(lat_lo, later_and_all, preferred_element_type=F32))
        q = q_ref[0, rows, :].astype(F32)
        k = k_ref[0, rows, :].astype(F32)
        q_dec = q * (GLA_DK ** -0.5) * jnp.exp(g)
        carry = jnp.exp(g[GLA_CHUNK - 1:GLA_CHUNK, :])
        q_carry = jnp.where(in_a, q_dec, q_dec * carry).astype(BF16)
        k_inv = k * jnp.exp(-g)
        k_inv_a = jnp.where(in_a, k_inv, 0.0).astype(BF16)
        k_inv_b = jnp.where(in_a, 0.0, k_inv).astype(BF16)
        k_end_t = (k.T * jnp.exp(tail[:, :GLA_SUPER])).astype(BF16)
        decay = jnp.exp(tail[:, GLA_SUPER:])
        return q_carry, q_dec.astype(BF16), k_inv_a, k_inv_b, k_end_t, decay

    def intra(c, q_carry, q_dec, k_inv_a, k_inv_b, k_end_t, decay):
        v = v_ref[0, pl.ds(c * GLA_SUPER, GLA_SUPER), :]
        att = (lax.dot_general(q_carry, k_inv_a, _NT, preferred_element_type=F32)
               + lax.dot_general(q_dec, k_inv_b, _NT, preferred_element_type=F32))
        att = jnp.where(causal, att, 0.0).astype(BF16)
        o_intra = jnp.dot(att, v, preferred_element_type=F32)
        upd = jnp.dot(k_end_t, v, preferred_element_type=F32)
        return q_carry, o_intra, upd, decay

    def inter(c, q_carry, o_intra, upd, decay):
        rows = pl.ds(c * GLA_SUPER, GLA_SUPER)
        state = s_ref[...]
        o = o_intra + jnp.dot(q_carry, state.astype(BF16), preferred_element_type=F32)
        s_ref[...] = jnp.concatenate([decay] * (GLA_DV // GLA_SUPER), axis=1) * state + upd
        o = o * _rms_scale(o) * nw_ref[...]
        o = o * gr_ref[0, rows, :].astype(F32) * gb_ref[0, rows, :].astype(F32)
        o_ref[0, rows, :] = o.astype(BF16)

    n = tb // GLA_SUPER
    staged = {}
    for step in range(n + 2):
        if step < n:
            staged[step] = decays(step)
        if 0 <= step - 1 < n:
            staged[step - 1] = intra(step - 1, *staged[step - 1])
        if 0 <= step - 2 < n:
            inter(step - 2, *staged.pop(step - 2))


def _gla(proj3, lr3, w2, bias, norm_w, *, tb):
    b, t, _ = proj3.shape
    return pl.pallas_call(
        functools.partial(_gla_kernel, tb=tb),
        grid=(b, GLA_HEADS, t // tb),
        in_specs=[
            pl.BlockSpec((1, tb, GLA_DK), lambda i, h, j: (i, j, COL_GQ // GLA_DK + h)),
            pl.BlockSpec((1, tb, GLA_DK), lambda i, h, j: (i, j, COL_GK // GLA_DK + h)),
            pl.BlockSpec((1, tb, GLA_DV), lambda i, h, j: (i, j, COL_GV // GLA_DV + h)),
            pl.BlockSpec((1, tb, GLA_DV), lambda i, h, j: (i, j, COL_GR // GLA_DV + h)),
            pl.BlockSpec((1, tb, GLA_DV), lambda i, h, j: (i, j, COL_GATE_B // GLA_DV + h)),
            pl.BlockSpec((1, tb, LANES), lambda i, h, j: (i, j, 0)),
            pl.BlockSpec((LANES, GLA_DK), lambda i, h, j: (0, h)),
            pl.BlockSpec((1, GLA_DK), lambda i, h, j: (0, h)),
            pl.BlockSpec((1, GLA_DV), lambda i, h, j: (0, 0)),
        ],
        out_specs=pl.BlockSpec((1, tb, GLA_DV), lambda i, h, j: (i, j, h)),
        out_shape=jax.ShapeDtypeStruct((b, t, D_MODEL), BF16),
        scratch_shapes=[pltpu.VMEM((GLA_DK, GLA_DV), F32)],
        compiler_params=pltpu.CompilerParams(
            dimension_semantics=("arbitrary", "arbitrary", "arbitrary"), vmem_limit_bytes=VMEM_LIMIT),
        name="gla",
    )(proj3, proj3, proj3, proj3, proj3, lr3, w2, bias, norm_w)


def _outproj_kernel(a_ref, g_ref, x_ref, w_ref, h_ref):
    merged = a_ref[...] + g_ref[...]
    h_ref[...] = x_ref[...] + jnp.dot(merged, w_ref[...], preferred_element_type=F32)


def _outproj(a2, g2, x2, w_out, *, tm):
    n = x2.shape[0]
    return pl.pallas_call(
        _outproj_kernel,
        grid=(n // tm,),
        in_specs=[
            pl.BlockSpec((tm, D_MODEL), lambda i: (i, 0)),
            pl.BlockSpec((tm, D_MODEL), lambda i: (i, 0)),
            pl.BlockSpec((tm, D_MODEL), lambda i: (i, 0)),
            pl.BlockSpec((D_MODEL, D_MODEL), lambda i: (0, 0)),
        ],
        out_specs=pl.BlockSpec((tm, D_MODEL), lambda i: (i, 0)),
        out_shape=jax.ShapeDtypeStruct((n, D_MODEL), F32),
        compiler_params=pltpu.CompilerParams(
            dimension_semantics=("arbitrary",), vmem_limit_bytes=VMEM_LIMIT),
        name="outproj",
    )(a2, g2, x2, w_out)


def _ffn_kernel(h_ref, nw_ref, wg_ref, wu_ref, wd_ref, fw_ref, o_ref, v_ref, *, final_norm):
    j = pl.program_id(1)

    @pl.when(j == 0)
    def _():
        h = h_ref[...]
        v_ref[...] = (h * _rms_scale(h) * nw_ref[...]).astype(BF16)
        o_ref[...] = h

    v = v_ref[...]
    th = wg_ref.shape[1]
    part = None
    halves = [(c * th // 2, (c + 1) * th // 2) for c in range(2)]
    acts = []
    for lo, hi in halves:
        gate = jnp.dot(v, wg_ref[:, lo:hi], preferred_element_type=F32)
        up = jnp.dot(v, wu_ref[:, lo:hi], preferred_element_type=F32)
        acts.append((gate, up))
    for (lo, hi), (gate, up) in zip(halves, acts):
        act = (gate * _sigmoid(gate) * up).astype(BF16)
        contrib = jnp.dot(act, wd_ref[lo:hi, :], preferred_element_type=F32)
        part = contrib if part is None else part + contrib
    o_ref[...] += part

    if final_norm:
        @pl.when(j == pl.num_programs(1) - 1)
        def _():
            y = o_ref[...]
            o_ref[...] = y * _rms_scale(y) * fw_ref[...]


def _ffn(h2, norm_w, w_gate, w_up, w_down, final_w, *, tm, th, final_norm):
    n = h2.shape[0]
    return pl.pallas_call(
        functools.partial(_ffn_kernel, final_norm=final_norm),
        grid=(n // tm, FFN_HIDDEN // th),
        in_specs=[
            pl.BlockSpec((tm, D_MODEL), lambda i, j: (i, 0)),
            pl.BlockSpec((1, D_MODEL), lambda i, j: (0, 0)),
            pl.BlockSpec((D_MODEL, th), lambda i, j: (0, j)),
            pl.BlockSpec((D_MODEL, th), lambda i, j: (0, j)),
            pl.BlockSpec((th, D_MODEL), lambda i, j: (j, 0)),
            pl.BlockSpec((1, D_MODEL), lambda i, j: (0, 0)),
        ],
        out_specs=pl.BlockSpec((tm, D_MODEL), lambda i, j: (i, 0)),
        out_shape=jax.ShapeDtypeStruct((n, D_MODEL), F32),
        scratch_shapes=[pltpu.VMEM((tm, D_MODEL), BF16)],
        compiler_params=pltpu.CompilerParams(
            dimension_semantics=("arbitrary", "arbitrary"), vmem_limit_bytes=VMEM_LIMIT),
        name="ffn",
    )(h2, norm_w, w_gate, w_up, w_down, final_w)


_IN_WIDTHS = (D_MODEL, KV_WIDTH, KV_WIDTH, GLA_HEADS * GLA_DK, GLA_HEADS * GLA_DK, D_MODEL,
              GLA_GATE_RANK, D_MODEL, D_MODEL, D_MODEL)
(IN_AQ, IN_AK, IN_AV, IN_GQ, IN_GK, IN_GV, IN_LR, IN_GR, IN_GATE_A, IN_GATE_B, D_IN) = (
    sum(_IN_WIDTHS[:i]) for i in range(len(_IN_WIDTHS) + 1))


REGROUP_ROWS = 2 * KV_WIDTH


def _regroup_source_row(i):
    row = i * REGROUP_ROWS
    src = row - COL_AQ + IN_AQ
    src = jnp.where(row >= COL_GV, row - COL_GV + IN_GV, src)
    src = jnp.where(row >= COL_GR, row - COL_GR + IN_GR, src)
    src = jnp.where(row >= COL_GQ, row - COL_GQ + IN_GQ, src)
    src = jnp.where(row >= COL_KV, row - COL_KV + IN_AK, src)
    return pl.multiple_of(src, GLA_GATE_RANK)


def _regroup_kernel(wt_ref, o_ref):
    o_ref[...] = wt_ref[0].astype(BF16)


def _regroup_w_in(w_in_t, layer):
    rows = COL_KV + 2 * KV_WIDTH
    return pl.pallas_call(
        _regroup_kernel,
        grid=(rows // REGROUP_ROWS,),
        in_specs=[pl.BlockSpec((pl.Element(1), pl.Element(REGROUP_ROWS), pl.Element(D_MODEL)),
                               lambda i: (layer, _regroup_source_row(i), 0))],
        out_specs=pl.BlockSpec((REGROUP_ROWS, D_MODEL), lambda i: (i, 0)),
        out_shape=jax.ShapeDtypeStruct((rows, D_MODEL), BF16),
        compiler_params=pltpu.CompilerParams(
            dimension_semantics=("arbitrary",), vmem_limit_bytes=VMEM_LIMIT),
        name="regroup_w_in",
    )(w_in_t)


def _tile(n, pref):
    return pref if n % pref == 0 else n


def kernel(x, norm1_w, w_in, gla_gate_w2, gla_gate_b, attn_sinks, gla_norm_w, w_out, norm2_w,
           w_ffn_gate, w_ffn_up, w_ffn_down, final_norm_w):
    b, t, d = x.shape
    n = b * t
    depth = w_in.shape[0]
    h2 = x.reshape(n, d)
    for l in range(depth):
        w_in_t = jnp.swapaxes(w_in, 1, 2)
        proj, kv, lr = _inproj(h2, norm1_w[l].reshape(1, d), _regroup_w_in(w_in_t, l),
                               w_in_t[l, IN_LR:IN_LR + GLA_GATE_RANK], tm=_tile(n, 1024))
        proj3 = proj.reshape(b, t, PROJ_WIDTH)

        wo, wg, wu, wd, a = _attention(proj3, kv.reshape(b, t, 2 * KV_WIDTH), attn_sinks[l], l,
                                       (w_out, w_ffn_gate, w_ffn_up, w_ffn_down), tq=_tile(t, 512))

        w2 = jnp.pad(gla_gate_w2[l], ((0, LANES - GLA_GATE_RANK), (0, 0))).astype(BF16)
        g = _gla(proj3, lr.reshape(b, t, LANES), w2, gla_gate_b[l].reshape(1, -1),
                 gla_norm_w[l].reshape(1, GLA_DV), tb=_tile(t, 2048))

        h2 = _outproj(a.reshape(n, d), g.reshape(n, d), h2, wo, tm=_tile(n, 512))
        h2 = _ffn(h2, norm2_w[l].reshape(1, d), wg, wu, wd, final_norm_w.reshape(1, d),
                  tm=_tile(n, 1024), th=512, final_norm=l == depth - 1)
    return h2.reshape(b, t, d)
```

```python
import functools

import jax
import jax.numpy as jnp
from jax import lax
from jax.experimental import pallas as pl
from jax.experimental.pallas import tpu as pltpu

F32 = jnp.float32
BF16 = jnp.bfloat16

D_MODEL = 2048
HEAD_DIM = 64
N_Q_HEADS = D_MODEL // HEAD_DIM
N_KV_HEADS = 4
GQA_GROUP = N_Q_HEADS // N_KV_HEADS
KV_WIDTH = N_KV_HEADS * HEAD_DIM
WINDOW = 128

GLA_HEADS = 4
GLA_DK = (D_MODEL // 2) // GLA_HEADS
GLA_DV = D_MODEL // GLA_HEADS
GLA_GATE_RANK = 16
GLA_GATE_NORMALIZER = 16.0
GLA_CHUNK = 64

FFN_HIDDEN = ((8 * D_MODEL // 3 + 255) // 256) * 256
RMS_EPS = 1e-6
MASK_VALUE = -1e30

LANES = 128

COL_AQ = 0
COL_GV = COL_AQ + D_MODEL
COL_GR = COL_GV + D_MODEL
COL_GATE_A = COL_GR + D_MODEL
COL_GATE_B = COL_GATE_A + D_MODEL
COL_GQ = COL_GATE_B + D_MODEL
COL_GK = COL_GQ + GLA_HEADS * GLA_DK
PROJ_WIDTH = COL_GK + GLA_HEADS * GLA_DK
PROJ_TILE = 2048
COL_KV = PROJ_WIDTH

VMEM_LIMIT = 56 * 1024 * 1024

_NT = (((1,), (1,)), ((), ()))
_TN = (((0,), (0,)), ((), ()))


def _rms_scale(x):
    return lax.rsqrt(jnp.mean(x * x, axis=-1, keepdims=True) + RMS_EPS)


def _sigmoid(x):
    return 0.5 * jnp.tanh(0.5 * x) + 0.5


def _inproj_kernel(x_ref, nw_ref, w_ref, wkv_ref, wlr_ref, proj_ref, kv_ref, lr_ref, u_ref):
    j = pl.program_id(1)

    @pl.when(j == 0)
    def _():
        x = x_ref[...]
        u = (x * _rms_scale(x) * nw_ref[...]).astype(BF16)
        u_ref[...] = u
        kv_ref[...] = lax.dot_general(u, wkv_ref[...], _NT, preferred_element_type=F32).astype(BF16)
        w_lr = wlr_ref[...].astype(BF16)
        w_lr = jnp.concatenate([w_lr, jnp.zeros((LANES - GLA_GATE_RANK, D_MODEL), BF16)], axis=0)
        lr_ref[...] = lax.dot_general(u, w_lr, _NT, preferred_element_type=F32).astype(BF16)

    def tile():
        return lax.dot_general(u_ref[...], w_ref[...], _NT, preferred_element_type=F32)

    is_silu = (j >= COL_GR // PROJ_TILE) & (j < COL_GATE_A // PROJ_TILE)
    is_sigmoid = (j >= COL_GATE_A // PROJ_TILE) & (j < COL_GQ // PROJ_TILE)

    @pl.when(is_silu)
    def _():
        y = tile()
        proj_ref[...] = (y * _sigmoid(y)).astype(BF16)

    @pl.when(is_sigmoid)
    def _():
        proj_ref[...] = _sigmoid(tile()).astype(BF16)

    @pl.when(jnp.logical_not(is_silu | is_sigmoid))
    def _():
        proj_ref[...] = tile().astype(BF16)


def _inproj(x2, norm_w, w_t, w_lr_t, *, tm):
    n = x2.shape[0]
    return pl.pallas_call(
        _inproj_kernel,
        grid=(n // tm, PROJ_WIDTH // PROJ_TILE),
        in_specs=[
            pl.BlockSpec((tm, D_MODEL), lambda i, j: (i, 0)),
            pl.BlockSpec((1, D_MODEL), lambda i, j: (0, 0)),
            pl.BlockSpec((PROJ_TILE, D_MODEL), lambda i, j: (j, 0)),
            pl.BlockSpec((2 * KV_WIDTH, D_MODEL), lambda i, j: (COL_KV // (2 * KV_WIDTH), 0)),
            pl.BlockSpec((GLA_GATE_RANK, D_MODEL), lambda i, j: (0, 0)),
        ],
        out_specs=[
            pl.BlockSpec((tm, PROJ_TILE), lambda i, j: (i, j)),
            pl.BlockSpec((tm, 2 * KV_WIDTH), lambda i, j: (i, 0)),
            pl.BlockSpec((tm, LANES), lambda i, j: (i, 0)),
        ],
        out_shape=[
            jax.ShapeDtypeStruct((n, PROJ_WIDTH), BF16),
            jax.ShapeDtypeStruct((n, 2 * KV_WIDTH), BF16),
            jax.ShapeDtypeStruct((n, LANES), BF16),
        ],
        scratch_shapes=[pltpu.VMEM((tm, D_MODEL), BF16)],
        compiler_params=pltpu.CompilerParams(
            dimension_semantics=("arbitrary", "arbitrary"), vmem_limit_bytes=VMEM_LIMIT),
        name="inproj",
    )(x2, norm_w, w_t, w_t, w_lr_t)


PAIR = 2 * HEAD_DIM
PAIRS_PER_GROUP = GQA_GROUP // 2
STACK = PAIRS_PER_GROUP * WINDOW


def _attn_blocks(sink_ref, q_ref, kv_ref, ga_ref, o_ref, blocks):
    lo = lax.broadcasted_iota(jnp.int32, (1, PAIR), 1) < HEAD_DIM
    top = lax.broadcasted_iota(jnp.int32, (PAIR, 1), 0) < HEAD_DIM
    ki = lax.broadcasted_iota(jnp.int32, (WINDOW, STACK), 0)
    qi = lax.broadcasted_iota(jnp.int32, (WINDOW, STACK), 1) & (WINDOW - 1)
    upper = ki > qi

    def rows(blk):
        return pl.ds(blocks[blk][0], WINDOW)

    def krows(blk):
        _, k_row0, has_prev = blocks[blk]
        return pl.ds(k_row0, 2 * WINDOW if has_prev else WINDOW)

    def halves(pair_tile, g):
        swapped = pltpu.roll(pair_tile, HEAD_DIM, axis=1)
        own, other = (pair_tile, swapped) if g % 2 == 0 else (swapped, pair_tile)
        zero = jnp.zeros_like(pair_tile)
        return jnp.where(lo, own, zero), jnp.where(lo, zero, other)

    def probs(st, sink_row, has_prev):
        if has_prev:
            folded = jnp.where(upper, st[:WINDOW], st[WINDOW:])
        else:
            folded = jnp.where(upper, MASK_VALUE, st)
        m = jnp.maximum(jnp.max(folded, axis=0, keepdims=True), sink_row)
        p = jnp.exp(folded - m)
        inv = 1.0 / (jnp.sum(p, axis=0, keepdims=True) + jnp.exp(sink_row - m))
        if has_prev:
            p = jnp.concatenate([jnp.where(upper, p, 0.0), jnp.where(upper, 0.0, p)], axis=0)
        return p.astype(BF16), inv

    def scores(blk, g):
        k_lanes = slice((g // 2) * PAIR, (g // 2 + 1) * PAIR)
        k_lo, k_hi = halves(kv_ref[0, krows(blk), k_lanes], g)
        first_pair = g * PAIRS_PER_GROUP
        q = jnp.concatenate(
            [q_ref[0, rows(blk), (first_pair + p) * PAIR:(first_pair + p + 1) * PAIR]
             for p in range(PAIRS_PER_GROUP)], axis=0) * (HEAD_DIM ** -0.5)
        return (lax.dot_general(k_lo, q, _NT, preferred_element_type=F32),
                lax.dot_general(k_hi, q, _NT, preferred_element_type=F32))

    def weighted_values(blk, g, st_even, st_odd):
        v_lanes = slice(KV_WIDTH + (g // 2) * PAIR, KV_WIDTH + (g // 2 + 1) * PAIR)
        v_lo, v_hi = halves(kv_ref[0, krows(blk), v_lanes], g)
        first_pair = g * PAIRS_PER_GROUP
        sink_even = jnp.concatenate(
            [jnp.full((1, WINDOW), sink_ref[2 * (first_pair + p)], F32) for p in range(PAIRS_PER_GROUP)], axis=1)
        sink_odd = jnp.concatenate(
            [jnp.full((1, WINDOW), sink_ref[2 * (first_pair + p) + 1], F32) for p in range(PAIRS_PER_GROUP)],
            axis=1)
        p_even, inv_even = probs(st_even, sink_even, blocks[blk][2])
        p_odd, inv_odd = probs(st_odd, sink_odd, blocks[blk][2])
        ot = (lax.dot_general(v_lo, p_even, _TN, preferred_element_type=F32)
              + lax.dot_general(v_hi, p_odd, _TN, preferred_element_type=F32))
        return ot, jnp.where(top, inv_even, inv_odd)

    def finish(blk, g, ot, inv):
        o = (ot * inv).T
        first_pair = g * PAIRS_PER_GROUP
        for p in range(PAIRS_PER_GROUP):
            cols = slice((first_pair + p) * PAIR, (first_pair + p + 1) * PAIR)
            gate = ga_ref[0, rows(blk), cols].astype(F32)
            o_ref[0, rows(blk), cols] = (gate * o[p * WINDOW:(p + 1) * WINDOW]).astype(BF16)

    items = [(blk, g) for blk in range(len(blocks)) for g in range(N_KV_HEADS)]
    st_next = scores(*items[0])
    unfinished = None
    for n, item in enumerate(items):
        st_even, st_odd = st_next
        if n + 1 < len(items):
            st_next = scores(*items[n + 1])
        if unfinished is not None:
            finish(*unfinished)
        unfinished = item + weighted_values(*item, st_even, st_odd)
    finish(*unfinished)


def _attn_kernel(sink_ref, q_ref, kv_ref, ga_ref, *rest, tq, n_cast):
    cast_in, cast_out, o_ref = rest[:n_cast], rest[n_cast:2 * n_cast], rest[2 * n_cast]
    for src, dst in zip(cast_in, cast_out):
        dst[...] = src[0].astype(BF16)

    t = pl.program_id(1)
    later = [(blk * WINDOW, pl.multiple_of(t * tq + (blk - 1) * WINDOW, WINDOW), True)
             for blk in range(1, tq // WINDOW)]

    @pl.when(t == 0)
    def _():
        _attn_blocks(sink_ref, q_ref, kv_ref, ga_ref, o_ref, [(0, 0, False)] + later)

    @pl.when(t > 0)
    def _():
        first = (0, pl.multiple_of(t * tq - WINDOW, WINDOW), True)
        _attn_blocks(sink_ref, q_ref, kv_ref, ga_ref, o_ref, [first] + later)


def _attention(proj3, kv3, sinks, layer, weights, *, tq):
    b, t, _ = proj3.shape
    steps = b * (t // tq)
    cast_in, cast_out, cast_shapes = [], [], []
    for w in weights:
        _, rows, cols = w.shape
        assert rows % (16 * steps) == 0, (rows, steps)
        cast_in.append(pl.BlockSpec((1, rows // steps, cols), lambda i, j: (layer, i * (t // tq) + j, 0)))
        cast_out.append(pl.BlockSpec((rows // steps, cols), lambda i, j: (i * (t // tq) + j, 0)))
        cast_shapes.append(jax.ShapeDtypeStruct((rows, cols), BF16))
    return pl.pallas_call(
        functools.partial(_attn_kernel, tq=tq, n_cast=len(weights)),
        grid=(b, t // tq),
        in_specs=[
            pl.BlockSpec(memory_space=pltpu.SMEM),
            pl.BlockSpec((1, tq, D_MODEL), lambda i, j: (i, j, COL_AQ // D_MODEL)),
            pl.BlockSpec((1, t, 2 * KV_WIDTH), lambda i, j: (i, 0, 0)),
            pl.BlockSpec((1, tq, D_MODEL), lambda i, j: (i, j, COL_GATE_A // D_MODEL)),
        ] + cast_in,
        out_specs=cast_out + [pl.BlockSpec((1, tq, D_MODEL), lambda i, j: (i, j, 0))],
        out_shape=cast_shapes + [jax.ShapeDtypeStruct((b, t, D_MODEL), BF16)],
        compiler_params=pltpu.CompilerParams(
            dimension_semantics=("arbitrary", "arbitrary"), vmem_limit_bytes=VMEM_LIMIT),
        name="swa_attention",
    )(sinks, proj3, kv3, proj3, *weights)


def _log_sigmoid(x):
    return jnp.minimum(x, 0.0) - jnp.log(1.0 + jnp.exp(-jnp.abs(x)))


GLA_SUPER = 2 * GLA_CHUNK


def _split_bf16(x):
    hi = x.astype(BF16)
    return hi, (x - hi.astype(F32)).astype(BF16)


def _gla_kernel(q_ref, k_ref, v_ref, gr_ref, gb_ref, lr_ref, w2_ref, b_ref, nw_ref, o_ref, s_ref, *, tb):
    @pl.when(pl.program_id(2) == 0)
    def _():
        s_ref[...] = jnp.zeros_like(s_ref)

    lr = lr_ref[0]
    logit = jnp.dot(lr, w2_ref[...], preferred_element_type=F32) + b_ref[...]
    log_a = _log_sigmoid(logit) * (1.0 / GLA_GATE_NORMALIZER)

    ri = lax.broadcasted_iota(jnp.int32, (GLA_SUPER, GLA_SUPER), 0)
    ci = lax.broadcasted_iota(jnp.int32, (GLA_SUPER, GLA_SUPER), 1)
    causal = ri >= ci
    cum = (causal & ((ri // GLA_CHUNK) == (ci // GLA_CHUNK))).astype(BF16)
    later_and_all = jnp.concatenate([(ri > ci).astype(BF16), jnp.ones((GLA_SUPER, GLA_SUPER), BF16)], axis=1)
    no_keys = jnp.zeros((GLA_CHUNK, GLA_DK), BF16)

    def decays(c):
        rows = pl.ds(c * GLA_SUPER, GLA_SUPER)
        la = log_a[c * GLA_SUPER:(c + 1) * GLA_SUPER, :]
        la_hi, la_lo = _split_bf16(la)
        g = (jnp.dot(cum, la_hi, preferred_element_type=F32)
             + jnp.dot(cum, la_lo, preferred_element_type=F32))
        lat_hi, lat_lo = _split_bf16(la.T)
        tail = (jnp.dot(lat_hi, later_and_all, preferred_element_type=F32)
                + jnp.dot(lat_lo, later_and_all, preferred_element_type=F32))
        q = q_ref[0, rows, :].astype(F32)
        k = k_ref[0, rows, :].astype(F32)
        q_dec = q * (GLA_DK ** -0.5) * jnp.exp(g)
        carry = jnp.exp(g[GLA_CHUNK - 1:GLA_CHUNK, :])
        q_carry = jnp.concatenate([q_dec[:GLA_CHUNK], q_dec[GLA_CHUNK:] * carry], axis=0).astype(BF16)
        k_inv = (k * jnp.exp(-g)).astype(BF16)
        k_inv_a = jnp.concatenate([k_inv[:GLA_CHUNK], no_keys], axis=0)
        k_inv_b = jnp.concatenate([no_keys, k_inv[GLA_CHUNK:]], axis=0)
        k_end_t = (k.T * jnp.exp(tail[:, :GLA_SUPER])).astype(BF16)
        decay = jnp.exp(tail[:, GLA_SUPER:])
        return q_carry, q_dec.astype(BF16), k_inv_a, k_inv_b, k_end_t, decay

    def intra(c, q_carry, q_dec, k_inv_a, k_inv_b, k_end_t, decay):
        v = v_ref[0, pl.ds(c * GLA_SUPER, GLA_SUPER), :]
        att = (lax.dot_general(q_carry, k_inv_a, _NT, preferred_element_type=F32)
               + lax.dot_general(q_dec, k_inv_b, _NT, preferred_element_type=F32))
        att = jnp.where(causal, att, 0.0).astype(BF16)
        o_intra = jnp.dot(att, v, preferred_element_type=F32)
        upd = jnp.dot(k_end_t, v, preferred_element_type=F32)
        return q_carry, o_intra, upd, decay

    def inter(c, q_carry, o_intra, upd, decay):
        rows = pl.ds(c * GLA_SUPER, GLA_SUPER)
        state = s_ref[...]
        o = o_intra + jnp.dot(q_carry, state.astype(BF16), preferred_element_type=F32)
        s_ref[...] = jnp.concatenate([decay] * (GLA_DV // GLA_SUPER), axis=1) * state + upd
        o = o * _rms_scale(o) * nw_ref[...]
        o = o * gr_ref[0, rows, :].astype(F32) * gb_ref[0, rows, :].astype(F32)
        o_ref[0, rows, :] = o.astype(BF16)

    n = tb // GLA_SUPER
    staged = {}
    for step in range(n + 2):
        if step < n:
            staged[step] = decays(step)
        if 0 <= step - 1 < n:
            staged[step - 1] = intra(step - 1, *staged[step - 1])
        if 0 <= step - 2 < n:
            inter(step - 2, *staged.pop(step - 2))


def _gla(proj3, lr3, w2, bias, norm_w, *, tb):
    b, t, _ = proj3.shape
    return pl.pallas_call(
        functools.partial(_gla_kernel, tb=tb),
        grid=(b, GLA_HEADS, t // tb),
        in_specs=[
            pl.BlockSpec((1, tb, GLA_DK), lambda i, h, j: (i, j, COL_GQ // GLA_DK + h)),
            pl.BlockSpec((1, tb, GLA_DK), lambda i, h, j: (i, j, COL_GK // GLA_DK + h)),
            pl.BlockSpec((1, tb, GLA_DV), lambda i, h, j: (i, j, COL_GV // GLA_DV + h)),
            pl.BlockSpec((1, tb, GLA_DV), lambda i, h, j: (i, j, COL_GR // GLA_DV + h)),
            pl.BlockSpec((1, tb, GLA_DV), lambda i, h, j: (i, j, COL_GATE_B // GLA_DV + h)),
            pl.BlockSpec((1, tb, LANES), lambda i, h, j: (i, j, 0)),
            pl.BlockSpec((LANES, GLA_DK), lambda i, h, j: (0, h)),
            pl.BlockSpec((1, GLA_DK), lambda i, h, j: (0, h)),
            pl.BlockSpec((1, GLA_DV), lambda i, h, j: (0, 0)),
        ],
        out_specs=pl.BlockSpec((1, tb, GLA_DV), lambda i, h, j: (i, j, h)),
        out_shape=jax.ShapeDtypeStruct((b, t, D_MODEL), BF16),
        scratch_shapes=[pltpu.VMEM((GLA_DK, GLA_DV), F32)],
        compiler_params=pltpu.CompilerParams(
            dimension_semantics=("arbitrary", "arbitrary", "arbitrary"), vmem_limit_bytes=VMEM_LIMIT),
        name="gla",
    )(proj3, proj3, proj3, proj3, proj3, lr3, w2, bias, norm_w)


def _outproj_kernel(a_ref, g_ref, x_ref, w_ref, h_ref):
    merged = a_ref[...] + g_ref[...]
    h_ref[...] = x_ref[...] + jnp.dot(merged, w_ref[...], preferred_element_type=F32)


def _outproj(a2, g2, x2, w_out, *, tm):
    n = x2.shape[0]
    return pl.pallas_call(
        _outproj_kernel,
        grid=(n // tm,),
        in_specs=[
            pl.BlockSpec((tm, D_MODEL), lambda i: (i, 0)),
            pl.BlockSpec((tm, D_MODEL), lambda i: (i, 0)),
            pl.BlockSpec((tm, D_MODEL), lambda i: (i, 0)),
            pl.BlockSpec((D_MODEL, D_MODEL), lambda i: (0, 0)),
        ],
        out_specs=pl.BlockSpec((tm, D_MODEL), lambda i: (i, 0)),
        out_shape=jax.ShapeDtypeStruct((n, D_MODEL), F32),
        compiler_params=pltpu.CompilerParams(
            dimension_semantics=("arbitrary",), vmem_limit_bytes=VMEM_LIMIT),
        name="outproj",
    )(a2, g2, x2, w_out)


def _ffn_kernel(h_ref, nw_ref, wg_ref, wu_ref, wd_ref, fw_ref, o_ref, v_ref, *, final_norm):
    j = pl.program_id(1)

    @pl.when(j == 0)
    def _():
        h = h_ref[...]
        v_ref[...] = (h * _rms_scale(h) * nw_ref[...]).astype(BF16)
        o_ref[...] = h

    v = v_ref[...]
    th = wg_ref.shape[1]
    part = None
    halves = [(c * th // 2, (c + 1) * th // 2) for c in range(2)]
    acts = []
    for lo, hi in halves:
        gate = jnp.dot(v, wg_ref[:, lo:hi], preferred_element_type=F32)
        up = jnp.dot(v, wu_ref[:, lo:hi], preferred_element_type=F32)
        acts.append((gate, up))
    for (lo, hi), (gate, up) in zip(halves, acts):
        act = (gate * _sigmoid(gate) * up).astype(BF16)
        contrib = jnp.dot(act, wd_ref[lo:hi, :], preferred_element_type=F32)
        part = contrib if part is None else part + contrib
    o_ref[...] += part

    if final_norm:
        @pl.when(j == pl.num_programs(1) - 1)
        def _():
            y = o_ref[...]
            o_ref[...] = y * _rms_scale(y) * fw_ref[...]


def _ffn(h2, norm_w, w_gate, w_up, w_down, final_w, *, tm, th, final_norm):
    n = h2.shape[0]
    return pl.pallas_call(
        functools.partial(_ffn_kernel, final_norm=final_norm),
        grid=(n // tm, FFN_HIDDEN // th),
        in_specs=[
            pl.BlockSpec((tm, D_MODEL), lambda i, j: (i, 0)),
            pl.BlockSpec((1, D_MODEL), lambda i, j: (0, 0)),
            pl.BlockSpec((D_MODEL, th), lambda i, j: (0, j)),
            pl.BlockSpec((D_MODEL, th), lambda i, j: (0, j)),
            pl.BlockSpec((th, D_MODEL), lambda i, j: (j, 0)),
            pl.BlockSpec((1, D_MODEL), lambda i, j: (0, 0)),
        ],
        out_specs=pl.BlockSpec((tm, D_MODEL), lambda i, j: (i, 0)),
        out_shape=jax.ShapeDtypeStruct((n, D_MODEL), F32),
        scratch_shapes=[pltpu.VMEM((tm, D_MODEL), BF16)],
        compiler_params=pltpu.CompilerParams(
            dimension_semantics=("arbitrary", "arbitrary"), vmem_limit_bytes=VMEM_LIMIT),
        name="ffn",
    )(h2, norm_w, w_gate, w_up, w_down, final_w)


_IN_WIDTHS = (D_MODEL, KV_WIDTH, KV_WIDTH, GLA_HEADS * GLA_DK, GLA_HEADS * GLA_DK, D_MODEL,
              GLA_GATE_RANK, D_MODEL, D_MODEL, D_MODEL)
(IN_AQ, IN_AK, IN_AV, IN_GQ, IN_GK, IN_GV, IN_LR, IN_GR, IN_GATE_A, IN_GATE_B, D_IN) = (
    sum(_IN_WIDTHS[:i]) for i in range(len(_IN_WIDTHS) + 1))


REGROUP_ROWS = 2 * KV_WIDTH


def _regroup_source_row(i):
    row = i * REGROUP_ROWS
    src = row - COL_AQ + IN_AQ
    src = jnp.where(row >= COL_GV, row - COL_GV + IN_GV, src)
    src = jnp.where(row >= COL_GR, row - COL_GR + IN_GR, src)
    src = jnp.where(row >= COL_GQ, row - COL_GQ + IN_GQ, src)
    src = jnp.where(row >= COL_KV, row - COL_KV + IN_AK, src)
    return pl.multiple_of(src, GLA_GATE_RANK)


def _regroup_kernel(wt_ref, o_ref):
    o_ref[...] = wt_ref[0].astype(BF16)


def _regroup_w_in(w_in_t, layer):
    rows = COL_KV + 2 * KV_WIDTH
    return pl.pallas_call(
        _regroup_kernel,
        grid=(rows // REGROUP_ROWS,),
        in_specs=[pl.BlockSpec((pl.Element(1), pl.Element(REGROUP_ROWS), pl.Element(D_MODEL)),
                               lambda i: (layer, _regroup_source_row(i), 0))],
        out_specs=pl.BlockSpec((REGROUP_ROWS, D_MODEL), lambda i: (i, 0)),
        out_shape=jax.ShapeDtypeStruct((rows, D_MODEL), BF16),
        compiler_params=pltpu.CompilerParams(
            dimension_semantics=("arbitrary",), vmem_limit_bytes=VMEM_LIMIT),
        name="regroup_w_in",
    )(w_in_t)


def _tile(n, pref):
    return pref if n % pref == 0 else n


def kernel(x, norm1_w, w_in, gla_gate_w2, gla_gate_b, attn_sinks, gla_norm_w, w_out, norm2_w,
           w_ffn_gate, w_ffn_up, w_ffn_down, final_norm_w):
    b, t, d = x.shape
    n = b * t
    depth = w_in.shape[0]
    h2 = x.reshape(n, d)
    for l in range(depth):
        w_in_t = jnp.swapaxes(w_in, 1, 2)
        proj, kv, lr = _inproj(h2, norm1_w[l].reshape(1, d), _regroup_w_in(w_in_t, l),
                               w_in_t[l, IN_LR:IN_LR + GLA_GATE_RANK], tm=_tile(n, 1024))
        proj3 = proj.reshape(b, t, PROJ_WIDTH)

        wo, wg, wu, wd, a = _attention(proj3, kv.reshape(b, t, 2 * KV_WIDTH), attn_sinks[l], l,
                                       (w_out, w_ffn_gate, w_ffn_up, w_ffn_down), tq=_tile(t, 512))

        w2 = jnp.pad(gla_gate_w2[l], ((0, LANES - GLA_GATE_RANK), (0, 0))).astype(BF16)
        g = _gla(proj3, lr.reshape(b, t, LANES), w2, gla_gate_b[l].reshape(1, -1),
                 gla_norm_w[l].reshape(1, GLA_DV), tb=_tile(t, 2048))

        h2 = _outproj(a.reshape(n, d), g.reshape(n, d), h2, wo, tm=_tile(n, 512))
        h2 = _ffn(h2, norm2_w[l].reshape(1, d), wg, wu, wd, final_norm_w.reshape(1, d),
                  tm=_tile(n, 1024), th=512, final_norm=l == depth - 1)
    return h2.reshape(b, t, d)
```

```python
import functools

import jax
import jax.numpy as jnp
from jax import lax
from jax.experimental import pallas as pl
from jax.experimental.pallas import tpu as pltpu

F32 = jnp.float32
BF16 = jnp.bfloat16

D_MODEL = 2048
HEAD_DIM = 64
N_Q_HEADS = D_MODEL // HEAD_DIM
N_KV_HEADS = 4
GQA_GROUP = N_Q_HEADS // N_KV_HEADS
KV_WIDTH = N_KV_HEADS * HEAD_DIM
WINDOW = 128

GLA_HEADS = 4
GLA_DK = (D_MODEL // 2) // GLA_HEADS
GLA_DV = D_MODEL // GLA_HEADS
GLA_GATE_RANK = 16
GLA_GATE_NORMALIZER = 16.0
GLA_CHUNK = 64

FFN_HIDDEN = ((8 * D_MODEL // 3 + 255) // 256) * 256
RMS_EPS = 1e-6
MASK_VALUE = -1e30

LANES = 128

COL_AQ = 0
COL_GV = COL_AQ + D_MODEL
COL_GR = COL_GV + D_MODEL
COL_GATE_A = COL_GR + D_MODEL
COL_GATE_B = COL_GATE_A + D_MODEL
COL_GQ = COL_GATE_B + D_MODEL
COL_GK = COL_GQ + GLA_HEADS * GLA_DK
PROJ_WIDTH = COL_GK + GLA_HEADS * GLA_DK
PROJ_TILE = 2048
COL_KV = PROJ_WIDTH

VMEM_LIMIT = 56 * 1024 * 1024

_NT = (((1,), (1,)), ((), ()))
_TN = (((0,), (0,)), ((), ()))


def _rms_scale(x):
    return lax.rsqrt(jnp.mean(x * x, axis=-1, keepdims=True) + RMS_EPS)


def _sigmoid(x):
    return 0.5 * jnp.tanh(0.5 * x) + 0.5


def _inproj_kernel(x_ref, nw_ref, w_ref, wkv_ref, wlr_ref, proj_ref, kv_ref, lr_ref, u_ref):
    j = pl.program_id(1)

    @pl.when(j == 0)
    def _():
        x = x_ref[...]
        u = (x * _rms_scale(x) * nw_ref[...]).astype(BF16)
        u_ref[...] = u
        kv_ref[...] = lax.dot_general(u, wkv_ref[...], _NT, preferred_element_type=F32).astype(BF16)
        w_lr = wlr_ref[...].astype(BF16)
        w_lr = jnp.concatenate([w_lr, jnp.zeros((LANES - GLA_GATE_RANK, D_MODEL), BF16)], axis=0)
        lr_ref[...] = lax.dot_general(u, w_lr, _NT, preferred_element_type=F32).astype(BF16)

    def tile():
        return lax.dot_general(u_ref[...], w_ref[...], _NT, preferred_element_type=F32)

    is_silu = (j >= COL_GR // PROJ_TILE) & (j < COL_GATE_A // PROJ_TILE)
    is_sigmoid = (j >= COL_GATE_A // PROJ_TILE) & (j < COL_GQ // PROJ_TILE)

    @pl.when(is_silu)
    def _():
        y = tile()
        proj_ref[...] = (y * _sigmoid(y)).astype(BF16)

    @pl.when(is_sigmoid)
    def _():
        proj_ref[...] = _sigmoid(tile()).astype(BF16)

    @pl.when(jnp.logical_not(is_silu | is_sigmoid))
    def _():
        proj_ref[...] = tile().astype(BF16)


def _inproj(x2, norm_w, w_t, w_lr_t, *, tm):
    n = x2.shape[0]
    return pl.pallas_call(
        _inproj_kernel,
        grid=(n // tm, PROJ_WIDTH // PROJ_TILE),
        in_specs=[
            pl.BlockSpec((tm, D_MODEL), lambda i, j: (i, 0)),
            pl.BlockSpec((1, D_MODEL), lambda i, j: (0, 0)),
            pl.BlockSpec((PROJ_TILE, D_MODEL), lambda i, j: (j, 0)),
            pl.BlockSpec((2 * KV_WIDTH, D_MODEL), lambda i, j: (COL_KV // (2 * KV_WIDTH), 0)),
            pl.BlockSpec((GLA_GATE_RANK, D_MODEL), lambda i, j: (0, 0)),
        ],
        out_specs=[
            pl.BlockSpec((tm, PROJ_TILE), lambda i, j: (i, j)),
            pl.BlockSpec((tm, 2 * KV_WIDTH), lambda i, j: (i, 0)),
            pl.BlockSpec((tm, LANES), lambda i, j: (i, 0)),
        ],
        out_shape=[
            jax.ShapeDtypeStruct((n, PROJ_WIDTH), BF16),
            jax.ShapeDtypeStruct((n, 2 * KV_WIDTH), BF16),
            jax.ShapeDtypeStruct((n, LANES), BF16),
        ],
        scratch_shapes=[pltpu.VMEM((tm, D_MODEL), BF16)],
        compiler_params=pltpu.CompilerParams(
            dimension_semantics=("arbitrary", "arbitrary"), vmem_limit_bytes=VMEM_LIMIT),
        name="inproj",
    )(x2, norm_w, w_t, w_t, w_lr_t)


PAIR = 2 * HEAD_DIM
PAIRS_PER_GROUP = GQA_GROUP // 2
STACK = PAIRS_PER_GROUP * WINDOW


def _attn_blocks(sink_ref, q_ref, kv_ref, ga_ref, o_ref, blocks):
    lo = lax.broadcasted_iota(jnp.int32, (1, PAIR), 1) < HEAD_DIM
    top = lax.broadcasted_iota(jnp.int32, (PAIR, 1), 0) < HEAD_DIM
    ki = lax.broadcasted_iota(jnp.int32, (WINDOW, STACK), 0)
    qi = lax.broadcasted_iota(jnp.int32, (WINDOW, STACK), 1) & (WINDOW - 1)
    upper = ki > qi

    def rows(blk):
        return pl.ds(blocks[blk][0], WINDOW)

    def krows(blk):
        _, k_row0, has_prev = blocks[blk]
        return pl.ds(k_row0, 2 * WINDOW if has_prev else WINDOW)

    def halves(pair_tile, g):
        swapped = pltpu.roll(pair_tile, HEAD_DIM, axis=1)
        own, other = (pair_tile, swapped) if g % 2 == 0 else (swapped, pair_tile)
        zero = jnp.zeros_like(pair_tile)
        return jnp.where(lo, own, zero), jnp.where(lo, zero, other)

    def probs(st, sink_row, has_prev):
        if has_prev:
            folded = jnp.where(upper, st[:WINDOW], st[WINDOW:])
        else:
            folded = jnp.where(upper, MASK_VALUE, st)
        m = jnp.maximum(jnp.max(folded, axis=0, keepdims=True), sink_row)
        p = jnp.exp(folded - m)
        inv = 1.0 / (jnp.sum(p, axis=0, keepdims=True) + jnp.exp(sink_row - m))
        if has_prev:
            p = jnp.concatenate([jnp.where(upper, p, 0.0), jnp.where(upper, 0.0, p)], axis=0)
        return p.astype(BF16), inv

    def scores(blk, g):
        k_lanes = slice((g // 2) * PAIR, (g // 2 + 1) * PAIR)
        k_lo, k_hi = halves(kv_ref[0, krows(blk), k_lanes], g)
        first_pair = g * PAIRS_PER_GROUP
        q = jnp.concatenate(
            [q_ref[0, rows(blk), (first_pair + p) * PAIR:(first_pair + p + 1) * PAIR]
             for p in range(PAIRS_PER_GROUP)], axis=0)
        return (lax.dot_general(k_lo, q, _NT, preferred_element_type=F32),
                lax.dot_general(k_hi, q, _NT, preferred_element_type=F32))

    def weighted_values(blk, g, st_even, st_odd):
        v_lanes = slice(KV_WIDTH + (g // 2) * PAIR, KV_WIDTH + (g // 2 + 1) * PAIR)
        v_lo, v_hi = halves(kv_ref[0, krows(blk), v_lanes], g)
        first_pair = g * PAIRS_PER_GROUP
        sink_even = jnp.concatenate(
            [jnp.full((1, WINDOW), sink_ref[2 * (first_pair + p)], F32) for p in range(PAIRS_PER_GROUP)], axis=1)
        sink_odd = jnp.concatenate(
            [jnp.full((1, WINDOW), sink_ref[2 * (first_pair + p) + 1], F32) for p in range(PAIRS_PER_GROUP)],
            axis=1)
        p_even, inv_even = probs(st_even, sink_even, blocks[blk][2])
        p_odd, inv_odd = probs(st_odd, sink_odd, blocks[blk][2])
        ot = (lax.dot_general(v_lo, p_even, _TN, preferred_element_type=F32)
              + lax.dot_general(v_hi, p_odd, _TN, preferred_element_type=F32))
        return ot, jnp.where(top, inv_even, inv_odd)

    def finish(blk, g, ot, inv):
        o = (ot * inv).T
        first_pair = g * PAIRS_PER_GROUP
        for p in range(PAIRS_PER_GROUP):
            cols = slice((first_pair + p) * PAIR, (first_pair + p + 1) * PAIR)
            gate = ga_ref[0, rows(blk), cols].astype(F32)
            o_ref[0, rows(blk), cols] = (gate * o[p * WINDOW:(p + 1) * WINDOW]).astype(BF16)

    items = [(blk, g) for blk in range(len(blocks)) for g in range(N_KV_HEADS)]
    st_next = scores(*items[0])
    unfinished = None
    for n, item in enumerate(items):
        st_even, st_odd = st_next
        if n + 1 < len(items):
            st_next = scores(*items[n + 1])
        if unfinished is not None:
            finish(*unfinished)
        unfinished = item + weighted_values(*item, st_even, st_odd)
    finish(*unfinished)


def _attn_kernel(sink_ref, q_ref, kv_ref, ga_ref, *rest, tq, n_cast):
    cast_in, cast_out, o_ref = rest[:n_cast], rest[n_cast:2 * n_cast], rest[2 * n_cast]
    for src, dst in zip(cast_in, cast_out):
        dst[...] = src[0].astype(BF16)

    t = pl.program_id(1)
    later = [(blk * WINDOW, pl.multiple_of(t * tq + (blk - 1) * WINDOW, WINDOW), True)
             for blk in range(1, tq // WINDOW)]

    @pl.when(t == 0)
    def _():
        _attn_blocks(sink_ref, q_ref, kv_ref, ga_ref, o_ref, [(0, 0, False)] + later)

    @pl.when(t > 0)
    def _():
        first = (0, pl.multiple_of(t * tq - WINDOW, WINDOW), True)
        _attn_blocks(sink_ref, q_ref, kv_ref, ga_ref, o_ref, [first] + later)


def _attention(proj3, kv3, sinks, layer, weights, *, tq):
    b, t, _ = proj3.shape
    steps = b * (t // tq)
    cast_in, cast_out, cast_shapes = [], [], []
    for w in weights:
        _, rows, cols = w.shape
        assert rows % (16 * steps) == 0, (rows, steps)
        cast_in.append(pl.BlockSpec((1, rows // steps, cols), lambda i, j: (layer, i * (t // tq) + j, 0)))
        cast_out.append(pl.BlockSpec((rows // steps, cols), lambda i, j: (i * (t // tq) + j, 0)))
        cast_shapes.append(jax.ShapeDtypeStruct((rows, cols), BF16))
    return pl.pallas_call(
        functools.partial(_attn_kernel, tq=tq, n_cast=len(weights)),
        grid=(b, t // tq),
        in_specs=[
            pl.BlockSpec(memory_space=pltpu.SMEM),
            pl.BlockSpec((1, tq, D_MODEL), lambda i, j: (i, j, COL_AQ // D_MODEL)),
            pl.BlockSpec((1, t, 2 * KV_WIDTH), lambda i, j: (i, 0, 0)),
            pl.BlockSpec((1, tq, D_MODEL), lambda i, j: (i, j, COL_GATE_A // D_MODEL)),
        ] + cast_in,
        out_specs=cast_out + [pl.BlockSpec((1, tq, D_MODEL), lambda i, j: (i, j, 0))],
        out_shape=cast_shapes + [jax.ShapeDtypeStruct((b, t, D_MODEL), BF16)],
        compiler_params=pltpu.CompilerParams(
            dimension_semantics=("arbitrary", "arbitrary"), vmem_limit_bytes=VMEM_LIMIT),
        name="swa_attention",
    )(sinks, proj3, kv3, proj3, *weights)


def _log_sigmoid(x):
    return jnp.minimum(x, 0.0) - jnp.log(1.0 + jnp.exp(-jnp.abs(x)))


GLA_SUPER = 2 * GLA_CHUNK


def _split_bf16(x):
    hi = x.astype(BF16)
    return hi, (x - hi.astype(F32)).astype(BF16)


def _gla_kernel(q_ref, k_ref, v_ref, gr_ref, gb_ref, lr_ref, w2_ref, b_ref, nw_ref, o_ref, s_ref, *, tb):
    @pl.when(pl.program_id(2) == 0)
    def _():
        s_ref[...] = jnp.zeros_like(s_ref)

    lr = lr_ref[0]
    logit = jnp.dot(lr, w2_ref[...], preferred_element_type=F32) + b_ref[...]
    log_a = _log_sigmoid(logit) * (1.0 / GLA_GATE_NORMALIZER)

    ri = lax.broadcasted_iota(jnp.int32, (GLA_SUPER, GLA_SUPER), 0)
    ci = lax.broadcasted_iota(jnp.int32, (GLA_SUPER, GLA_SUPER), 1)
    causal = ri >= ci
    cum = (causal & ((ri // GLA_CHUNK) == (ci // GLA_CHUNK))).astype(BF16)
    cum_twice = jnp.concatenate([cum, cum], axis=1)
    later_and_all = jnp.concatenate([(ri > ci).astype(BF16), jnp.ones((GLA_SUPER, GLA_SUPER), BF16)], axis=1)
    later_and_all_twice = jnp.concatenate([later_and_all, later_and_all], axis=0)
    no_keys = jnp.zeros((GLA_CHUNK, GLA_DK), BF16)

    def decays(c):
        rows = pl.ds(c * GLA_SUPER, GLA_SUPER)
        la = log_a[c * GLA_SUPER:(c + 1) * GLA_SUPER, :]
        g = jnp.dot(cum_twice, jnp.concatenate(_split_bf16(la), axis=0),
                    preferred_element_type=F32)
        tail = jnp.dot(jnp.concatenate(_split_bf16(la.T), axis=1), later_and_all_twice,
                       preferred_element_type=F32)
        q = q_ref[0, rows, :].astype(F32)
        k = k_ref[0, rows, :].astype(F32)
        q_dec = q * jnp.exp(g)
        carry = jnp.exp(g[GLA_CHUNK - 1:GLA_CHUNK, :])
        q_carry = jnp.concatenate([q_dec[:GLA_CHUNK], q_dec[GLA_CHUNK:] * carry], axis=0).astype(BF16)
        k_inv = (k * jnp.exp(-g)).astype(BF16)
        k_inv_a = jnp.concatenate([k_inv[:GLA_CHUNK], no_keys], axis=0)
        k_inv_b = jnp.concatenate([no_keys, k_inv[GLA_CHUNK:]], axis=0)
        k_end_t = (k.T * jnp.exp(tail[:, :GLA_SUPER])).astype(BF16)
        decay = jnp.exp(tail[:, GLA_SUPER:])
        return q_carry, q_dec.astype(BF16), k_inv_a, k_inv_b, k_end_t, decay

    def intra(c, q_carry, q_dec, k_inv_a, k_inv_b, k_end_t, decay):
        v = v_ref[0, pl.ds(c * GLA_SUPER, GLA_SUPER), :]
        att = (lax.dot_general(q_carry, k_inv_a, _NT, preferred_element_type=F32)
               + lax.dot_general(q_dec, k_inv_b, _NT, preferred_element_type=F32))
        att = jnp.where(causal, att, 0.0).astype(BF16)
        o_intra = jnp.dot(att, v, preferred_element_type=F32)
        upd = jnp.dot(k_end_t, v, preferred_element_type=F32)
        return q_carry, o_intra, upd, decay

    def inter(c, q_carry, o_intra, upd, decay):
        rows = pl.ds(c * GLA_SUPER, GLA_SUPER)
        state = s_ref[...]
        o = o_intra + jnp.dot(q_carry, state.astype(BF16), preferred_element_type=F32)
        s_ref[...] = jnp.concatenate([decay] * (GLA_DV // GLA_SUPER), axis=1) * state + upd
        o = o * _rms_scale(o) * nw_ref[...]
        o = o * (gr_ref[0, rows, :] * gb_ref[0, rows, :]).astype(F32)
        o_ref[0, rows, :] = o.astype(BF16)

    n = tb // GLA_SUPER
    staged = {}
    for step in range(n + 2):
        if step < n:
            staged[step] = decays(step)
        if 0 <= step - 1 < n:
            staged[step - 1] = intra(step - 1, *staged[step - 1])
        if 0 <= step - 2 < n:
            inter(step - 2, *staged.pop(step - 2))


def _gla(proj3, lr3, w2, bias, norm_w, *, tb):
    b, t, _ = proj3.shape
    return pl.pallas_call(
        functools.partial(_gla_kernel, tb=tb),
        grid=(b, GLA_HEADS, t // tb),
        in_specs=[
            pl.BlockSpec((1, tb, GLA_DK), lambda i, h, j: (i, j, COL_GQ // GLA_DK + h)),
            pl.BlockSpec((1, tb, GLA_DK), lambda i, h, j: (i, j, COL_GK // GLA_DK + h)),
            pl.BlockSpec((1, tb, GLA_DV), lambda i, h, j: (i, j, COL_GV // GLA_DV + h)),
            pl.BlockSpec((1, tb, GLA_DV), lambda i, h, j: (i, j, COL_GR // GLA_DV + h)),
            pl.BlockSpec((1, tb, GLA_DV), lambda i, h, j: (i, j, COL_GATE_B // GLA_DV + h)),
            pl.BlockSpec((1, tb, LANES), lambda i, h, j: (i, j, 0)),
            pl.BlockSpec((LANES, GLA_DK), lambda i, h, j: (0, h)),
            pl.BlockSpec((1, GLA_DK), lambda i, h, j: (0, h)),
            pl.BlockSpec((1, GLA_DV), lambda i, h, j: (0, 0)),
        ],
        out_specs=pl.BlockSpec((1, tb, GLA_DV), lambda i, h, j: (i, j, h)),
        out_shape=jax.ShapeDtypeStruct((b, t, D_MODEL), BF16),
        scratch_shapes=[pltpu.VMEM((GLA_DK, GLA_DV), F32)],
        compiler_params=pltpu.CompilerParams(
            dimension_semantics=("arbitrary", "arbitrary", "arbitrary"), vmem_limit_bytes=VMEM_LIMIT),
        name="gla",
    )(proj3, proj3, proj3, proj3, proj3, lr3, w2, bias, norm_w)


def _outproj_kernel(a_ref, g_ref, x_ref, w_ref, h_ref):
    merged = a_ref[...] + g_ref[...]
    h_ref[...] = x_ref[...] + jnp.dot(merged, w_ref[...], preferred_element_type=F32)


def _outproj(a2, g2, x2, w_out, *, tm):
    n = x2.shape[0]
    return pl.pallas_call(
        _outproj_kernel,
        grid=(n // tm,),
        in_specs=[
            pl.BlockSpec((tm, D_MODEL), lambda i: (i, 0)),
            pl.BlockSpec((tm, D_MODEL), lambda i: (i, 0)),
            pl.BlockSpec((tm, D_MODEL), lambda i: (i, 0)),
            pl.BlockSpec((D_MODEL, D_MODEL), lambda i: (0, 0)),
        ],
        out_specs=pl.BlockSpec((tm, D_MODEL), lambda i: (i, 0)),
        out_shape=jax.ShapeDtypeStruct((n, D_MODEL), F32),
        compiler_params=pltpu.CompilerParams(
            dimension_semantics=("arbitrary",), vmem_limit_bytes=VMEM_LIMIT),
        name="outproj",
    )(a2, g2, x2, w_out)


def _ffn_kernel(h_ref, nw_ref, wg_ref, wu_ref, wd_ref, fw_ref, o_ref, v_ref, *, final_norm):
    j = pl.program_id(1)

    @pl.when(j == 0)
    def _():
        h = h_ref[...]
        v_ref[...] = (h * _rms_scale(h) * nw_ref[...]).astype(BF16)
        o_ref[...] = h

    v = v_ref[...]
    th = wg_ref.shape[1]
    part = None
    halves = [(c * th // 2, (c + 1) * th // 2) for c in range(2)]
    acts = []
    for lo, hi in halves:
        gate = jnp.dot(v, wg_ref[:, lo:hi], preferred_element_type=F32)
        up = jnp.dot(v, wu_ref[:, lo:hi], preferred_element_type=F32)
        acts.append((gate, up))
    for (lo, hi), (gate, up) in zip(halves, acts):
        act = (gate * _sigmoid(gate) * up).astype(BF16)
        contrib = jnp.dot(act, wd_ref[lo:hi, :], preferred_element_type=F32)
        part = contrib if part is None else part + contrib
    o_ref[...] += part

    if final_norm:
        @pl.when(j == pl.num_programs(1) - 1)
        def _():
            y = o_ref[...]
            o_ref[...] = y * _rms_scale(y) * fw_ref[...]


def _ffn(h2, norm_w, w_gate, w_up, w_down, final_w, *, tm, th, final_norm):
    n = h2.shape[0]
    return pl.pallas_call(
        functools.partial(_ffn_kernel, final_norm=final_norm),
        grid=(n // tm, FFN_HIDDEN // th),
        in_specs=[
            pl.BlockSpec((tm, D_MODEL), lambda i, j: (i, 0)),
            pl.BlockSpec((1, D_MODEL), lambda i, j: (0, 0)),
            pl.BlockSpec((D_MODEL, th), lambda i, j: (0, j)),
            pl.BlockSpec((D_MODEL, th), lambda i, j: (0, j)),
            pl.BlockSpec((th, D_MODEL), lambda i, j: (j, 0)),
            pl.BlockSpec((1, D_MODEL), lambda i, j: (0, 0)),
        ],
        out_specs=pl.BlockSpec((tm, D_MODEL), lambda i, j: (i, 0)),
        out_shape=jax.ShapeDtypeStruct((n, D_MODEL), F32),
        scratch_shapes=[pltpu.VMEM((tm, D_MODEL), BF16)],
        compiler_params=pltpu.CompilerParams(
            dimension_semantics=("arbitrary", "arbitrary"), vmem_limit_bytes=VMEM_LIMIT),
        name="ffn",
    )(h2, norm_w, w_gate, w_up, w_down, final_w)


_IN_WIDTHS = (D_MODEL, KV_WIDTH, KV_WIDTH, GLA_HEADS * GLA_DK, GLA_HEADS * GLA_DK, D_MODEL,
              GLA_GATE_RANK, D_MODEL, D_MODEL, D_MODEL)
(IN_AQ, IN_AK, IN_AV, IN_GQ, IN_GK, IN_GV, IN_LR, IN_GR, IN_GATE_A, IN_GATE_B, D_IN) = (
    sum(_IN_WIDTHS[:i]) for i in range(len(_IN_WIDTHS) + 1))


REGROUP_ROWS = 2 * KV_WIDTH
ATTN_Q_SCALE = HEAD_DIM ** -0.5
GLA_Q_SCALE = GLA_DK ** -0.5
assert ATTN_Q_SCALE == 2.0 ** -3 and GLA_Q_SCALE == 2.0 ** -4


def _regroup_source_row(i):
    row = i * REGROUP_ROWS
    src = row - COL_AQ + IN_AQ
    src = jnp.where(row >= COL_GV, row - COL_GV + IN_GV, src)
    src = jnp.where(row >= COL_GR, row - COL_GR + IN_GR, src)
    src = jnp.where(row >= COL_GQ, row - COL_GQ + IN_GQ, src)
    src = jnp.where(row >= COL_KV, row - COL_KV + IN_AK, src)
    return pl.multiple_of(src, GLA_GATE_RANK)


def _regroup_kernel(wt_ref, o_ref):
    row = pl.program_id(0) * REGROUP_ROWS
    scale = jnp.where(row < COL_AQ + D_MODEL, ATTN_Q_SCALE, 1.0)
    scale = jnp.where((row >= COL_GQ) & (row < COL_GK), GLA_Q_SCALE, scale)
    o_ref[...] = (wt_ref[0] * scale).astype(BF16)


def _regroup_w_in(w_in_t, layer):
    rows = COL_KV + 2 * KV_WIDTH
    return pl.pallas_call(
        _regroup_kernel,
        grid=(rows // REGROUP_ROWS,),
        in_specs=[pl.BlockSpec((pl.Element(1), pl.Element(REGROUP_ROWS), pl.Element(D_MODEL)),
                               lambda i: (layer, _regroup_source_row(i), 0))],
        out_specs=pl.BlockSpec((REGROUP_ROWS, D_MODEL), lambda i: (i, 0)),
        out_shape=jax.ShapeDtypeStruct((rows, D_MODEL), BF16),
        compiler_params=pltpu.CompilerParams(
            dimension_semantics=("arbitrary",), vmem_limit_bytes=VMEM_LIMIT),
        name="regroup_w_in",
    )(w_in_t)


def _tile(n, pref):
    return pref if n % pref == 0 else n


def kernel(x, norm1_w, w_in, gla_gate_w2, gla_gate_b, attn_sinks, gla_norm_w, w_out, norm2_w,
           w_ffn_gate, w_ffn_up, w_ffn_down, final_norm_w):
    b, t, d = x.shape
    n = b * t
    depth = w_in.shape[0]
    h2 = x.reshape(n, d)
    for l in range(depth):
        w_in_t = jnp.swapaxes(w_in, 1, 2)
        proj, kv, lr = _inproj(h2, norm1_w[l].reshape(1, d), _regroup_w_in(w_in_t, l),
                               w_in_t[l, IN_LR:IN_LR + GLA_GATE_RANK], tm=_tile(n, 1024))
        proj3 = proj.reshape(b, t, PROJ_WIDTH)

        wo, wg, wu, wd, a = _attention(proj3, kv.reshape(b, t, 2 * KV_WIDTH), attn_sinks[l], l,
                                       (w_out, w_ffn_gate, w_ffn_up, w_ffn_down), tq=_tile(t, 512))

        w2 = jnp.pad(gla_gate_w2[l], ((0, LANES - GLA_GATE_RANK), (0, 0))).astype(BF16)
        g = _gla(proj3, lr.reshape(b, t, LANES), w2, gla_gate_b[l].reshape(1, -1),
                 gla_norm_w[l].reshape(1, GLA_DV), tb=_tile(t, 2048))

        h2 = _outproj(a.reshape(n, d), g.reshape(n, d), h2, wo, tm=_tile(n, 512))
        h2 = _ffn(h2, norm2_w[l].reshape(1, d), wg, wu, wd, final_norm_w.reshape(1, d),
                  tm=_tile(n, 1024), th=512, final_norm=l == depth - 1)
    return h2.reshape(b, t, d)
```

```python
import functools

import jax
import jax.numpy as jnp
from jax import lax
from jax.experimental import pallas as pl
from jax.experimental.pallas import tpu as pltpu

F32 = jnp.float32
BF16 = jnp.bfloat16

D_MODEL = 2048
HEAD_DIM = 64
N_Q_HEADS = D_MODEL // HEAD_DIM
N_KV_HEADS = 4
GQA_GROUP = N_Q_HEADS // N_KV_HEADS
KV_WIDTH = N_KV_HEADS * HEAD_DIM
WINDOW = 128

GLA_HEADS = 4
GLA_DK = (D_MODEL // 2) // GLA_HEADS
GLA_DV = D_MODEL // GLA_HEADS
GLA_GATE_RANK = 16
GLA_GATE_NORMALIZER = 16.0
GLA_CHUNK = 64

FFN_HIDDEN = ((8 * D_MODEL // 3 + 255) // 256) * 256
RMS_EPS = 1e-6
MASK_VALUE = -1e30

LANES = 128

COL_AQ = 0
COL_GV = COL_AQ + D_MODEL
COL_GR = COL_GV + D_MODEL
COL_GATE_A = COL_GR + D_MODEL
COL_GATE_B = COL_GATE_A + D_MODEL
COL_GQ = COL_GATE_B + D_MODEL
COL_GK = COL_GQ + GLA_HEADS * GLA_DK
PROJ_WIDTH = COL_GK + GLA_HEADS * GLA_DK
PROJ_TILE = 2048
COL_KV = PROJ_WIDTH

VMEM_LIMIT = 56 * 1024 * 1024

_NT = (((1,), (1,)), ((), ()))
_TN = (((0,), (0,)), ((), ()))


def _rms_scale(x):
    return lax.rsqrt(jnp.mean(x * x, axis=-1, keepdims=True) + RMS_EPS)


def _sigmoid(x):
    return 0.5 * jnp.tanh(0.5 * x) + 0.5


def _inproj_kernel(x_ref, nw_ref, w_ref, wkv_ref, wlr_ref, proj_ref, kv_ref, lr_ref, u_ref):
    j = pl.program_id(1)

    @pl.when(j == 0)
    def _():
        x = x_ref[...]
        u = (x * _rms_scale(x) * nw_ref[...]).astype(BF16)
        u_ref[...] = u
        kv_ref[...] = lax.dot_general(u, wkv_ref[...], _NT, preferred_element_type=F32).astype(BF16)
        w_lr = wlr_ref[...].astype(BF16)
        w_lr = jnp.concatenate([w_lr, jnp.zeros((LANES - GLA_GATE_RANK, D_MODEL), BF16)], axis=0)
        lr_ref[...] = lax.dot_general(u, w_lr, _NT, preferred_element_type=F32).astype(BF16)

    def tile():
        return lax.dot_general(u_ref[...], w_ref[...], _NT, preferred_element_type=F32)

    is_silu = (j >= COL_GR // PROJ_TILE) & (j < COL_GATE_A // PROJ_TILE)
    is_sigmoid = (j >= COL_GATE_A // PROJ_TILE) & (j < COL_GQ // PROJ_TILE)

    @pl.when(is_silu)
    def _():
        y = tile()
        proj_ref[...] = (y * _sigmoid(y)).astype(BF16)

    @pl.when(is_sigmoid)
    def _():
        proj_ref[...] = _sigmoid(tile()).astype(BF16)

    @pl.when(jnp.logical_not(is_silu | is_sigmoid))
    def _():
        proj_ref[...] = tile().astype(BF16)


def _inproj(x2, norm_w, w_t, w_lr_t, *, tm):
    n = x2.shape[0]
    return pl.pallas_call(
        _inproj_kernel,
        grid=(n // tm, PROJ_WIDTH // PROJ_TILE),
        in_specs=[
            pl.BlockSpec((tm, D_MODEL), lambda i, j: (i, 0)),
            pl.BlockSpec((1, D_MODEL), lambda i, j: (0, 0)),
            pl.BlockSpec((PROJ_TILE, D_MODEL), lambda i, j: (j, 0)),
            pl.BlockSpec((2 * KV_WIDTH, D_MODEL), lambda i, j: (COL_KV // (2 * KV_WIDTH), 0)),
            pl.BlockSpec((GLA_GATE_RANK, D_MODEL), lambda i, j: (0, 0)),
        ],
        out_specs=[
            pl.BlockSpec((tm, PROJ_TILE), lambda i, j: (i, j)),
            pl.BlockSpec((tm, 2 * KV_WIDTH), lambda i, j: (i, 0)),
            pl.BlockSpec((tm, LANES), lambda i, j: (i, 0)),
        ],
        out_shape=[
            jax.ShapeDtypeStruct((n, PROJ_WIDTH), BF16),
            jax.ShapeDtypeStruct((n, 2 * KV_WIDTH), BF16),
            jax.ShapeDtypeStruct((n, LANES), BF16),
        ],
        scratch_shapes=[pltpu.VMEM((tm, D_MODEL), BF16)],
        compiler_params=pltpu.CompilerParams(
            dimension_semantics=("arbitrary", "arbitrary"), vmem_limit_bytes=VMEM_LIMIT),
        name="inproj",
    )(x2, norm_w, w_t, w_t, w_lr_t)


PAIR = 2 * HEAD_DIM
PAIRS_PER_GROUP = GQA_GROUP // 2
STACK = PAIRS_PER_GROUP * WINDOW


def _attn_blocks(sink_ref, q_ref, kv_ref, ga_ref, o_ref, blocks):
    lo = lax.broadcasted_iota(jnp.int32, (1, PAIR), 1) < HEAD_DIM
    top = lax.broadcasted_iota(jnp.int32, (PAIR, 1), 0) < HEAD_DIM
    ki = lax.broadcasted_iota(jnp.int32, (WINDOW, STACK), 0)
    qi = lax.broadcasted_iota(jnp.int32, (WINDOW, STACK), 1) & (WINDOW - 1)
    upper = ki > qi

    def rows(blk):
        return pl.ds(blocks[blk][0], WINDOW)

    def krows(blk):
        _, k_row0, has_prev = blocks[blk]
        return pl.ds(k_row0, 2 * WINDOW if has_prev else WINDOW)

    def halves(pair_tile, g):
        swapped = pltpu.roll(pair_tile, HEAD_DIM, axis=1)
        own, other = (pair_tile, swapped) if g % 2 == 0 else (swapped, pair_tile)
        zero = jnp.zeros_like(pair_tile)
        return jnp.where(lo, own, zero), jnp.where(lo, zero, other)

    def probs(st, sink_row, has_prev):
        if has_prev:
            folded = jnp.where(upper, st[:WINDOW], st[WINDOW:])
        else:
            folded = jnp.where(upper, MASK_VALUE, st)
        m = jnp.maximum(jnp.max(folded, axis=0, keepdims=True), sink_row)
        p = jnp.exp(folded - m)
        inv = 1.0 / (jnp.sum(p, axis=0, keepdims=True) + jnp.exp(sink_row - m))
        if has_prev:
            p = jnp.concatenate([jnp.where(upper, p, 0.0), jnp.where(upper, 0.0, p)], axis=0)
        return p.astype(BF16), inv

    def scores(blk, g):
        k_lanes = slice((g // 2) * PAIR, (g // 2 + 1) * PAIR)
        k_lo, k_hi = halves(kv_ref[0, krows(blk), k_lanes], g)
        first_pair = g * PAIRS_PER_GROUP
        q = jnp.concatenate(
            [q_ref[0, rows(blk), (first_pair + p) * PAIR:(first_pair + p + 1) * PAIR]
             for p in range(PAIRS_PER_GROUP)], axis=0)
        return (lax.dot_general(k_lo, q, _NT, preferred_element_type=F32),
                lax.dot_general(k_hi, q, _NT, preferred_element_type=F32))

    def weighted_values(blk, g, st_even, st_odd):
        v_lanes = slice(KV_WIDTH + (g // 2) * PAIR, KV_WIDTH + (g // 2 + 1) * PAIR)
        v_lo, v_hi = halves(kv_ref[0, krows(blk), v_lanes], g)
        first_pair = g * PAIRS_PER_GROUP
        sink_even = jnp.concatenate(
            [jnp.full((1, WINDOW), sink_ref[2 * (first_pair + p)], F32) for p in range(PAIRS_PER_GROUP)], axis=1)
        sink_odd = jnp.concatenate(
            [jnp.full((1, WINDOW), sink_ref[2 * (first_pair + p) + 1], F32) for p in range(PAIRS_PER_GROUP)],
            axis=1)
        p_even, inv_even = probs(st_even, sink_even, blocks[blk][2])
        p_odd, inv_odd = probs(st_odd, sink_odd, blocks[blk][2])
        ot = (lax.dot_general(v_lo, p_even, _TN, preferred_element_type=F32)
              + lax.dot_general(v_hi, p_odd, _TN, preferred_element_type=F32))
        return ot, jnp.where(top, inv_even, inv_odd)

    def finish(blk, g, ot, inv):
        o = (ot * inv).T
        first_pair = g * PAIRS_PER_GROUP
        for p in range(PAIRS_PER_GROUP):
            cols = slice((first_pair + p) * PAIR, (first_pair + p + 1) * PAIR)
            gate = ga_ref[0, rows(blk), cols].astype(F32)
            o_ref[0, rows(blk), cols] = (gate * o[p * WINDOW:(p + 1) * WINDOW]).astype(BF16)

    items = [(blk, g) for blk in range(len(blocks)) for g in range(N_KV_HEADS)]
    st_next = scores(*items[0])
    unfinished = None
    for n, item in enumerate(items):
        st_even, st_odd = st_next
        if n + 1 < len(items):
            st_next = scores(*items[n + 1])
        if unfinished is not None:
            finish(*unfinished)
        unfinished = item + weighted_values(*item, st_even, st_odd)
    finish(*unfinished)


def _attn_kernel(sink_ref, q_ref, kv_ref, ga_ref, *rest, tq, n_cast):
    cast_in, cast_out, o_ref = rest[:n_cast], rest[n_cast:2 * n_cast], rest[2 * n_cast]
    for src, dst in zip(cast_in, cast_out):
        dst[...] = src[0].astype(BF16)

    t = pl.program_id(1)
    later = [(blk * WINDOW, pl.multiple_of(t * tq + (blk - 1) * WINDOW, WINDOW), True)
             for blk in range(1, tq // WINDOW)]

    @pl.when(t == 0)
    def _():
        _attn_blocks(sink_ref, q_ref, kv_ref, ga_ref, o_ref, [(0, 0, False)] + later)

    @pl.when(t > 0)
    def _():
        first = (0, pl.multiple_of(t * tq - WINDOW, WINDOW), True)
        _attn_blocks(sink_ref, q_ref, kv_ref, ga_ref, o_ref, [first] + later)


def _attention(proj3, kv3, sinks, layer, weights, *, tq):
    b, t, _ = proj3.shape
    steps = b * (t // tq)
    cast_in, cast_out, cast_shapes = [], [], []
    for w in weights:
        _, rows, cols = w.shape
        assert rows % (16 * steps) == 0, (rows, steps)
        cast_in.append(pl.BlockSpec((1, rows // steps, cols), lambda i, j: (layer, i * (t // tq) + j, 0)))
        cast_out.append(pl.BlockSpec((rows // steps, cols), lambda i, j: (i * (t // tq) + j, 0)))
        cast_shapes.append(jax.ShapeDtypeStruct((rows, cols), BF16))
    return pl.pallas_call(
        functools.partial(_attn_kernel, tq=tq, n_cast=len(weights)),
        grid=(b, t // tq),
        in_specs=[
            pl.BlockSpec(memory_space=pltpu.SMEM),
            pl.BlockSpec((1, tq, D_MODEL), lambda i, j: (i, j, COL_AQ // D_MODEL)),
            pl.BlockSpec((1, t, 2 * KV_WIDTH), lambda i, j: (i, 0, 0)),
            pl.BlockSpec((1, tq, D_MODEL), lambda i, j: (i, j, COL_GATE_A // D_MODEL)),
        ] + cast_in,
        out_specs=cast_out + [pl.BlockSpec((1, tq, D_MODEL), lambda i, j: (i, j, 0))],
        out_shape=cast_shapes + [jax.ShapeDtypeStruct((b, t, D_MODEL), BF16)],
        compiler_params=pltpu.CompilerParams(
            dimension_semantics=("arbitrary", "arbitrary"), vmem_limit_bytes=VMEM_LIMIT),
        name="swa_attention",
    )(sinks, proj3, kv3, proj3, *weights)


def _log_sigmoid(x):
    return jnp.minimum(x, 0.0) - jnp.log(1.0 + jnp.exp(-jnp.abs(x)))


GLA_SUPER = 2 * GLA_CHUNK


def _split_bf16(x):
    hi = x.astype(BF16)
    return hi, (x - hi.astype(F32)).astype(BF16)


def _gla_kernel(q_ref, k_ref, v_ref, gr_ref, gb_ref, lr_ref, w2_ref, b_ref, nw_ref, o_ref, s_ref, *, tb):
    @pl.when(pl.program_id(2) == 0)
    def _():
        s_ref[...] = jnp.zeros_like(s_ref)

    lr = lr_ref[0]
    logit = jnp.dot(lr, w2_ref[...], preferred_element_type=F32) + b_ref[...]
    log_a = _log_sigmoid(logit) * (1.0 / GLA_GATE_NORMALIZER)

    ri = lax.broadcasted_iota(jnp.int32, (GLA_SUPER, GLA_SUPER), 0)
    ci = lax.broadcasted_iota(jnp.int32, (GLA_SUPER, GLA_SUPER), 1)
    causal = ri >= ci
    cum = (causal & ((ri // GLA_CHUNK) == (ci // GLA_CHUNK))).astype(BF16)
    cum_twice = jnp.concatenate([cum, cum], axis=1)
    later_and_all = jnp.concatenate([(ri > ci).astype(BF16), jnp.ones((GLA_SUPER, GLA_SUPER), BF16)], axis=1)
    later_and_all_twice = jnp.concatenate([later_and_all, later_and_all], axis=0)
    no_keys = jnp.zeros((GLA_CHUNK, GLA_DK), BF16)

    def log_decay_sums(c):
        la = log_a[c * GLA_SUPER:(c + 1) * GLA_SUPER, :]
        g = jnp.dot(cum_twice, jnp.concatenate(_split_bf16(la), axis=0),
                    preferred_element_type=F32)
        tail = jnp.dot(jnp.concatenate(_split_bf16(la.T), axis=1), later_and_all_twice,
                       preferred_element_type=F32)
        return g, tail

    def decays(c, g, tail):
        rows = pl.ds(c * GLA_SUPER, GLA_SUPER)
        q = q_ref[0, rows, :].astype(F32)
        k = k_ref[0, rows, :].astype(F32)
        q_dec = q * jnp.exp(g)
        carry = jnp.exp(g[GLA_CHUNK - 1:GLA_CHUNK, :])
        q_carry = jnp.concatenate([q_dec[:GLA_CHUNK], q_dec[GLA_CHUNK:] * carry], axis=0).astype(BF16)
        k_inv = (k * jnp.exp(-g)).astype(BF16)
        k_inv_a = jnp.concatenate([k_inv[:GLA_CHUNK], no_keys], axis=0)
        k_inv_b = jnp.concatenate([no_keys, k_inv[GLA_CHUNK:]], axis=0)
        k_end_t = (k.T * jnp.exp(tail[:, :GLA_SUPER])).astype(BF16)
        decay = jnp.exp(tail[:, GLA_SUPER:])
        return q_carry, q_dec.astype(BF16), k_inv_a, k_inv_b, k_end_t, decay

    def intra(c, q_carry, q_dec, k_inv_a, k_inv_b, k_end_t, decay):
        v = v_ref[0, pl.ds(c * GLA_SUPER, GLA_SUPER), :]
        att = (lax.dot_general(q_carry, k_inv_a, _NT, preferred_element_type=F32)
               + lax.dot_general(q_dec, k_inv_b, _NT, preferred_element_type=F32))
        att = jnp.where(causal, att, 0.0).astype(BF16)
        o_intra = jnp.dot(att, v, preferred_element_type=F32)
        upd = jnp.dot(k_end_t, v, preferred_element_type=F32)
        return q_carry, o_intra, upd, decay

    def inter(c, q_carry, o_intra, upd, decay):
        rows = pl.ds(c * GLA_SUPER, GLA_SUPER)
        state = s_ref[...]
        o = o_intra + jnp.dot(q_carry, state.astype(BF16), preferred_element_type=F32)
        s_ref[...] = jnp.concatenate([decay] * (GLA_DV // GLA_SUPER), axis=1) * state + upd
        o = o * _rms_scale(o) * nw_ref[...]
        o = o * (gr_ref[0, rows, :] * gb_ref[0, rows, :]).astype(F32)
        o_ref[0, rows, :] = o.astype(BF16)

    n = tb // GLA_SUPER
    staged = {}
    for step in range(n + 3):
        if step < n:
            staged[step] = log_decay_sums(step)
        if 0 <= step - 1 < n:
            staged[step - 1] = decays(step - 1, *staged[step - 1])
        if 0 <= step - 2 < n:
            staged[step - 2] = intra(step - 2, *staged[step - 2])
        if 0 <= step - 3 < n:
            inter(step - 3, *staged.pop(step - 3))


def _gla(proj3, lr3, w2, bias, norm_w, *, tb):
    b, t, _ = proj3.shape
    return pl.pallas_call(
        functools.partial(_gla_kernel, tb=tb),
        grid=(b, GLA_HEADS, t // tb),
        in_specs=[
            pl.BlockSpec((1, tb, GLA_DK), lambda i, h, j: (i, j, COL_GQ // GLA_DK + h)),
            pl.BlockSpec((1, tb, GLA_DK), lambda i, h, j: (i, j, COL_GK // GLA_DK + h)),
            pl.BlockSpec((1, tb, GLA_DV), lambda i, h, j: (i, j, COL_GV // GLA_DV + h)),
            pl.BlockSpec((1, tb, GLA_DV), lambda i, h, j: (i, j, COL_GR // GLA_DV + h)),
            pl.BlockSpec((1, tb, GLA_DV), lambda i, h, j: (i, j, COL_GATE_B // GLA_DV + h)),
            pl.BlockSpec((1, tb, LANES), lambda i, h, j: (i, j, 0)),
            pl.BlockSpec((LANES, GLA_DK), lambda i, h, j: (0, h)),
            pl.BlockSpec((1, GLA_DK), lambda i, h, j: (0, h)),
            pl.BlockSpec((1, GLA_DV), lambda i, h, j: (0, 0)),
        ],
        out_specs=pl.BlockSpec((1, tb, GLA_DV), lambda i, h, j: (i, j, h)),
        out_shape=jax.ShapeDtypeStruct((b, t, D_MODEL), BF16),
        scratch_shapes=[pltpu.VMEM((GLA_DK, GLA_DV), F32)],
        compiler_params=pltpu.CompilerParams(
            dimension_semantics=("arbitrary", "arbitrary", "arbitrary"), vmem_limit_bytes=VMEM_LIMIT),
        name="gla",
    )(proj3, proj3, proj3, proj3, proj3, lr3, w2, bias, norm_w)


def _outproj_kernel(a_ref, g_ref, x_ref, w_ref, h_ref):
    merged = a_ref[...] + g_ref[...]
    h_ref[...] = x_ref[...] + jnp.dot(merged, w_ref[...], preferred_element_type=F32)


def _outproj(a2, g2, x2, w_out, *, tm):
    n = x2.shape[0]
    return pl.pallas_call(
        _outproj_kernel,
        grid=(n // tm,),
        in_specs=[
            pl.BlockSpec((tm, D_MODEL), lambda i: (i, 0)),
            pl.BlockSpec((tm, D_MODEL), lambda i: (i, 0)),
            pl.BlockSpec((tm, D_MODEL), lambda i: (i, 0)),
            pl.BlockSpec((D_MODEL, D_MODEL), lambda i: (0, 0)),
        ],
        out_specs=pl.BlockSpec((tm, D_MODEL), lambda i: (i, 0)),
        out_shape=jax.ShapeDtypeStruct((n, D_MODEL), F32),
        compiler_params=pltpu.CompilerParams(
            dimension_semantics=("arbitrary",), vmem_limit_bytes=VMEM_LIMIT),
        name="outproj",
    )(a2, g2, x2, w_out)


def _ffn_kernel(h_ref, nw_ref, wg_ref, wu_ref, wd_ref, fw_ref, o_ref, v_ref, *, final_norm):
    j = pl.program_id(1)
    last = pl.num_programs(1) - 1
    tm, th = h_ref.shape[0], wg_ref.shape[1]
    row_halves = [pl.ds(r * tm // 2, tm // 2) for r in range(2)]

    def hidden_tile(rows):
        v = v_ref[rows, :]
        part = None
        halves = [(c * th // 2, (c + 1) * th // 2) for c in range(2)]
        acts = []
        for lo, hi in halves:
            gate = jnp.dot(v, wg_ref[:, lo:hi], preferred_element_type=F32)
            up = jnp.dot(v, wu_ref[:, lo:hi], preferred_element_type=F32)
            acts.append((gate, up))
        for (lo, hi), (gate, up) in zip(halves, acts):
            act = (gate * _sigmoid(gate) * up).astype(BF16)
            contrib = jnp.dot(act, wd_ref[lo:hi, :], preferred_element_type=F32)
            part = contrib if part is None else part + contrib
        o_ref[rows, :] += part

    def prologue(rows):
        h = h_ref[rows, :]
        v_ref[rows, :] = (h * _rms_scale(h) * nw_ref[...]).astype(BF16)
        o_ref[rows, :] = h

    def epilogue(rows):
        y = o_ref[rows, :]
        o_ref[rows, :] = y * _rms_scale(y) * fw_ref[...]

    @pl.when(j == 0)
    def _():
        for rows in row_halves:
            prologue(rows)
        for rows in row_halves:
            hidden_tile(rows)

    @pl.when((j > 0) & (j < last))
    def _():
        hidden_tile(pl.ds(0, tm))

    @pl.when(j == last)
    def _():
        for rows in row_halves:
            hidden_tile(rows)
            if final_norm:
                epilogue(rows)


def _ffn(h2, norm_w, w_gate, w_up, w_down, final_w, *, tm, th, final_norm):
    n = h2.shape[0]
    assert FFN_HIDDEN // th >= 2
    return pl.pallas_call(
        functools.partial(_ffn_kernel, final_norm=final_norm),
        grid=(n // tm, FFN_HIDDEN // th),
        in_specs=[
            pl.BlockSpec((tm, D_MODEL), lambda i, j: (i, 0)),
            pl.BlockSpec((1, D_MODEL), lambda i, j: (0, 0)),
            pl.BlockSpec((D_MODEL, th), lambda i, j: (0, j)),
            pl.BlockSpec((D_MODEL, th), lambda i, j: (0, j)),
            pl.BlockSpec((th, D_MODEL), lambda i, j: (j, 0)),
            pl.BlockSpec((1, D_MODEL), lambda i, j: (0, 0)),
        ],
        out_specs=pl.BlockSpec((tm, D_MODEL), lambda i, j: (i, 0)),
        out_shape=jax.ShapeDtypeStruct((n, D_MODEL), F32),
        scratch_shapes=[pltpu.VMEM((tm, D_MODEL), BF16)],
        compiler_params=pltpu.CompilerParams(
            dimension_semantics=("arbitrary", "arbitrary"), vmem_limit_bytes=VMEM_LIMIT),
        name="ffn",
    )(h2, norm_w, w_gate, w_up, w_down, final_w)


_IN_WIDTHS = (D_MODEL, KV_WIDTH, KV_WIDTH, GLA_HEADS * GLA_DK, GLA_HEADS * GLA_DK, D_MODEL,
              GLA_GATE_RANK, D_MODEL, D_MODEL, D_MODEL)
(IN_AQ, IN_AK, IN_AV, IN_GQ, IN_GK, IN_GV, IN_LR, IN_GR, IN_GATE_A, IN_GATE_B, D_IN) = (
    sum(_IN_WIDTHS[:i]) for i in range(len(_IN_WIDTHS) + 1))


REGROUP_ROWS = 2 * KV_WIDTH
ATTN_Q_SCALE = HEAD_DIM ** -0.5
GLA_Q_SCALE = GLA_DK ** -0.5
assert ATTN_Q_SCALE == 2.0 ** -3 and GLA_Q_SCALE == 2.0 ** -4


def _regroup_source_row(i):
    row = i * REGROUP_ROWS
    src = row - COL_AQ + IN_AQ
    src = jnp.where(row >= COL_GV, row - COL_GV + IN_GV, src)
    src = jnp.where(row >= COL_GR, row - COL_GR + IN_GR, src)
    src = jnp.where(row >= COL_GQ, row - COL_GQ + IN_GQ, src)
    src = jnp.where(row >= COL_KV, row - COL_KV + IN_AK, src)
    return pl.multiple_of(src, GLA_GATE_RANK)


def _regroup_kernel(wt_ref, o_ref):
    row = pl.program_id(0) * REGROUP_ROWS
    scale = jnp.where(row < COL_AQ + D_MODEL, ATTN_Q_SCALE, 1.0)
    scale = jnp.where((row >= COL_GQ) & (row < COL_GK), GLA_Q_SCALE, scale)
    o_ref[...] = (wt_ref[0] * scale).astype(BF16)


def _regroup_w_in(w_in_t, layer):
    rows = COL_KV + 2 * KV_WIDTH
    return pl.pallas_call(
        _regroup_kernel,
        grid=(rows // REGROUP_ROWS,),
        in_specs=[pl.BlockSpec((pl.Element(1), pl.Element(REGROUP_ROWS), pl.Element(D_MODEL)),
                               lambda i: (layer, _regroup_source_row(i), 0))],
        out_specs=pl.BlockSpec((REGROUP_ROWS, D_MODEL), lambda i: (i, 0)),
        out_shape=jax.ShapeDtypeStruct((rows, D_MODEL), BF16),
        compiler_params=pltpu.CompilerParams(
            dimension_semantics=("arbitrary",), vmem_limit_bytes=VMEM_LIMIT),
        name="regroup_w_in",
    )(w_in_t)


def _tile(n, pref):
    return pref if n % pref == 0 else n


def kernel(x, norm1_w, w_in, gla_gate_w2, gla_gate_b, attn_sinks, gla_norm_w, w_out, norm2_w,
           w_ffn_gate, w_ffn_up, w_ffn_down, final_norm_w):
    b, t, d = x.shape
    n = b * t
    depth = w_in.shape[0]
    h2 = x.reshape(n, d)
    for l in range(depth):
        w_in_t = jnp.swapaxes(w_in, 1, 2)
        proj, kv, lr = _inproj(h2, norm1_w[l].reshape(1, d), _regroup_w_in(w_in_t, l),
                               w_in_t[l, IN_LR:IN_LR + GLA_GATE_RANK], tm=_tile(n, 1024))
        proj3 = proj.reshape(b, t, PROJ_WIDTH)

        wo, wg, wu, wd, a = _attention(proj3, kv.reshape(b, t, 2 * KV_WIDTH), attn_sinks[l], l,
                                       (w_out, w_ffn_gate, w_ffn_up, w_ffn_down), tq=_tile(t, 512))

        w2 = jnp.pad(gla_gate_w2[l], ((0, LANES - GLA_GATE_RANK), (0, 0))).astype(BF16)
        g = _gla(proj3, lr.reshape(b, t, LANES), w2, gla_gate_b[l].reshape(1, -1),
                 gla_norm_w[l].reshape(1, GLA_DV), tb=_tile(t, 2048))

        h2 = _outproj(a.reshape(n, d), g.reshape(n, d), h2, wo, tm=_tile(n, 512))
        h2 = _ffn(h2, norm2_w[l].reshape(1, d), wg, wu, wd, final_norm_w.reshape(1, d),
                  tm=_tile(n, 1024), th=512, final_norm=l == depth - 1)
    return h2.reshape(b, t, d)
```

```python
import functools

import jax
import jax.numpy as jnp
from jax import lax
from jax.experimental import pallas as pl
from jax.experimental.pallas import tpu as pltpu

F32 = jnp.float32
BF16 = jnp.bfloat16

D_MODEL = 2048
HEAD_DIM = 64
N_Q_HEADS = D_MODEL // HEAD_DIM
N_KV_HEADS = 4
GQA_GROUP = N_Q_HEADS // N_KV_HEADS
KV_WIDTH = N_KV_HEADS * HEAD_DIM
WINDOW = 128

GLA_HEADS = 4
GLA_DK = (D_MODEL // 2) // GLA_HEADS
GLA_DV = D_MODEL // GLA_HEADS
GLA_GATE_RANK = 16
GLA_GATE_NORMALIZER = 16.0
GLA_CHUNK = 64

FFN_HIDDEN = ((8 * D_MODEL // 3 + 255) // 256) * 256
RMS_EPS = 1e-6
MASK_VALUE = -1e30

LANES = 128

COL_AQ = 0
COL_GV = COL_AQ + D_MODEL
COL_GR = COL_GV + D_MODEL
COL_GATE_A = COL_GR + D_MODEL
COL_GATE_B = COL_GATE_A + D_MODEL
COL_GQ = COL_GATE_B + D_MODEL
COL_GK = COL_GQ + GLA_HEADS * GLA_DK
PROJ_WIDTH = COL_GK + GLA_HEADS * GLA_DK
PROJ_TILE = 2048
COL_KV = PROJ_WIDTH

VMEM_LIMIT = 56 * 1024 * 1024

_NT = (((1,), (1,)), ((), ()))
_TN = (((0,), (0,)), ((), ()))


def _rms_scale(x):
    return lax.rsqrt(jnp.mean(x * x, axis=-1, keepdims=True) + RMS_EPS)


def _sigmoid(x):
    return 0.5 * jnp.tanh(0.5 * x) + 0.5


def _inproj_kernel(x_ref, nw_ref, w_ref, wkv_ref, wlr_ref, proj_ref, kv_ref, lr_ref, u_ref):
    j = pl.program_id(1)

    @pl.when(j == 0)
    def _():
        x = x_ref[...]
        u = (x * _rms_scale(x) * nw_ref[...]).astype(BF16)
        u_ref[...] = u
        kv_ref[...] = lax.dot_general(u, wkv_ref[...], _NT, preferred_element_type=F32).astype(BF16)
        w_lr = wlr_ref[...].astype(BF16)
        w_lr = jnp.concatenate([w_lr, jnp.zeros((LANES - GLA_GATE_RANK, D_MODEL), BF16)], axis=0)
        lr_ref[...] = lax.dot_general(u, w_lr, _NT, preferred_element_type=F32).astype(BF16)

    def tile():
        return lax.dot_general(u_ref[...], w_ref[...], _NT, preferred_element_type=F32)

    is_silu = (j >= COL_GR // PROJ_TILE) & (j < COL_GATE_A // PROJ_TILE)
    is_sigmoid = (j >= COL_GATE_A // PROJ_TILE) & (j < COL_GQ // PROJ_TILE)

    @pl.when(is_silu)
    def _():
        y = tile()
        proj_ref[...] = (y * _sigmoid(y)).astype(BF16)

    @pl.when(is_sigmoid)
    def _():
        proj_ref[...] = _sigmoid(tile()).astype(BF16)

    @pl.when(jnp.logical_not(is_silu | is_sigmoid))
    def _():
        proj_ref[...] = tile().astype(BF16)


def _inproj(x2, norm_w, w_t, w_lr_t, *, tm):
    n = x2.shape[0]
    return pl.pallas_call(
        _inproj_kernel,
        grid=(n // tm, PROJ_WIDTH // PROJ_TILE),
        in_specs=[
            pl.BlockSpec((tm, D_MODEL), lambda i, j: (i, 0)),
            pl.BlockSpec((1, D_MODEL), lambda i, j: (0, 0)),
            pl.BlockSpec((PROJ_TILE, D_MODEL), lambda i, j: (j, 0)),
            pl.BlockSpec((2 * KV_WIDTH, D_MODEL), lambda i, j: (COL_KV // (2 * KV_WIDTH), 0)),
            pl.BlockSpec((GLA_GATE_RANK, D_MODEL), lambda i, j: (0, 0)),
        ],
        out_specs=[
            pl.BlockSpec((tm, PROJ_TILE), lambda i, j: (i, j)),
            pl.BlockSpec((tm, 2 * KV_WIDTH), lambda i, j: (i, 0)),
            pl.BlockSpec((tm, LANES), lambda i, j: (i, 0)),
        ],
        out_shape=[
            jax.ShapeDtypeStruct((n, PROJ_WIDTH), BF16),
            jax.ShapeDtypeStruct((n, 2 * KV_WIDTH), BF16),
            jax.ShapeDtypeStruct((n, LANES), BF16),
        ],
        scratch_shapes=[pltpu.VMEM((tm, D_MODEL), BF16)],
        compiler_params=pltpu.CompilerParams(
            dimension_semantics=("arbitrary", "arbitrary"), vmem_limit_bytes=VMEM_LIMIT),
        name="inproj",
    )(x2, norm_w, w_t, w_t, w_lr_t)


PAIR = 2 * HEAD_DIM
PAIRS_PER_GROUP = GQA_GROUP // 2
STACK = PAIRS_PER_GROUP * WINDOW


def _attn_blocks(sink_ref, q_ref, kv_ref, ga_ref, o_ref, blocks):
    lo = lax.broadcasted_iota(jnp.int32, (1, PAIR), 1) < HEAD_DIM
    top = lax.broadcasted_iota(jnp.int32, (PAIR, 1), 0) < HEAD_DIM
    ki = lax.broadcasted_iota(jnp.int32, (WINDOW, STACK), 0)
    qi = lax.broadcasted_iota(jnp.int32, (WINDOW, STACK), 1) & (WINDOW - 1)
    upper = ki > qi

    def rows(blk):
        return pl.ds(blocks[blk][0], WINDOW)

    def krows(blk):
        _, k_row0, has_prev = blocks[blk]
        return pl.ds(k_row0, 2 * WINDOW if has_prev else WINDOW)

    def halves(pair_tile, g):
        swapped = pltpu.roll(pair_tile, HEAD_DIM, axis=1)
        own, other = (pair_tile, swapped) if g % 2 == 0 else (swapped, pair_tile)
        zero = jnp.zeros_like(pair_tile)
        return jnp.where(lo, own, zero), jnp.where(lo, zero, other)

    def probs(st, sink_row, has_prev):
        if has_prev:
            folded = jnp.where(upper, st[:WINDOW], st[WINDOW:])
        else:
            folded = jnp.where(upper, MASK_VALUE, st)
        m = jnp.maximum(jnp.max(folded, axis=0, keepdims=True), sink_row)
        p = jnp.exp(folded - m)
        inv = 1.0 / (jnp.sum(p, axis=0, keepdims=True) + jnp.exp(sink_row - m))
        if has_prev:
            p = jnp.concatenate([jnp.where(upper, p, 0.0), jnp.where(upper, 0.0, p)], axis=0)
        return p.astype(BF16), inv

    def scores(blk, g):
        k_lanes = slice((g // 2) * PAIR, (g // 2 + 1) * PAIR)
        k_lo, k_hi = halves(kv_ref[0, krows(blk), k_lanes], g)
        first_pair = g * PAIRS_PER_GROUP
        q = jnp.concatenate(
            [q_ref[0, rows(blk), (first_pair + p) * PAIR:(first_pair + p + 1) * PAIR]
             for p in range(PAIRS_PER_GROUP)], axis=0)
        return (lax.dot_general(k_lo, q, _NT, preferred_element_type=F32),
                lax.dot_general(k_hi, q, _NT, preferred_element_type=F32))

    def weighted_values(blk, g, st_even, st_odd):
        v_lanes = slice(KV_WIDTH + (g // 2) * PAIR, KV_WIDTH + (g // 2 + 1) * PAIR)
        v_lo, v_hi = halves(kv_ref[0, krows(blk), v_lanes], g)
        first_pair = g * PAIRS_PER_GROUP
        sink_even = jnp.concatenate(
            [jnp.full((1, WINDOW), sink_ref[2 * (first_pair + p)], F32) for p in range(PAIRS_PER_GROUP)], axis=1)
        sink_odd = jnp.concatenate(
            [jnp.full((1, WINDOW), sink_ref[2 * (first_pair + p) + 1], F32) for p in range(PAIRS_PER_GROUP)],
            axis=1)
        p_even, inv_even = probs(st_even, sink_even, blocks[blk][2])
        p_odd, inv_odd = probs(st_odd, sink_odd, blocks[blk][2])
        ot = (lax.dot_general(v_lo, p_even, _TN, preferred_element_type=F32)
              + lax.dot_general(v_hi, p_odd, _TN, preferred_element_type=F32))
        return ot, jnp.where(top, inv_even, inv_odd)

    def finish(blk, g, ot, inv):
        o = (ot * inv).T
        first_pair = g * PAIRS_PER_GROUP
        for p in range(PAIRS_PER_GROUP):
            cols = slice((first_pair + p) * PAIR, (first_pair + p + 1) * PAIR)
            gate = ga_ref[0, rows(blk), cols].astype(F32)
            o_ref[0, rows(blk), cols] = (gate * o[p * WINDOW:(p + 1) * WINDOW]).astype(BF16)

    items = [(blk, g) for blk in range(len(blocks)) for g in range(N_KV_HEADS)]
    st_next = scores(*items[0])
    unfinished = None
    for n, item in enumerate(items):
        st_even, st_odd = st_next
        if n + 1 < len(items):
            st_next = scores(*items[n + 1])
        if unfinished is not None:
            finish(*unfinished)
        unfinished = item + weighted_values(*item, st_even, st_odd)
    finish(*unfinished)


def _attn_kernel(sink_ref, q_ref, kv_ref, ga_ref, *rest, tq, n_cast):
    cast_in, cast_out, o_ref = rest[:n_cast], rest[n_cast:2 * n_cast], rest[2 * n_cast]
    for src, dst in zip(cast_in, cast_out):
        dst[...] = src[0].astype(BF16)

    t = pl.program_id(1)
    later = [(blk * WINDOW, pl.multiple_of(t * tq + (blk - 1) * WINDOW, WINDOW), True)
             for blk in range(1, tq // WINDOW)]

    @pl.when(t == 0)
    def _():
        _attn_blocks(sink_ref, q_ref, kv_ref, ga_ref, o_ref, [(0, 0, False)] + later)

    @pl.when(t > 0)
    def _():
        first = (0, pl.multiple_of(t * tq - WINDOW, WINDOW), True)
        _attn_blocks(sink_ref, q_ref, kv_ref, ga_ref, o_ref, [first] + later)


def _attention(proj3, kv3, sinks, layer, weights, *, tq):
    b, t, _ = proj3.shape
    steps = b * (t // tq)
    cast_in, cast_out, cast_shapes = [], [], []
    for w in weights:
        _, rows, cols = w.shape
        assert rows % (16 * steps) == 0, (rows, steps)
        cast_in.append(pl.BlockSpec((1, rows // steps, cols), lambda i, j: (layer, i * (t // tq) + j, 0)))
        cast_out.append(pl.BlockSpec((rows // steps, cols), lambda i, j: (i * (t // tq) + j, 0)))
        cast_shapes.append(jax.ShapeDtypeStruct((rows, cols), BF16))
    return pl.pallas_call(
        functools.partial(_attn_kernel, tq=tq, n_cast=len(weights)),
        grid=(b, t // tq),
        in_specs=[
            pl.BlockSpec(memory_space=pltpu.SMEM),
            pl.BlockSpec((1, tq, D_MODEL), lambda i, j: (i, j, COL_AQ // D_MODEL)),
            pl.BlockSpec((1, t, 2 * KV_WIDTH), lambda i, j: (i, 0, 0)),
            pl.BlockSpec((1, tq, D_MODEL), lambda i, j: (i, j, COL_GATE_A // D_MODEL)),
        ] + cast_in,
        out_specs=cast_out + [pl.BlockSpec((1, tq, D_MODEL), lambda i, j: (i, j, 0))],
        out_shape=cast_shapes + [jax.ShapeDtypeStruct((b, t, D_MODEL), BF16)],
        compiler_params=pltpu.CompilerParams(
            dimension_semantics=("arbitrary", "arbitrary"), vmem_limit_bytes=VMEM_LIMIT),
        name="swa_attention",
    )(sinks, proj3, kv3, proj3, *weights)


def _log_sigmoid(x):
    return jnp.minimum(x, 0.0) - jnp.log(1.0 + jnp.exp(-jnp.abs(x)))


GLA_SUPER = 2 * GLA_CHUNK


def _split_bf16(x):
    hi = x.astype(BF16)
    return hi, (x - hi.astype(F32)).astype(BF16)


def _gla_kernel(q_ref, k_ref, v_ref, gr_ref, gb_ref, lr_ref, w2_ref, b_ref, nw_ref, o_ref, s_ref, *, tb):
    @pl.when(pl.program_id(2) == 0)
    def _():
        s_ref[...] = jnp.zeros_like(s_ref)

    lr = lr_ref[0]
    logit = jnp.dot(lr, w2_ref[...], preferred_element_type=F32) + b_ref[...]
    log_a = _log_sigmoid(logit) * (1.0 / GLA_GATE_NORMALIZER)

    ri = lax.broadcasted_iota(jnp.int32, (GLA_SUPER, GLA_SUPER), 0)
    ci = lax.broadcasted_iota(jnp.int32, (GLA_SUPER, GLA_SUPER), 1)
    causal = ri >= ci
    cum = (causal & ((ri // GLA_CHUNK) == (ci // GLA_CHUNK))).astype(BF16)
    cum_twice = jnp.concatenate([cum, cum], axis=1)
    later_and_all = jnp.concatenate([(ri > ci).astype(BF16), jnp.ones((GLA_SUPER, GLA_SUPER), BF16)], axis=1)
    later_and_all_twice = jnp.concatenate([later_and_all, later_and_all], axis=0)
    no_keys = jnp.zeros((GLA_CHUNK, GLA_DK), BF16)

    def log_decay_sums(c):
        la = log_a[c * GLA_SUPER:(c + 1) * GLA_SUPER, :]
        g = jnp.dot(cum_twice, jnp.concatenate(_split_bf16(la), axis=0),
                    preferred_element_type=F32)
        tail = jnp.dot(jnp.concatenate(_split_bf16(la.T), axis=1), later_and_all_twice,
                       preferred_element_type=F32)
        return g, tail

    def decays(c, g, tail):
        rows = pl.ds(c * GLA_SUPER, GLA_SUPER)
        q = q_ref[0, rows, :].astype(F32)
        k = k_ref[0, rows, :].astype(F32)
        q_dec = q * jnp.exp(g)
        carry = jnp.exp(g[GLA_CHUNK - 1:GLA_CHUNK, :])
        q_carry = jnp.concatenate([q_dec[:GLA_CHUNK], q_dec[GLA_CHUNK:] * carry], axis=0).astype(BF16)
        k_inv = (k * jnp.exp(-g)).astype(BF16)
        k_inv_a = jnp.concatenate([k_inv[:GLA_CHUNK], no_keys], axis=0)
        k_inv_b = jnp.concatenate([no_keys, k_inv[GLA_CHUNK:]], axis=0)
        k_end_t = (k.T * jnp.exp(tail[:, :GLA_SUPER])).astype(BF16)
        decay = jnp.exp(tail[:, GLA_SUPER:])
        return q_carry, q_dec.astype(BF16), k_inv_a, k_inv_b, k_end_t, decay

    def intra(c, q_carry, q_dec, k_inv_a, k_inv_b, k_end_t, decay):
        v = v_ref[0, pl.ds(c * GLA_SUPER, GLA_SUPER), :]
        att = (lax.dot_general(q_carry, k_inv_a, _NT, preferred_element_type=F32)
               + lax.dot_general(q_dec, k_inv_b, _NT, preferred_element_type=F32))
        att = jnp.where(causal, att, 0.0).astype(BF16)
        o_intra = jnp.dot(att, v, preferred_element_type=F32)
        upd = jnp.dot(k_end_t, v, preferred_element_type=F32)
        return q_carry, o_intra, upd, decay

    def inter(c, q_carry, o_intra, upd, decay):
        rows = pl.ds(c * GLA_SUPER, GLA_SUPER)
        state = s_ref[...]
        o = o_intra + jnp.dot(q_carry, state.astype(BF16), preferred_element_type=F32)
        s_ref[...] = jnp.concatenate([decay] * (GLA_DV // GLA_SUPER), axis=1) * state + upd
        o = o * _rms_scale(o) * nw_ref[...]
        o = o * (gr_ref[0, rows, :] * gb_ref[0, rows, :]).astype(F32)
        o_ref[0, rows, :] = o.astype(BF16)

    n = tb // GLA_SUPER
    staged = {}
    for step in range(n + 3):
        if step < n:
            staged[step] = log_decay_sums(step)
        if 0 <= step - 1 < n:
            staged[step - 1] = decays(step - 1, *staged[step - 1])
        if 0 <= step - 2 < n:
            staged[step - 2] = intra(step - 2, *staged[step - 2])
        if 0 <= step - 3 < n:
            inter(step - 3, *staged.pop(step - 3))


def _gla(proj3, lr3, w2, bias, norm_w, *, tb):
    b, t, _ = proj3.shape
    return pl.pallas_call(
        functools.partial(_gla_kernel, tb=tb),
        grid=(b, GLA_HEADS, t // tb),
        in_specs=[
            pl.BlockSpec((1, tb, GLA_DK), lambda i, h, j: (i, j, COL_GQ // GLA_DK + h)),
            pl.BlockSpec((1, tb, GLA_DK), lambda i, h, j: (i, j, COL_GK // GLA_DK + h)),
            pl.BlockSpec((1, tb, GLA_DV), lambda i, h, j: (i, j, COL_GV // GLA_DV + h)),
            pl.BlockSpec((1, tb, GLA_DV), lambda i, h, j: (i, j, COL_GR // GLA_DV + h)),
            pl.BlockSpec((1, tb, GLA_DV), lambda i, h, j: (i, j, COL_GATE_B // GLA_DV + h)),
            pl.BlockSpec((1, tb, LANES), lambda i, h, j: (i, j, 0)),
            pl.BlockSpec((LANES, GLA_DK), lambda i, h, j: (0, h)),
            pl.BlockSpec((1, GLA_DK), lambda i, h, j: (0, h)),
            pl.BlockSpec((1, GLA_DV), lambda i, h, j: (0, 0)),
        ],
        out_specs=pl.BlockSpec((1, tb, GLA_DV), lambda i, h, j: (i, j, h)),
        out_shape=jax.ShapeDtypeStruct((b, t, D_MODEL), BF16),
        scratch_shapes=[pltpu.VMEM((GLA_DK, GLA_DV), F32)],
        compiler_params=pltpu.CompilerParams(
            dimension_semantics=("arbitrary", "arbitrary", "arbitrary"), vmem_limit_bytes=VMEM_LIMIT),
        name="gla",
    )(proj3, proj3, proj3, proj3, proj3, lr3, w2, bias, norm_w)


def _outproj_kernel(a_ref, g_ref, x_ref, w_ref, h_ref):
    merged = a_ref[...] + g_ref[...]
    h_ref[...] = x_ref[...] + jnp.dot(merged, w_ref[...], preferred_element_type=F32)


def _outproj(a2, g2, x2, w_out, *, tm):
    n = x2.shape[0]
    return pl.pallas_call(
        _outproj_kernel,
        grid=(n // tm,),
        in_specs=[
            pl.BlockSpec((tm, D_MODEL), lambda i: (i, 0)),
            pl.BlockSpec((tm, D_MODEL), lambda i: (i, 0)),
            pl.BlockSpec((tm, D_MODEL), lambda i: (i, 0)),
            pl.BlockSpec((D_MODEL, D_MODEL), lambda i: (0, 0)),
        ],
        out_specs=pl.BlockSpec((tm, D_MODEL), lambda i: (i, 0)),
        out_shape=jax.ShapeDtypeStruct((n, D_MODEL), F32),
        compiler_params=pltpu.CompilerParams(
            dimension_semantics=("arbitrary",), vmem_limit_bytes=VMEM_LIMIT),
        name="outproj",
    )(a2, g2, x2, w_out)


def _ffn_kernel(h_ref, nw_ref, wg_ref, wu_ref, wd_ref, fw_ref, o_ref, v_ref, *, final_norm):
    j = pl.program_id(1)

    @pl.when(j == 0)
    def _():
        h = h_ref[...]
        v_ref[...] = (h * _rms_scale(h) * nw_ref[...]).astype(BF16)
        o_ref[...] = h

    v = v_ref[...]
    th = wg_ref.shape[1]
    part = None
    halves = [(c * th // 2, (c + 1) * th // 2) for c in range(2)]
    acts = []
    for lo, hi in halves:
        gate = jnp.dot(v, wg_ref[:, lo:hi], preferred_element_type=F32)
        up = jnp.dot(v, wu_ref[:, lo:hi], preferred_element_type=F32)
        acts.append((gate, up))
    for (lo, hi), (gate, up) in zip(halves, acts):
        act = (gate * _sigmoid(gate) * up).astype(BF16)
        contrib = jnp.dot(act, wd_ref[lo:hi, :], preferred_element_type=F32)
        part = contrib if part is None else part + contrib
    o_ref[...] += part

    if final_norm:
        @pl.when(j == pl.num_programs(1) - 1)
        def _():
            y = o_ref[...]
            o_ref[...] = y * _rms_scale(y) * fw_ref[...]


def _ffn(h2, norm_w, w_gate, w_up, w_down, final_w, *, tm, th, final_norm):
    n = h2.shape[0]
    return pl.pallas_call(
        functools.partial(_ffn_kernel, final_norm=final_norm),
        grid=(n // tm, FFN_HIDDEN // th),
        in_specs=[
            pl.BlockSpec((tm, D_MODEL), lambda i, j: (i, 0)),
            pl.BlockSpec((1, D_MODEL), lambda i, j: (0, 0)),
            pl.BlockSpec((D_MODEL, th), lambda i, j: (0, j)),
            pl.BlockSpec((D_MODEL, th), lambda i, j: (0, j)),
            pl.BlockSpec((th, D_MODEL), lambda i, j: (j, 0)),
            pl.BlockSpec((1, D_MODEL), lambda i, j: (0, 0)),
        ],
        out_specs=pl.BlockSpec((tm, D_MODEL), lambda i, j: (i, 0)),
        out_shape=jax.ShapeDtypeStruct((n, D_MODEL), F32),
        scratch_shapes=[pltpu.VMEM((tm, D_MODEL), BF16)],
        compiler_params=pltpu.CompilerParams(
            dimension_semantics=("arbitrary", "arbitrary"), vmem_limit_bytes=VMEM_LIMIT),
        name="ffn",
    )(h2, norm_w, w_gate, w_up, w_down, final_w)


_IN_WIDTHS = (D_MODEL, KV_WIDTH, KV_WIDTH, GLA_HEADS * GLA_DK, GLA_HEADS * GLA_DK, D_MODEL,
              GLA_GATE_RANK, D_MODEL, D_MODEL, D_MODEL)
(IN_AQ, IN_AK, IN_AV, IN_GQ, IN_GK, IN_GV, IN_LR, IN_GR, IN_GATE_A, IN_GATE_B, D_IN) = (
    sum(_IN_WIDTHS[:i]) for i in range(len(_IN_WIDTHS) + 1))


REGROUP_ROWS = 2 * KV_WIDTH
ATTN_Q_SCALE = HEAD_DIM ** -0.5
GLA_Q_SCALE = GLA_DK ** -0.5
assert ATTN_Q_SCALE == 2.0 ** -3 and GLA_Q_SCALE == 2.0 ** -4


def _regroup_source_row(i):
    row = i * REGROUP_ROWS
    src = row - COL_AQ + IN_AQ
    src = jnp.where(row >= COL_GV, row - COL_GV + IN_GV, src)
    src = jnp.where(row >= COL_GR, row - COL_GR + IN_GR, src)
    src = jnp.where(row >= COL_GQ, row - COL_GQ + IN_GQ, src)
    src = jnp.where(row >= COL_KV, row - COL_KV + IN_AK, src)
    return pl.multiple_of(src, GLA_GATE_RANK)


def _regroup_kernel(wt_ref, o_ref):
    row = pl.program_id(0) * REGROUP_ROWS
    scale = jnp.where(row < COL_AQ + D_MODEL, ATTN_Q_SCALE, 1.0)
    scale = jnp.where((row >= COL_GQ) & (row < COL_GK), GLA_Q_SCALE, scale)
    o_ref[...] = (wt_ref[0] * scale).astype(BF16)


def _regroup_w_in(w_in_t, layer):
    rows = COL_KV + 2 * KV_WIDTH
    return pl.pallas_call(
        _regroup_kernel,
        grid=(rows // REGROUP_ROWS,),
        in_specs=[pl.BlockSpec((pl.Element(1), pl.Element(REGROUP_ROWS), pl.Element(D_MODEL)),
                               lambda i: (layer, _regroup_source_row(i), 0))],
        out_specs=pl.BlockSpec((REGROUP_ROWS, D_MODEL), lambda i: (i, 0)),
        out_shape=jax.ShapeDtypeStruct((rows, D_MODEL), BF16),
        compiler_params=pltpu.CompilerParams(
            dimension_semantics=("arbitrary",), vmem_limit_bytes=VMEM_LIMIT),
        name="regroup_w_in",
    )(w_in_t)


def _tile(n, pref):
    return pref if n % pref == 0 else n


def kernel(x, norm1_w, w_in, gla_gate_w2, gla_gate_b, attn_sinks, gla_norm_w, w_out, norm2_w,
           w_ffn_gate, w_ffn_up, w_ffn_down, final_norm_w):
    b, t, d = x.shape
    n = b * t
    depth = w_in.shape[0]
    h2 = x.reshape(n, d)
    for l in range(depth):
        w_in_t = jnp.swapaxes(w_in, 1, 2)
        proj, kv, lr = _inproj(h2, norm1_w[l].reshape(1, d), _regroup_w_in(w_in_t, l),
                               w_in_t[l, IN_LR:IN_LR + GLA_GATE_RANK], tm=_tile(n, 1024))
        proj3 = proj.reshape(b, t, PROJ_WIDTH)

        wo, wg, wu, wd, a = _attention(proj3, kv.reshape(b, t, 2 * KV_WIDTH), attn_sinks[l], l,
                                       (w_out, w_ffn_gate, w_ffn_up, w_ffn_down), tq=_tile(t, 512))

        w2 = jnp.pad(gla_gate_w2[l], ((0, LANES - GLA_GATE_RANK), (0, 0))).astype(BF16)
        g = _gla(proj3, lr.reshape(b, t, LANES), w2, gla_gate_b[l].reshape(1, -1),
                 gla_norm_w[l].reshape(1, GLA_DV), tb=_tile(t, 2048))

        h2 = _outproj(a.reshape(n, d), g.reshape(n, d), h2, wo, tm=_tile(n, 512))
        h2 = _ffn(h2, norm2_w[l].reshape(1, d), wg, wu, wd, final_norm_w.reshape(1, d),
                  tm=_tile(n, 1024), th=512, final_norm=l == depth - 1)
    return h2.reshape(b, t, d)
```

```python
import functools

import jax
import jax.numpy as jnp
from jax import lax
from jax.experimental import pallas as pl
from jax.experimental.pallas import tpu as pltpu

F32 = jnp.float32
BF16 = jnp.bfloat16

D_MODEL = 2048
HEAD_DIM = 64
N_Q_HEADS = D_MODEL // HEAD_DIM
N_KV_HEADS = 4
GQA_GROUP = N_Q_HEADS // N_KV_HEADS
KV_WIDTH = N_KV_HEADS * HEAD_DIM
WINDOW = 128

GLA_HEADS = 4
GLA_DK = (D_MODEL // 2) // GLA_HEADS
GLA_DV = D_MODEL // GLA_HEADS
GLA_GATE_RANK = 16
GLA_GATE_NORMALIZER = 16.0
GLA_CHUNK = 64

FFN_HIDDEN = ((8 * D_MODEL // 3 + 255) // 256) * 256
RMS_EPS = 1e-6
MASK_VALUE = -1e30

LANES = 128

COL_AQ = 0
COL_GV = COL_AQ + D_MODEL
COL_GR = COL_GV + D_MODEL
COL_GATE_A = COL_GR + D_MODEL
COL_GATE_B = COL_GATE_A + D_MODEL
COL_GQ = COL_GATE_B + D_MODEL
COL_GK = COL_GQ + GLA_HEADS * GLA_DK
PROJ_WIDTH = COL_GK + GLA_HEADS * GLA_DK
PROJ_TILE = 2048
COL_KV = PROJ_WIDTH

VMEM_LIMIT = 56 * 1024 * 1024

_NT = (((1,), (1,)), ((), ()))
_TN = (((0,), (0,)), ((), ()))


def _rms_scale(x):
    return lax.rsqrt(jnp.mean(x * x, axis=-1, keepdims=True) + RMS_EPS)


def _sigmoid(x):
    return 0.5 * jnp.tanh(0.5 * x) + 0.5


def _inproj_kernel(x_ref, nw_ref, w_ref, wkv_ref, wlr_ref, proj_ref, kv_ref, lr_ref, u_ref):
    j = pl.program_id(1)

    @pl.when(j == 0)
    def _():
        x = x_ref[...]
        u = (x * _rms_scale(x) * nw_ref[...]).astype(BF16)
        u_ref[...] = u
        kv_ref[...] = lax.dot_general(u, wkv_ref[...], _NT, preferred_element_type=F32).astype(BF16)
        w_lr = wlr_ref[...].astype(BF16)
        w_lr = jnp.concatenate([w_lr, jnp.zeros((LANES - GLA_GATE_RANK, D_MODEL), BF16)], axis=0)
        lr_ref[...] = lax.dot_general(u, w_lr, _NT, preferred_element_type=F32).astype(BF16)

    def tile():
        return lax.dot_general(u_ref[...], w_ref[...], _NT, preferred_element_type=F32)

    is_silu = (j >= COL_GR // PROJ_TILE) & (j < COL_GATE_A // PROJ_TILE)
    is_sigmoid = (j >= COL_GATE_A // PROJ_TILE) & (j < COL_GQ // PROJ_TILE)

    @pl.when(is_silu)
    def _():
        y = tile()
        proj_ref[...] = (y * _sigmoid(y)).astype(BF16)

    @pl.when(is_sigmoid)
    def _():
        proj_ref[...] = _sigmoid(tile()).astype(BF16)

    @pl.when(jnp.logical_not(is_silu | is_sigmoid))
    def _():
        proj_ref[...] = tile().astype(BF16)


def _inproj(x2, norm_w, w_t, w_lr_t, *, tm):
    n = x2.shape[0]
    return pl.pallas_call(
        _inproj_kernel,
        grid=(n // tm, PROJ_WIDTH // PROJ_TILE),
        in_specs=[
            pl.BlockSpec((tm, D_MODEL), lambda i, j: (i, 0)),
            pl.BlockSpec((1, D_MODEL), lambda i, j: (0, 0)),
            pl.BlockSpec((PROJ_TILE, D_MODEL), lambda i, j: (j, 0)),
            pl.BlockSpec((2 * KV_WIDTH, D_MODEL), lambda i, j: (COL_KV // (2 * KV_WIDTH), 0)),
            pl.BlockSpec((GLA_GATE_RANK, D_MODEL), lambda i, j: (0, 0)),
        ],
        out_specs=[
            pl.BlockSpec((tm, PROJ_TILE), lambda i, j: (i, j)),
            pl.BlockSpec((tm, 2 * KV_WIDTH), lambda i, j: (i, 0)),
            pl.BlockSpec((tm, LANES), lambda i, j: (i, 0)),
        ],
        out_shape=[
            jax.ShapeDtypeStruct((n, PROJ_WIDTH), BF16),
            jax.ShapeDtypeStruct((n, 2 * KV_WIDTH), BF16),
            jax.ShapeDtypeStruct((n, LANES), BF16),
        ],
        scratch_shapes=[pltpu.VMEM((tm, D_MODEL), BF16)],
        compiler_params=pltpu.CompilerParams(
            dimension_semantics=("arbitrary", "arbitrary"), vmem_limit_bytes=VMEM_LIMIT),
        name="inproj",
    )(x2, norm_w, w_t, w_t, w_lr_t)


PAIR = 2 * HEAD_DIM
PAIRS_PER_GROUP = GQA_GROUP // 2
STACK = PAIRS_PER_GROUP * WINDOW


def _attn_blocks(sink_ref, q_ref, kv_ref, ga_ref, o_ref, blocks):
    lo = lax.broadcasted_iota(jnp.int32, (1, PAIR), 1) < HEAD_DIM
    top = lax.broadcasted_iota(jnp.int32, (PAIR, 1), 0) < HEAD_DIM
    ki = lax.broadcasted_iota(jnp.int32, (WINDOW, STACK), 0)
    qi = lax.broadcasted_iota(jnp.int32, (WINDOW, STACK), 1) & (WINDOW - 1)
    upper = ki > qi

    def rows(blk):
        return pl.ds(blocks[blk][0], WINDOW)

    def krows(blk):
        _, k_row0, has_prev = blocks[blk]
        return pl.ds(k_row0, 2 * WINDOW if has_prev else WINDOW)

    def halves(pair_tile, g):
        swapped = pltpu.roll(pair_tile, HEAD_DIM, axis=1)
        own, other = (pair_tile, swapped) if g % 2 == 0 else (swapped, pair_tile)
        zero = jnp.zeros_like(pair_tile)
        return jnp.where(lo, own, zero), jnp.where(lo, zero, other)

    def probs(st, sink_row, has_prev):
        if has_prev:
            folded = jnp.where(upper, st[:WINDOW], st[WINDOW:])
        else:
            folded = jnp.where(upper, MASK_VALUE, st)
        m = jnp.maximum(jnp.max(folded, axis=0, keepdims=True), sink_row)
        p = jnp.exp(folded - m)
        inv = 1.0 / (jnp.sum(p, axis=0, keepdims=True) + jnp.exp(sink_row - m))
        if has_prev:
            p = jnp.concatenate([jnp.where(upper, p, 0.0), jnp.where(upper, 0.0, p)], axis=0)
        return p.astype(BF16), inv

    def scores(blk, g):
        k_lanes = slice((g // 2) * PAIR, (g // 2 + 1) * PAIR)
        k_lo, k_hi = halves(kv_ref[0, krows(blk), k_lanes], g)
        first_pair = g * PAIRS_PER_GROUP
        q = jnp.concatenate(
            [q_ref[0, rows(blk), (first_pair + p) * PAIR:(first_pair + p + 1) * PAIR]
             for p in range(PAIRS_PER_GROUP)], axis=0)
        return (lax.dot_general(k_lo, q, _NT, preferred_element_type=F32),
                lax.dot_general(k_hi, q, _NT, preferred_element_type=F32))

    def weighted_values(blk, g, st_even, st_odd):
        v_lanes = slice(KV_WIDTH + (g // 2) * PAIR, KV_WIDTH + (g // 2 + 1) * PAIR)
        v_lo, v_hi = halves(kv_ref[0, krows(blk), v_lanes], g)
        first_pair = g * PAIRS_PER_GROUP
        sink_even = jnp.concatenate(
            [jnp.full((1, WINDOW), sink_ref[2 * (first_pair + p)], F32) for p in range(PAIRS_PER_GROUP)], axis=1)
        sink_odd = jnp.concatenate(
            [jnp.full((1, WINDOW), sink_ref[2 * (first_pair + p) + 1], F32) for p in range(PAIRS_PER_GROUP)],
            axis=1)
        p_even, inv_even = probs(st_even, sink_even, blocks[blk][2])
        p_odd, inv_odd = probs(st_odd, sink_odd, blocks[blk][2])
        ot = (lax.dot_general(v_lo, p_even, _TN, preferred_element_type=F32)
              + lax.dot_general(v_hi, p_odd, _TN, preferred_element_type=F32))
        return ot, jnp.where(top, inv_even, inv_odd)

    def finish(blk, g, ot, inv):
        o = (ot * inv).T
        first_pair = g * PAIRS_PER_GROUP
        for p in range(PAIRS_PER_GROUP):
            cols = slice((first_pair + p) * PAIR, (first_pair + p + 1) * PAIR)
            gate = ga_ref[0, rows(blk), cols].astype(F32)
            o_ref[0, rows(blk), cols] = (gate * o[p * WINDOW:(p + 1) * WINDOW]).astype(BF16)

    items = [(blk, g) for blk in range(len(blocks)) for g in range(N_KV_HEADS)]
    st_next = scores(*items[0])
    unfinished = None
    for n, item in enumerate(items):
        st_even, st_odd = st_next
        if n + 1 < len(items):
            st_next = scores(*items[n + 1])
        if unfinished is not None:
            finish(*unfinished)
        unfinished = item + weighted_values(*item, st_even, st_odd)
    finish(*unfinished)


def _attn_kernel(sink_ref, q_ref, kv_ref, ga_ref, *rest, tq, n_cast):
    cast_in, cast_out, o_ref = rest[:n_cast], rest[n_cast:2 * n_cast], rest[2 * n_cast]
    for src, dst in zip(cast_in, cast_out):
        dst[...] = src[0].astype(BF16)

    t = pl.program_id(1)
    later = [(blk * WINDOW, pl.multiple_of(t * tq + (blk - 1) * WINDOW, WINDOW), True)
             for blk in range(1, tq // WINDOW)]

    @pl.when(t == 0)
    def _():
        _attn_blocks(sink_ref, q_ref, kv_ref, ga_ref, o_ref, [(0, 0, False)] + later)

    @pl.when(t > 0)
    def _():
        first = (0, pl.multiple_of(t * tq - WINDOW, WINDOW), True)
        _attn_blocks(sink_ref, q_ref, kv_ref, ga_ref, o_ref, [first] + later)


def _attention(proj3, kv3, sinks, layer, weights, *, tq):
    b, t, _ = proj3.shape
    steps = b * (t // tq)
    cast_in, cast_out, cast_shapes = [], [], []
    for w in weights:
        _, rows, cols = w.shape
        assert rows % (16 * steps) == 0, (rows, steps)
        cast_in.append(pl.BlockSpec((1, rows // steps, cols), lambda i, j: (layer, i * (t // tq) + j, 0)))
        cast_out.append(pl.BlockSpec((rows // steps, cols), lambda i, j: (i * (t // tq) + j, 0)))
        cast_shapes.append(jax.ShapeDtypeStruct((rows, cols), BF16))
    return pl.pallas_call(
        functools.partial(_attn_kernel, tq=tq, n_cast=len(weights)),
        grid=(b, t // tq),
        in_specs=[
            pl.BlockSpec(memory_space=pltpu.SMEM),
            pl.BlockSpec((1, tq, D_MODEL), lambda i, j: (i, j, COL_AQ // D_MODEL)),
            pl.BlockSpec((1, t, 2 * KV_WIDTH), lambda i, j: (i, 0, 0)),
            pl.BlockSpec((1, tq, D_MODEL), lambda i, j: (i, j, COL_GATE_A // D_MODEL)),
        ] + cast_in,
        out_specs=cast_out + [pl.BlockSpec((1, tq, D_MODEL), lambda i, j: (i, j, 0))],
        out_shape=cast_shapes + [jax.ShapeDtypeStruct((b, t, D_MODEL), BF16)],
        compiler_params=pltpu.CompilerParams(
            dimension_semantics=("arbitrary", "arbitrary"), vmem_limit_bytes=VMEM_LIMIT),
        name="swa_attention",
    )(sinks, proj3, kv3, proj3, *weights)


def _log_sigmoid(x):
    return jnp.minimum(x, 0.0) - jnp.log(1.0 + jnp.exp(-jnp.abs(x)))


GLA_SUPER = 2 * GLA_CHUNK


def _split_bf16(x):
    hi = x.astype(BF16)
    return hi, (x - hi.astype(F32)).astype(BF16)


def _gla_kernel(q_ref, k_ref, v_ref, gr_ref, gb_ref, lr_ref, w2_ref, b_ref, nw_ref, *rest, tb, n_cast):
    cast_in, cast_out = rest[:n_cast], rest[n_cast:2 * n_cast]
    o_ref, s_ref = rest[2 * n_cast:]
    for src, dst in zip(cast_in, cast_out):
        dst[...] = src[0].astype(BF16)

    @pl.when(pl.program_id(2) == 0)
    def _():
        s_ref[...] = jnp.zeros_like(s_ref)

    lr = lr_ref[0]
    logit = jnp.dot(lr, w2_ref[...], preferred_element_type=F32) + b_ref[...]
    log_a = _log_sigmoid(logit) * (1.0 / GLA_GATE_NORMALIZER)

    ri = lax.broadcasted_iota(jnp.int32, (GLA_SUPER, GLA_SUPER), 0)
    ci = lax.broadcasted_iota(jnp.int32, (GLA_SUPER, GLA_SUPER), 1)
    causal = ri >= ci
    cum = (causal & ((ri // GLA_CHUNK) == (ci // GLA_CHUNK))).astype(BF16)
    cum_twice = jnp.concatenate([cum, cum], axis=1)
    later_and_all = jnp.concatenate([(ri > ci).astype(BF16), jnp.ones((GLA_SUPER, GLA_SUPER), BF16)], axis=1)
    later_and_all_twice = jnp.concatenate([later_and_all, later_and_all], axis=0)
    no_keys = jnp.zeros((GLA_CHUNK, GLA_DK), BF16)

    def log_decay_sums(c):
        la = log_a[c * GLA_SUPER:(c + 1) * GLA_SUPER, :]
        g = jnp.dot(cum_twice, jnp.concatenate(_split_bf16(la), axis=0),
                    preferred_element_type=F32)
        tail = jnp.dot(jnp.concatenate(_split_bf16(la.T), axis=1), later_and_all_twice,
                       preferred_element_type=F32)
        return g, tail

    def decays(c, g, tail):
        rows = pl.ds(c * GLA_SUPER, GLA_SUPER)
        q = q_ref[0, rows, :].astype(F32)
        k = k_ref[0, rows, :].astype(F32)
        q_dec = q * jnp.exp(g)
        carry = jnp.exp(g[GLA_CHUNK - 1:GLA_CHUNK, :])
        q_carry = jnp.concatenate([q_dec[:GLA_CHUNK], q_dec[GLA_CHUNK:] * carry], axis=0).astype(BF16)
        k_inv = (k * jnp.exp(-g)).astype(BF16)
        k_inv_a = jnp.concatenate([k_inv[:GLA_CHUNK], no_keys], axis=0)
        k_inv_b = jnp.concatenate([no_keys, k_inv[GLA_CHUNK:]], axis=0)
        k_end_t = (k.T * jnp.exp(tail[:, :GLA_SUPER])).astype(BF16)
        decay = jnp.exp(tail[:, GLA_SUPER:])
        return q_carry, q_dec.astype(BF16), k_inv_a, k_inv_b, k_end_t, decay

    def intra(c, q_carry, q_dec, k_inv_a, k_inv_b, k_end_t, decay):
        v = v_ref[0, pl.ds(c * GLA_SUPER, GLA_SUPER), :]
        att = (lax.dot_general(q_carry, k_inv_a, _NT, preferred_element_type=F32)
               + lax.dot_general(q_dec, k_inv_b, _NT, preferred_element_type=F32))
        att = jnp.where(causal, att, 0.0).astype(BF16)
        o_intra = jnp.dot(att, v, preferred_element_type=F32)
        upd = jnp.dot(k_end_t, v, preferred_element_type=F32)
        return q_carry, o_intra, upd, decay

    def inter(c, q_carry, o_intra, upd, decay):
        rows = pl.ds(c * GLA_SUPER, GLA_SUPER)
        state = s_ref[...]
        o = o_intra + jnp.dot(q_carry, state.astype(BF16), preferred_element_type=F32)
        s_ref[...] = jnp.concatenate([decay] * (GLA_DV // GLA_SUPER), axis=1) * state + upd
        o = o * _rms_scale(o) * nw_ref[...]
        o = o * (gr_ref[0, rows, :] * gb_ref[0, rows, :]).astype(F32)
        o_ref[0, rows, :] = o.astype(BF16)

    n = tb // GLA_SUPER
    staged = {}
    for step in range(n + 3):
        if step < n:
            staged[step] = log_decay_sums(step)
        if 0 <= step - 1 < n:
            staged[step - 1] = decays(step - 1, *staged[step - 1])
        if 0 <= step - 2 < n:
            staged[step - 2] = intra(step - 2, *staged[step - 2])
        if 0 <= step - 3 < n:
            inter(step - 3, *staged.pop(step - 3))


def _gla(proj3, lr3, w2, bias, norm_w, layer, weights, *, tb):
    b, t, _ = proj3.shape
    nt = t // tb
    steps = b * GLA_HEADS * nt
    cast_in, cast_out, cast_shapes = [], [], []
    for w in weights:
        _, rows, cols = w.shape
        assert rows % (16 * steps) == 0, (rows, steps)
        cast_in.append(pl.BlockSpec((1, rows // steps, cols),
                                    lambda i, h, j: (layer, (i * GLA_HEADS + h) * nt + j, 0)))
        cast_out.append(pl.BlockSpec((rows // steps, cols), lambda i, h, j: ((i * GLA_HEADS + h) * nt + j, 0)))
        cast_shapes.append(jax.ShapeDtypeStruct((rows, cols), BF16))
    return pl.pallas_call(
        functools.partial(_gla_kernel, tb=tb, n_cast=len(weights)),
        grid=(b, GLA_HEADS, nt),
        in_specs=[
            pl.BlockSpec((1, tb, GLA_DK), lambda i, h, j: (i, j, COL_GQ // GLA_DK + h)),
            pl.BlockSpec((1, tb, GLA_DK), lambda i, h, j: (i, j, COL_GK // GLA_DK + h)),
            pl.BlockSpec((1, tb, GLA_DV), lambda i, h, j: (i, j, COL_GV // GLA_DV + h)),
            pl.BlockSpec((1, tb, GLA_DV), lambda i, h, j: (i, j, COL_GR // GLA_DV + h)),
            pl.BlockSpec((1, tb, GLA_DV), lambda i, h, j: (i, j, COL_GATE_B // GLA_DV + h)),
            pl.BlockSpec((1, tb, LANES), lambda i, h, j: (i, j, 0)),
            pl.BlockSpec((LANES, GLA_DK), lambda i, h, j: (0, h)),
            pl.BlockSpec((1, GLA_DK), lambda i, h, j: (0, h)),
            pl.BlockSpec((1, GLA_DV), lambda i, h, j: (0, 0)),
        ] + cast_in,
        out_specs=cast_out + [pl.BlockSpec((1, tb, GLA_DV), lambda i, h, j: (i, j, h))],
        out_shape=cast_shapes + [jax.ShapeDtypeStruct((b, t, D_MODEL), BF16)],
        scratch_shapes=[pltpu.VMEM((GLA_DK, GLA_DV), F32)],
        compiler_params=pltpu.CompilerParams(
            dimension_semantics=("arbitrary", "arbitrary", "arbitrary"), vmem_limit_bytes=VMEM_LIMIT),
        name="gla",
    )(proj3, proj3, proj3, proj3, proj3, lr3, w2, bias, norm_w, *weights)


def _outproj_kernel(a_ref, g_ref, x_ref, w_ref, h_ref):
    merged = a_ref[...] + g_ref[...]
    h_ref[...] = x_ref[...] + jnp.dot(merged, w_ref[...], preferred_element_type=F32)


def _outproj(a2, g2, x2, w_out, *, tm):
    n = x2.shape[0]
    return pl.pallas_call(
        _outproj_kernel,
        grid=(n // tm,),
        in_specs=[
            pl.BlockSpec((tm, D_MODEL), lambda i: (i, 0)),
            pl.BlockSpec((tm, D_MODEL), lambda i: (i, 0)),
            pl.BlockSpec((tm, D_MODEL), lambda i: (i, 0)),
            pl.BlockSpec((D_MODEL, D_MODEL), lambda i: (0, 0)),
        ],
        out_specs=pl.BlockSpec((tm, D_MODEL), lambda i: (i, 0)),
        out_shape=jax.ShapeDtypeStruct((n, D_MODEL), F32),
        compiler_params=pltpu.CompilerParams(
            dimension_semantics=("arbitrary",), vmem_limit_bytes=VMEM_LIMIT),
        name="outproj",
    )(a2, g2, x2, w_out)


def _ffn_kernel(h_ref, nw_ref, wg_ref, wu_ref, wd_ref, fw_ref, o_ref, v_ref, *, final_norm):
    j = pl.program_id(1)
    last = pl.num_programs(1) - 1
    tm, th = h_ref.shape[0], wg_ref.shape[1]
    row_halves = [pl.ds(r * tm // 2, tm // 2) for r in range(2)]

    def hidden_tile(rows):
        v = v_ref[rows, :]
        part = None
        halves = [(c * th // 2, (c + 1) * th // 2) for c in range(2)]
        acts = []
        for lo, hi in halves:
            gate = jnp.dot(v, wg_ref[:, lo:hi], preferred_element_type=F32)
            up = jnp.dot(v, wu_ref[:, lo:hi], preferred_element_type=F32)
            acts.append((gate, up))
        for (lo, hi), (gate, up) in zip(halves, acts):
            act = (gate * _sigmoid(gate) * up).astype(BF16)
            contrib = jnp.dot(act, wd_ref[lo:hi, :], preferred_element_type=F32)
            part = contrib if part is None else part + contrib
        o_ref[rows, :] += part

    def prologue(rows):
        h = h_ref[rows, :]
        v_ref[rows, :] = (h * _rms_scale(h) * nw_ref[...]).astype(BF16)
        o_ref[rows, :] = h

    def epilogue(rows):
        y = o_ref[rows, :]
        o_ref[rows, :] = y * _rms_scale(y) * fw_ref[...]

    @pl.when(j == 0)
    def _():
        for rows in row_halves:
            prologue(rows)
        for rows in row_halves:
            hidden_tile(rows)

    @pl.when((j > 0) & (j < last))
    def _():
        hidden_tile(pl.ds(0, tm))

    @pl.when(j == last)
    def _():
        for rows in row_halves:
            hidden_tile(rows)
            if final_norm:
                epilogue(rows)


def _ffn(h2, norm_w, w_gate, w_up, w_down, final_w, *, tm, th, final_norm):
    n = h2.shape[0]
    assert FFN_HIDDEN // th >= 2
    return pl.pallas_call(
        functools.partial(_ffn_kernel, final_norm=final_norm),
        grid=(n // tm, FFN_HIDDEN // th),
        in_specs=[
            pl.BlockSpec((tm, D_MODEL), lambda i, j: (i, 0)),
            pl.BlockSpec((1, D_MODEL), lambda i, j: (0, 0)),
            pl.BlockSpec((D_MODEL, th), lambda i, j: (0, j)),
            pl.BlockSpec((D_MODEL, th), lambda i, j: (0, j)),
            pl.BlockSpec((th, D_MODEL), lambda i, j: (j, 0)),
            pl.BlockSpec((1, D_MODEL), lambda i, j: (0, 0)),
        ],
        out_specs=pl.BlockSpec((tm, D_MODEL), lambda i, j: (i, 0)),
        out_shape=jax.ShapeDtypeStruct((n, D_MODEL), F32),
        scratch_shapes=[pltpu.VMEM((tm, D_MODEL), BF16)],
        compiler_params=pltpu.CompilerParams(
            dimension_semantics=("arbitrary", "arbitrary"), vmem_limit_bytes=VMEM_LIMIT),
        name="ffn",
    )(h2, norm_w, w_gate, w_up, w_down, final_w)


_IN_WIDTHS = (D_MODEL, KV_WIDTH, KV_WIDTH, GLA_HEADS * GLA_DK, GLA_HEADS * GLA_DK, D_MODEL,
              GLA_GATE_RANK, D_MODEL, D_MODEL, D_MODEL)
(IN_AQ, IN_AK, IN_AV, IN_GQ, IN_GK, IN_GV, IN_LR, IN_GR, IN_GATE_A, IN_GATE_B, D_IN) = (
    sum(_IN_WIDTHS[:i]) for i in range(len(_IN_WIDTHS) + 1))


REGROUP_ROWS = 2 * KV_WIDTH
ATTN_Q_SCALE = HEAD_DIM ** -0.5
GLA_Q_SCALE = GLA_DK ** -0.5
assert ATTN_Q_SCALE == 2.0 ** -3 and GLA_Q_SCALE == 2.0 ** -4


def _regroup_source_row(i):
    row = i * REGROUP_ROWS
    src = row - COL_AQ + IN_AQ
    src = jnp.where(row >= COL_GV, row - COL_GV + IN_GV, src)
    src = jnp.where(row >= COL_GR, row - COL_GR + IN_GR, src)
    src = jnp.where(row >= COL_GQ, row - COL_GQ + IN_GQ, src)
    src = jnp.where(row >= COL_KV, row - COL_KV + IN_AK, src)
    return pl.multiple_of(src, GLA_GATE_RANK)


def _regroup_kernel(wt_ref, o_ref):
    row = pl.program_id(0) * REGROUP_ROWS
    scale = jnp.where(row < COL_AQ + D_MODEL, ATTN_Q_SCALE, 1.0)
    scale = jnp.where((row >= COL_GQ) & (row < COL_GK), GLA_Q_SCALE, scale)
    o_ref[...] = (wt_ref[0] * scale).astype(BF16)


def _regroup_w_in(w_in_t, layer):
    rows = COL_KV + 2 * KV_WIDTH
    return pl.pallas_call(
        _regroup_kernel,
        grid=(rows // REGROUP_ROWS,),
        in_specs=[pl.BlockSpec((pl.Element(1), pl.Element(REGROUP_ROWS), pl.Element(D_MODEL)),
                               lambda i: (layer, _regroup_source_row(i), 0))],
        out_specs=pl.BlockSpec((REGROUP_ROWS, D_MODEL), lambda i: (i, 0)),
        out_shape=jax.ShapeDtypeStruct((rows, D_MODEL), BF16),
        compiler_params=pltpu.CompilerParams(
            dimension_semantics=("arbitrary",), vmem_limit_bytes=VMEM_LIMIT),
        name="regroup_w_in",
    )(w_in_t)


def _tile(n, pref):
    return pref if n % pref == 0 else n


def kernel(x, norm1_w, w_in, gla_gate_w2, gla_gate_b, attn_sinks, gla_norm_w, w_out, norm2_w,
           w_ffn_gate, w_ffn_up, w_ffn_down, final_norm_w):
    b, t, d = x.shape
    n = b * t
    depth = w_in.shape[0]
    h2 = x.reshape(n, d)
    for l in range(depth):
        w_in_t = jnp.swapaxes(w_in, 1, 2)
        proj, kv, lr = _inproj(h2, norm1_w[l].reshape(1, d), _regroup_w_in(w_in_t, l),
                               w_in_t[l, IN_LR:IN_LR + GLA_GATE_RANK], tm=_tile(n, 1024))
        proj3 = proj.reshape(b, t, PROJ_WIDTH)

        wg, wu, a = _attention(proj3, kv.reshape(b, t, 2 * KV_WIDTH), attn_sinks[l], l,
                               (w_ffn_gate, w_ffn_up), tq=_tile(t, 512))

        w2 = jnp.pad(gla_gate_w2[l], ((0, LANES - GLA_GATE_RANK), (0, 0))).astype(BF16)
        wo, wd, g = _gla(proj3, lr.reshape(b, t, LANES), w2, gla_gate_b[l].reshape(1, -1),
                         gla_norm_w[l].reshape(1, GLA_DV), l, (w_out, w_ffn_down), tb=_tile(t, 2048))

        h2 = _outproj(a.reshape(n, d), g.reshape(n, d), h2, wo, tm=_tile(n, 512))
        h2 = _ffn(h2, norm2_w[l].reshape(1, d), wg, wu, wd, final_norm_w.reshape(1, d),
                  tm=_tile(n, 1024), th=512, final_norm=l == depth - 1)
    return h2.reshape(b, t, d)
```

```python
import functools

import jax
import jax.numpy as jnp
from jax import lax
from jax.experimental import pallas as pl
from jax.experimental.pallas import tpu as pltpu

F32 = jnp.float32
BF16 = jnp.bfloat16

D_MODEL = 2048
HEAD_DIM = 64
N_Q_HEADS = D_MODEL // HEAD_DIM
N_KV_HEADS = 4
GQA_GROUP = N_Q_HEADS // N_KV_HEADS
KV_WIDTH = N_KV_HEADS * HEAD_DIM
WINDOW = 128

GLA_HEADS = 4
GLA_DK = (D_MODEL // 2) // GLA_HEADS
GLA_DV = D_MODEL // GLA_HEADS
GLA_GATE_RANK = 16
GLA_GATE_NORMALIZER = 16.0
GLA_CHUNK = 64

FFN_HIDDEN = ((8 * D_MODEL // 3 + 255) // 256) * 256
RMS_EPS = 1e-6
MASK_VALUE = -1e30

LANES = 128

COL_AQ = 0
COL_GV = COL_AQ + D_MODEL
COL_GR = COL_GV + D_MODEL
COL_GATE_A = COL_GR + D_MODEL
COL_GATE_B = COL_GATE_A + D_MODEL
COL_GQ = COL_GATE_B + D_MODEL
COL_GK = COL_GQ + GLA_HEADS * GLA_DK
PROJ_WIDTH = COL_GK + GLA_HEADS * GLA_DK
PROJ_TILE = 2048
COL_KV = PROJ_WIDTH

VMEM_LIMIT = 56 * 1024 * 1024

_NT = (((1,), (1,)), ((), ()))
_TN = (((0,), (0,)), ((), ()))


def _rms_scale(x):
    return lax.rsqrt(jnp.mean(x * x, axis=-1, keepdims=True) + RMS_EPS)


def _sigmoid(x):
    return 0.5 * jnp.tanh(0.5 * x) + 0.5


def _inproj_kernel(x_ref, nw_ref, w_ref, wkv_ref, wlr_ref, proj_ref, kv_ref, lr_ref, u_ref):
    j = pl.program_id(1)

    @pl.when(j == 0)
    def _():
        x = x_ref[...]
        u = (x * _rms_scale(x) * nw_ref[...]).astype(BF16)
        u_ref[...] = u
        kv_ref[...] = lax.dot_general(u, wkv_ref[...], _NT, preferred_element_type=F32).astype(BF16)
        w_lr = wlr_ref[...].astype(BF16)
        w_lr = jnp.concatenate([w_lr, jnp.zeros((LANES - GLA_GATE_RANK, D_MODEL), BF16)], axis=0)
        lr_ref[...] = lax.dot_general(u, w_lr, _NT, preferred_element_type=F32).astype(BF16)

    def tile():
        return lax.dot_general(u_ref[...], w_ref[...], _NT, preferred_element_type=F32)

    is_silu = (j >= COL_GR // PROJ_TILE) & (j < COL_GATE_A // PROJ_TILE)
    is_sigmoid = (j >= COL_GATE_A // PROJ_TILE) & (j < COL_GQ // PROJ_TILE)

    @pl.when(is_silu)
    def _():
        y = tile()
        proj_ref[...] = (y * _sigmoid(y)).astype(BF16)

    @pl.when(is_sigmoid)
    def _():
        proj_ref[...] = _sigmoid(tile()).astype(BF16)

    @pl.when(jnp.logical_not(is_silu | is_sigmoid))
    def _():
        proj_ref[...] = tile().astype(BF16)


def _inproj(x2, norm_w, w_t, w_lr_t, *, tm):
    n = x2.shape[0]
    return pl.pallas_call(
        _inproj_kernel,
        grid=(n // tm, PROJ_WIDTH // PROJ_TILE),
        in_specs=[
            pl.BlockSpec((tm, D_MODEL), lambda i, j: (i, 0)),
            pl.BlockSpec((1, D_MODEL), lambda i, j: (0, 0)),
            pl.BlockSpec((PROJ_TILE, D_MODEL), lambda i, j: (j, 0)),
            pl.BlockSpec((2 * KV_WIDTH, D_MODEL), lambda i, j: (COL_KV // (2 * KV_WIDTH), 0)),
            pl.BlockSpec((GLA_GATE_RANK, D_MODEL), lambda i, j: (0, 0)),
        ],
        out_specs=[
            pl.BlockSpec((tm, PROJ_TILE), lambda i, j: (i, j)),
            pl.BlockSpec((tm, 2 * KV_WIDTH), lambda i, j: (i, 0)),
            pl.BlockSpec((tm, LANES), lambda i, j: (i, 0)),
        ],
        out_shape=[
            jax.ShapeDtypeStruct((n, PROJ_WIDTH), BF16),
            jax.ShapeDtypeStruct((n, 2 * KV_WIDTH), BF16),
            jax.ShapeDtypeStruct((n, LANES), BF16),
        ],
        scratch_shapes=[pltpu.VMEM((tm, D_MODEL), BF16)],
        compiler_params=pltpu.CompilerParams(
            dimension_semantics=("arbitrary", "arbitrary"), vmem_limit_bytes=VMEM_LIMIT),
        name="inproj",
    )(x2, norm_w, w_t, w_t, w_lr_t)


PAIR = 2 * HEAD_DIM
PAIRS_PER_GROUP = GQA_GROUP // 2
STACK = PAIRS_PER_GROUP * WINDOW


def _attn_blocks(sink_ref, q_ref, kv_ref, ga_ref, o_ref, blocks):
    lo = lax.broadcasted_iota(jnp.int32, (1, PAIR), 1) < HEAD_DIM
    top = lax.broadcasted_iota(jnp.int32, (PAIR, 1), 0) < HEAD_DIM
    ki = lax.broadcasted_iota(jnp.int32, (WINDOW, STACK), 0)
    qi = lax.broadcasted_iota(jnp.int32, (WINDOW, STACK), 1) & (WINDOW - 1)
    upper = ki > qi

    def rows(blk):
        return pl.ds(blocks[blk][0], WINDOW)

    def krows(blk):
        _, k_row0, has_prev = blocks[blk]
        return pl.ds(k_row0, 2 * WINDOW if has_prev else WINDOW)

    def halves(pair_tile, g):
        swapped = pltpu.roll(pair_tile, HEAD_DIM, axis=1)
        own, other = (pair_tile, swapped) if g % 2 == 0 else (swapped, pair_tile)
        zero = jnp.zeros_like(pair_tile)
        return jnp.where(lo, own, zero), jnp.where(lo, zero, other)

    def probs(st, sink_row, has_prev):
        if has_prev:
            folded = jnp.where(upper, st[:WINDOW], st[WINDOW:])
        else:
            folded = jnp.where(upper, MASK_VALUE, st)
        m = jnp.maximum(jnp.max(folded, axis=0, keepdims=True), sink_row)
        p = jnp.exp(folded - m)
        inv = 1.0 / (jnp.sum(p, axis=0, keepdims=True) + jnp.exp(sink_row - m))
        if has_prev:
            p = jnp.concatenate([jnp.where(upper, p, 0.0), jnp.where(upper, 0.0, p)], axis=0)
        return p.astype(BF16), inv

    def scores(blk, g):
        k_lanes = slice((g // 2) * PAIR, (g // 2 + 1) * PAIR)
        k_lo, k_hi = halves(kv_ref[0, krows(blk), k_lanes], g)
        first_pair = g * PAIRS_PER_GROUP
        q = jnp.concatenate(
            [q_ref[0, rows(blk), (first_pair + p) * PAIR:(first_pair + p + 1) * PAIR]
             for p in range(PAIRS_PER_GROUP)], axis=0)
        return (lax.dot_general(k_lo, q, _NT, preferred_element_type=F32),
                lax.dot_general(k_hi, q, _NT, preferred_element_type=F32))

    def weighted_values(blk, g, st_even, st_odd):
        v_lanes = slice(KV_WIDTH + (g // 2) * PAIR, KV_WIDTH + (g // 2 + 1) * PAIR)
        v_lo, v_hi = halves(kv_ref[0, krows(blk), v_lanes], g)
        first_pair = g * PAIRS_PER_GROUP
        sink_even = jnp.concatenate(
            [jnp.full((1, WINDOW), sink_ref[2 * (first_pair + p)], F32) for p in range(PAIRS_PER_GROUP)], axis=1)
        sink_odd = jnp.concatenate(
            [jnp.full((1, WINDOW), sink_ref[2 * (first_pair + p) + 1], F32) for p in range(PAIRS_PER_GROUP)],
            axis=1)
        p_even, inv_even = probs(st_even, sink_even, blocks[blk][2])
        p_odd, inv_odd = probs(st_odd, sink_odd, blocks[blk][2])
        ot = (lax.dot_general(v_lo, p_even, _TN, preferred_element_type=F32)
              + lax.dot_general(v_hi, p_odd, _TN, preferred_element_type=F32))
        return ot, jnp.where(top, inv_even, inv_odd)

    def finish(blk, g, ot, inv):
        o = (ot * inv).T
        first_pair = g * PAIRS_PER_GROUP
        for p in range(PAIRS_PER_GROUP):
            cols = slice((first_pair + p) * PAIR, (first_pair + p + 1) * PAIR)
            gate = ga_ref[0, rows(blk), cols].astype(F32)
            o_ref[0, rows(blk), cols] = (gate * o[p * WINDOW:(p + 1) * WINDOW]).astype(BF16)

    items = [(blk, g) for blk in range(len(blocks)) for g in range(N_KV_HEADS)]
    st_next = scores(*items[0])
    unfinished = None
    for n, item in enumerate(items):
        st_even, st_odd = st_next
        if n + 1 < len(items):
            st_next = scores(*items[n + 1])
        if unfinished is not None:
            finish(*unfinished)
        unfinished = item + weighted_values(*item, st_even, st_odd)
    finish(*unfinished)


ATTN_RING = 3


def _attn_kernel(sink_ref, proj_hbm, kv_ref, *rest, tq, n_cast, total_steps):
    cast_in, cast_out = rest[:n_cast], rest[n_cast:2 * n_cast]
    o_ref, q_buf, ga_buf, sems = rest[2 * n_cast:]
    for src, dst in zip(cast_in, cast_out):
        dst[...] = src[0].astype(BF16)

    nt = pl.num_programs(1)
    step = pl.program_id(0) * nt + pl.program_id(1)

    def ring_copies(s):
        slot = s % ATTN_RING
        rows = pl.ds(pl.multiple_of((s % nt) * tq, tq), tq)
        return (
            pltpu.make_async_copy(proj_hbm.at[s // nt, rows, pl.ds(COL_AQ, D_MODEL)], q_buf.at[slot],
                                  sems.at[0, slot]),
            pltpu.make_async_copy(proj_hbm.at[s // nt, rows, pl.ds(COL_GATE_A, D_MODEL)], ga_buf.at[slot],
                                  sems.at[1, slot]),
        )

    @pl.when(step == 0)
    def _():
        for s in range(min(ATTN_RING - 1, total_steps)):
            for copy in ring_copies(s):
                copy.start()

    @pl.when(step + (ATTN_RING - 1) < total_steps)
    def _():
        for copy in ring_copies(step + (ATTN_RING - 1)):
            copy.start()

    for copy in ring_copies(step):
        copy.wait()
    q_ref = q_buf.at[pl.ds(step % ATTN_RING, 1)]
    ga_ref = ga_buf.at[pl.ds(step % ATTN_RING, 1)]

    t = pl.program_id(1)
    later = [(blk * WINDOW, pl.multiple_of(t * tq + (blk - 1) * WINDOW, WINDOW), True)
             for blk in range(1, tq // WINDOW)]

    @pl.when(t == 0)
    def _():
        _attn_blocks(sink_ref, q_ref, kv_ref, ga_ref, o_ref, [(0, 0, False)] + later)

    @pl.when(t > 0)
    def _():
        first = (0, pl.multiple_of(t * tq - WINDOW, WINDOW), True)
        _attn_blocks(sink_ref, q_ref, kv_ref, ga_ref, o_ref, [first] + later)


def _attention(proj3, kv3, sinks, layer, weights, *, tq):
    b, t, _ = proj3.shape
    steps = b * (t // tq)
    cast_in, cast_out, cast_shapes = [], [], []
    for w in weights:
        _, rows, cols = w.shape
        assert rows % (16 * steps) == 0, (rows, steps)
        cast_in.append(pl.BlockSpec((1, rows // steps, cols), lambda i, j: (layer, i * (t // tq) + j, 0)))
        cast_out.append(pl.BlockSpec((rows // steps, cols), lambda i, j: (i * (t // tq) + j, 0)))
        cast_shapes.append(jax.ShapeDtypeStruct((rows, cols), BF16))
    return pl.pallas_call(
        functools.partial(_attn_kernel, tq=tq, n_cast=len(weights), total_steps=steps),
        grid=(b, t // tq),
        in_specs=[
            pl.BlockSpec(memory_space=pltpu.SMEM),
            pl.BlockSpec(memory_space=pl.ANY),
            pl.BlockSpec((1, t, 2 * KV_WIDTH), lambda i, j: (i, 0, 0)),
        ] + cast_in,
        out_specs=cast_out + [pl.BlockSpec((1, tq, D_MODEL), lambda i, j: (i, j, 0))],
        out_shape=cast_shapes + [jax.ShapeDtypeStruct((b, t, D_MODEL), BF16)],
        scratch_shapes=[
            pltpu.VMEM((ATTN_RING, tq, D_MODEL), BF16),
            pltpu.VMEM((ATTN_RING, tq, D_MODEL), BF16),
            pltpu.SemaphoreType.DMA((2, ATTN_RING)),
        ],
        compiler_params=pltpu.CompilerParams(
            dimension_semantics=("arbitrary", "arbitrary"), vmem_limit_bytes=VMEM_LIMIT),
        name="swa_attention",
    )(sinks, proj3, kv3, *weights)


def _log_sigmoid(x):
    return jnp.minimum(x, 0.0) - jnp.log(1.0 + jnp.exp(-jnp.abs(x)))


GLA_SUPER = 2 * GLA_CHUNK


def _split_bf16(x):
    hi = x.astype(BF16)
    return hi, (x - hi.astype(F32)).astype(BF16)


def _gla_kernel(q_ref, k_ref, v_ref, gr_ref, gb_ref, lr_ref, w2_ref, b_ref, nw_ref, *rest, tb, n_cast):
    cast_in, cast_out = rest[:n_cast], rest[n_cast:2 * n_cast]
    o_ref, s_ref = rest[2 * n_cast:]
    for src, dst in zip(cast_in, cast_out):
        dst[...] = src[0].astype(BF16)

    @pl.when(pl.program_id(2) == 0)
    def _():
        s_ref[...] = jnp.zeros_like(s_ref)

    lr = lr_ref[0]
    logit = jnp.dot(lr, w2_ref[...], preferred_element_type=F32) + b_ref[...]
    log_a = _log_sigmoid(logit) * (1.0 / GLA_GATE_NORMALIZER)

    ri = lax.broadcasted_iota(jnp.int32, (GLA_SUPER, GLA_SUPER), 0)
    ci = lax.broadcasted_iota(jnp.int32, (GLA_SUPER, GLA_SUPER), 1)
    causal = ri >= ci
    cum = (causal & ((ri // GLA_CHUNK) == (ci // GLA_CHUNK))).astype(BF16)
    cum_twice = jnp.concatenate([cum, cum], axis=1)
    later_and_all = jnp.concatenate([(ri > ci).astype(BF16), jnp.ones((GLA_SUPER, GLA_SUPER), BF16)], axis=1)
    later_and_all_twice = jnp.concatenate([later_and_all, later_and_all], axis=0)
    no_keys = jnp.zeros((GLA_CHUNK, GLA_DK), BF16)

    def log_decay_sums(c):
        la = log_a[c * GLA_SUPER:(c + 1) * GLA_SUPER, :]
        g = jnp.dot(cum_twice, jnp.concatenate(_split_bf16(la), axis=0),
                    preferred_element_type=F32)
        tail = jnp.dot(jnp.concatenate(_split_bf16(la.T), axis=1), later_and_all_twice,
                       preferred_element_type=F32)
        return g, tail

    def decays(c, g, tail):
        rows = pl.ds(c * GLA_SUPER, GLA_SUPER)
        q = q_ref[0, rows, :].astype(F32)
        k = k_ref[0, rows, :].astype(F32)
        q_dec = q * jnp.exp(g)
        carry = jnp.exp(g[GLA_CHUNK - 1:GLA_CHUNK, :])
        q_carry = jnp.concatenate([q_dec[:GLA_CHUNK], q_dec[GLA_CHUNK:] * carry], axis=0).astype(BF16)
        k_inv = (k * jnp.exp(-g)).astype(BF16)
        k_inv_a = jnp.concatenate([k_inv[:GLA_CHUNK], no_keys], axis=0)
        k_inv_b = jnp.concatenate([no_keys, k_inv[GLA_CHUNK:]], axis=0)
        k_end_t = (k.T * jnp.exp(tail[:, :GLA_SUPER])).astype(BF16)
        decay = jnp.exp(tail[:, GLA_SUPER:])
        return q_carry, q_dec.astype(BF16), k_inv_a, k_inv_b, k_end_t, decay

    def intra(c, q_carry, q_dec, k_inv_a, k_inv_b, k_end_t, decay):
        v = v_ref[0, pl.ds(c * GLA_SUPER, GLA_SUPER), :]
        att = (lax.dot_general(q_carry, k_inv_a, _NT, preferred_element_type=F32)
               + lax.dot_general(q_dec, k_inv_b, _NT, preferred_element_type=F32))
        att = jnp.where(causal, att, 0.0).astype(BF16)
        o_intra = jnp.dot(att, v, preferred_element_type=F32)
        upd = jnp.dot(k_end_t, v, preferred_element_type=F32)
        return q_carry, o_intra, upd, decay

    def inter(c, q_carry, o_intra, upd, decay):
        rows = pl.ds(c * GLA_SUPER, GLA_SUPER)
        state = s_ref[...]
        o = o_intra + jnp.dot(q_carry, state.astype(BF16), preferred_element_type=F32)
        s_ref[...] = jnp.concatenate([decay] * (GLA_DV // GLA_SUPER), axis=1) * state + upd
        o = o * _rms_scale(o) * nw_ref[...]
        o = o * (gr_ref[0, rows, :] * gb_ref[0, rows, :]).astype(F32)
        o_ref[0, rows, :] = o.astype(BF16)

    n = tb // GLA_SUPER
    staged = {}
    for step in range(n + 3):
        if step < n:
            staged[step] = log_decay_sums(step)
        if 0 <= step - 1 < n:
            staged[step - 1] = decays(step - 1, *staged[step - 1])
        if 0 <= step - 2 < n:
            staged[step - 2] = intra(step - 2, *staged[step - 2])
        if 0 <= step - 3 < n:
            inter(step - 3, *staged.pop(step - 3))


def _gla(proj3, lr3, w2, bias, norm_w, layer, weights, *, tb):
    b, t, _ = proj3.shape
    nt = t // tb
    steps = b * GLA_HEADS * nt
    cast_in, cast_out, cast_shapes = [], [], []
    for w in weights:
        _, rows, cols = w.shape
        assert rows % (16 * steps) == 0, (rows, steps)
        cast_in.append(pl.BlockSpec((1, rows // steps, cols),
                                    lambda i, h, j: (layer, (i * GLA_HEADS + h) * nt + j, 0)))
        cast_out.append(pl.BlockSpec((rows // steps, cols), lambda i, h, j: ((i * GLA_HEADS + h) * nt + j, 0)))
        cast_shapes.append(jax.ShapeDtypeStruct((rows, cols), BF16))
    return pl.pallas_call(
        functools.partial(_gla_kernel, tb=tb, n_cast=len(weights)),
        grid=(b, GLA_HEADS, nt),
        in_specs=[
            pl.BlockSpec((1, tb, GLA_DK), lambda i, h, j: (i, j, COL_GQ // GLA_DK + h)),
            pl.BlockSpec((1, tb, GLA_DK), lambda i, h, j: (i, j, COL_GK // GLA_DK + h)),
            pl.BlockSpec((1, tb, GLA_DV), lambda i, h, j: (i, j, COL_GV // GLA_DV + h)),
            pl.BlockSpec((1, tb, GLA_DV), lambda i, h, j: (i, j, COL_GR // GLA_DV + h)),
            pl.BlockSpec((1, tb, GLA_DV), lambda i, h, j: (i, j, COL_GATE_B // GLA_DV + h)),
            pl.BlockSpec((1, tb, LANES), lambda i, h, j: (i, j, 0)),
            pl.BlockSpec((LANES, GLA_DK), lambda i, h, j: (0, h)),
            pl.BlockSpec((1, GLA_DK), lambda i, h, j: (0, h)),
            pl.BlockSpec((1, GLA_DV), lambda i, h, j: (0, 0)),
        ] + cast_in,
        out_specs=cast_out + [pl.BlockSpec((1, tb, GLA_DV), lambda i, h, j: (i, j, h))],
        out_shape=cast_shapes + [jax.ShapeDtypeStruct((b, t, D_MODEL), BF16)],
        scratch_shapes=[pltpu.VMEM((GLA_DK, GLA_DV), F32)],
        compiler_params=pltpu.CompilerParams(
            dimension_semantics=("arbitrary", "arbitrary", "arbitrary"), vmem_limit_bytes=VMEM_LIMIT),
        name="gla",
    )(proj3, proj3, proj3, proj3, proj3, lr3, w2, bias, norm_w, *weights)


def _outproj_kernel(a_ref, g_ref, x_ref, w_ref, h_ref):
    merged = a_ref[...] + g_ref[...]
    h_ref[...] = x_ref[...] + jnp.dot(merged, w_ref[...], preferred_element_type=F32)


def _outproj(a2, g2, x2, w_out, *, tm):
    n = x2.shape[0]
    return pl.pallas_call(
        _outproj_kernel,
        grid=(n // tm,),
        in_specs=[
            pl.BlockSpec((tm, D_MODEL), lambda i: (i, 0)),
            pl.BlockSpec((tm, D_MODEL), lambda i: (i, 0)),
            pl.BlockSpec((tm, D_MODEL), lambda i: (i, 0)),
            pl.BlockSpec((D_MODEL, D_MODEL), lambda i: (0, 0)),
        ],
        out_specs=pl.BlockSpec((tm, D_MODEL), lambda i: (i, 0)),
        out_shape=jax.ShapeDtypeStruct((n, D_MODEL), F32),
        compiler_params=pltpu.CompilerParams(
            dimension_semantics=("arbitrary",), vmem_limit_bytes=VMEM_LIMIT),
        name="outproj",
    )(a2, g2, x2, w_out)


def _ffn_kernel(h_ref, nw_ref, wg_ref, wu_ref, wd_ref, fw_ref, o_ref, v_ref, *, final_norm):
    j = pl.program_id(1)
    last = pl.num_programs(1) - 1
    tm, th = h_ref.shape[0], wg_ref.shape[1]
    row_halves = [pl.ds(r * tm // 2, tm // 2) for r in range(2)]

    def hidden_tile(rows):
        v = v_ref[rows, :]
        part = None
        halves = [(c * th // 2, (c + 1) * th // 2) for c in range(2)]
        acts = []
        for lo, hi in halves:
            gate = jnp.dot(v, wg_ref[:, lo:hi], preferred_element_type=F32)
            up = jnp.dot(v, wu_ref[:, lo:hi], preferred_element_type=F32)
            acts.append((gate, up))
        for (lo, hi), (gate, up) in zip(halves, acts):
            act = (gate * _sigmoid(gate) * up).astype(BF16)
            contrib = jnp.dot(act, wd_ref[lo:hi, :], preferred_element_type=F32)
            part = contrib if part is None else part + contrib
        o_ref[rows, :] += part

    def prologue(rows):
        h = h_ref[rows, :]
        v_ref[rows, :] = (h * _rms_scale(h) * nw_ref[...]).astype(BF16)
        o_ref[rows, :] = h

    def epilogue(rows):
        y = o_ref[rows, :]
        o_ref[rows, :] = y * _rms_scale(y) * fw_ref[...]

    @pl.when(j == 0)
    def _():
        for rows in row_halves:
            prologue(rows)
        for rows in row_halves:
            hidden_tile(rows)

    @pl.when((j > 0) & (j < last))
    def _():
        hidden_tile(pl.ds(0, tm))

    @pl.when(j == last)
    def _():
        for rows in row_halves:
            hidden_tile(rows)
            if final_norm:
                epilogue(rows)


def _ffn(h2, norm_w, w_gate, w_up, w_down, final_w, *, tm, th, final_norm):
    n = h2.shape[0]
    assert FFN_HIDDEN // th >= 2
    return pl.pallas_call(
        functools.partial(_ffn_kernel, final_norm=final_norm),
        grid=(n // tm, FFN_HIDDEN // th),
        in_specs=[
            pl.BlockSpec((tm, D_MODEL), lambda i, j: (i, 0)),
            pl.BlockSpec((1, D_MODEL), lambda i, j: (0, 0)),
            pl.BlockSpec((D_MODEL, th), lambda i, j: (0, j)),
            pl.BlockSpec((D_MODEL, th), lambda i, j: (0, j)),
            pl.BlockSpec((th, D_MODEL), lambda i, j: (j, 0)),
            pl.BlockSpec((1, D_MODEL), lambda i, j: (0, 0)),
        ],
        out_specs=pl.BlockSpec((tm, D_MODEL), lambda i, j: (i, 0)),
        out_shape=jax.ShapeDtypeStruct((n, D_MODEL), F32),
        scratch_shapes=[pltpu.VMEM((tm, D_MODEL), BF16)],
        compiler_params=pltpu.CompilerParams(
            dimension_semantics=("arbitrary", "arbitrary"), vmem_limit_bytes=VMEM_LIMIT),
        name="ffn",
    )(h2, norm_w, w_gate, w_up, w_down, final_w)


_IN_WIDTHS = (D_MODEL, KV_WIDTH, KV_WIDTH, GLA_HEADS * GLA_DK, GLA_HEADS * GLA_DK, D_MODEL,
              GLA_GATE_RANK, D_MODEL, D_MODEL, D_MODEL)
(IN_AQ, IN_AK, IN_AV, IN_GQ, IN_GK, IN_GV, IN_LR, IN_GR, IN_GATE_A, IN_GATE_B, D_IN) = (
    sum(_IN_WIDTHS[:i]) for i in range(len(_IN_WIDTHS) + 1))


REGROUP_ROWS = 2 * KV_WIDTH
ATTN_Q_SCALE = HEAD_DIM ** -0.5
GLA_Q_SCALE = GLA_DK ** -0.5
assert ATTN_Q_SCALE == 2.0 ** -3 and GLA_Q_SCALE == 2.0 ** -4


def _regroup_source_row(i):
    row = i * REGROUP_ROWS
    src = row - COL_AQ + IN_AQ
    src = jnp.where(row >= COL_GV, row - COL_GV + IN_GV, src)
    src = jnp.where(row >= COL_GR, row - COL_GR + IN_GR, src)
    src = jnp.where(row >= COL_GQ, row - COL_GQ + IN_GQ, src)
    src = jnp.where(row >= COL_KV, row - COL_KV + IN_AK, src)
    return pl.multiple_of(src, GLA_GATE_RANK)


def _regroup_kernel(wt_ref, o_ref):
    row = pl.program_id(0) * REGROUP_ROWS
    scale = jnp.where(row < COL_AQ + D_MODEL, ATTN_Q_SCALE, 1.0)
    scale = jnp.where((row >= COL_GQ) & (row < COL_GK), GLA_Q_SCALE, scale)
    o_ref[...] = (wt_ref[0] * scale).astype(BF16)


def _regroup_w_in(w_in_t, layer):
    rows = COL_KV + 2 * KV_WIDTH
    return pl.pallas_call(
        _regroup_kernel,
        grid=(rows // REGROUP_ROWS,),
        in_specs=[pl.BlockSpec((pl.Element(1), pl.Element(REGROUP_ROWS), pl.Element(D_MODEL)),
                               lambda i: (layer, _regroup_source_row(i), 0))],
        out_specs=pl.BlockSpec((REGROUP_ROWS, D_MODEL), lambda i: (i, 0)),
        out_shape=jax.ShapeDtypeStruct((rows, D_MODEL), BF16),
        compiler_params=pltpu.CompilerParams(
            dimension_semantics=("arbitrary",), vmem_limit_bytes=VMEM_LIMIT),
        name="regroup_w_in",
    )(w_in_t)


def _tile(n, pref):
    return pref if n % pref == 0 else n


def kernel(x, norm1_w, w_in, gla_gate_w2, gla_gate_b, attn_sinks, gla_norm_w, w_out, norm2_w,
           w_ffn_gate, w_ffn_up, w_ffn_down, final_norm_w):
    b, t, d = x.shape
    n = b * t
    depth = w_in.shape[0]
    h2 = x.reshape(n, d)
    for l in range(depth):
        w_in_t = jnp.swapaxes(w_in, 1, 2)
        proj, kv, lr = _inproj(h2, norm1_w[l].reshape(1, d), _regroup_w_in(w_in_t, l),
                               w_in_t[l, IN_LR:IN_LR + GLA_GATE_RANK], tm=_tile(n, 1024))
        proj3 = proj.reshape(b, t, PROJ_WIDTH)

        wg, wu, a = _attention(proj3, kv.reshape(b, t, 2 * KV_WIDTH), attn_sinks[l], l,
                               (w_ffn_gate, w_ffn_up), tq=_tile(t, 512))

        w2 = jnp.pad(gla_gate_w2[l], ((0, LANES - GLA_GATE_RANK), (0, 0))).astype(BF16)
        wo, wd, g = _gla(proj3, lr.reshape(b, t, LANES), w2, gla_gate_b[l].reshape(1, -1),
                         gla_norm_w[l].reshape(1, GLA_DV), l, (w_out, w_ffn_down), tb=_tile(t, 2048))

        h2 = _outproj(a.reshape(n, d), g.reshape(n, d), h2, wo, tm=_tile(n, 512))
        h2 = _ffn(h2, norm2_w[l].reshape(1, d), wg, wu, wd, final_norm_w.reshape(1, d),
                  tm=_tile(n, 1024), th=512, final_norm=l == depth - 1)
    return h2.reshape(b, t, d)
```

```python
import functools

import jax
import jax.numpy as jnp
from jax import lax
from jax.experimental import pallas as pl
from jax.experimental.pallas import tpu as pltpu

F32 = jnp.float32
BF16 = jnp.bfloat16

D_MODEL = 2048
HEAD_DIM = 64
N_Q_HEADS = D_MODEL // HEAD_DIM
N_KV_HEADS = 4
GQA_GROUP = N_Q_HEADS // N_KV_HEADS
KV_WIDTH = N_KV_HEADS * HEAD_DIM
WINDOW = 128

GLA_HEADS = 4
GLA_DK = (D_MODEL // 2) // GLA_HEADS
GLA_DV = D_MODEL // GLA_HEADS
GLA_GATE_RANK = 16
GLA_GATE_NORMALIZER = 16.0
GLA_CHUNK = 64

FFN_HIDDEN = ((8 * D_MODEL // 3 + 255) // 256) * 256
RMS_EPS = 1e-6
MASK_VALUE = -1e30

LANES = 128

COL_AQ = 0
COL_GV = COL_AQ + D_MODEL
COL_GR = COL_GV + D_MODEL
COL_GATE_A = COL_GR + D_MODEL
COL_GATE_B = COL_GATE_A + D_MODEL
COL_GQ = COL_GATE_B + D_MODEL
COL_GK = COL_GQ + GLA_HEADS * GLA_DK
PROJ_WIDTH = COL_GK + GLA_HEADS * GLA_DK
PROJ_TILE = 2048
COL_KV = PROJ_WIDTH

VMEM_LIMIT = 56 * 1024 * 1024
INPROJ_VMEM_LIMIT = 60 * 1024 * 1024

_NT = (((1,), (1,)), ((), ()))
_TN = (((0,), (0,)), ((), ()))


def _rms_scale(x):
    return lax.rsqrt(jnp.mean(x * x, axis=-1, keepdims=True) + RMS_EPS)


def _sigmoid(x):
    return 0.5 * jnp.tanh(0.5 * x) + 0.5


INPROJ_CAST_STEPS = 4


def _inproj_kernel(x_ref, nw_ref, w_ref, wkv_ref, wlr_ref, *rest, n_cast):
    cast_in, (proj_ref, kv_ref, lr_ref) = rest[:n_cast], rest[n_cast:n_cast + 3]
    cast_out, u_ref = rest[n_cast + 3:2 * n_cast + 3], rest[2 * n_cast + 3]
    j = pl.program_id(1)

    @pl.when(j < INPROJ_CAST_STEPS)
    def _():
        for src, dst in zip(cast_in, cast_out):
            dst[...] = src[0].astype(BF16)

    @pl.when(j == 0)
    def _():
        x = x_ref[...]
        u = (x * _rms_scale(x) * nw_ref[...]).astype(BF16)
        u_ref[...] = u
        kv_ref[...] = lax.dot_general(u, wkv_ref[...], _NT, preferred_element_type=F32).astype(BF16)
        w_lr = wlr_ref[...].astype(BF16)
        w_lr = jnp.concatenate([w_lr, jnp.zeros((LANES - GLA_GATE_RANK, D_MODEL), BF16)], axis=0)
        lr_ref[...] = lax.dot_general(u, w_lr, _NT, preferred_element_type=F32).astype(BF16)

    def tile():
        return lax.dot_general(u_ref[...], w_ref[...], _NT, preferred_element_type=F32)

    is_silu = (j >= COL_GR // PROJ_TILE) & (j < COL_GATE_A // PROJ_TILE)
    is_sigmoid = (j >= COL_GATE_A // PROJ_TILE) & (j < COL_GQ // PROJ_TILE)

    @pl.when(is_silu)
    def _():
        y = tile()
        proj_ref[...] = (y * _sigmoid(y)).astype(BF16)

    @pl.when(is_sigmoid)
    def _():
        proj_ref[...] = _sigmoid(tile()).astype(BF16)

    @pl.when(jnp.logical_not(is_silu | is_sigmoid))
    def _():
        proj_ref[...] = tile().astype(BF16)


def _inproj(x2, norm_w, w_t, w_lr_t, layer, weights, *, tm):
    n = x2.shape[0]
    assert PROJ_WIDTH // PROJ_TILE >= INPROJ_CAST_STEPS
    steps = (n // tm) * INPROJ_CAST_STEPS

    def cast_block(i, j):
        return i * INPROJ_CAST_STEPS + jnp.minimum(j, INPROJ_CAST_STEPS - 1)

    cast_in, cast_out, cast_shapes = [], [], []
    for w in weights:
        _, rows, cols = w.shape
        assert rows % (16 * steps) == 0, (rows, steps)
        cast_in.append(pl.BlockSpec((1, rows // steps, cols), lambda i, j: (layer, cast_block(i, j), 0)))
        cast_out.append(pl.BlockSpec((rows // steps, cols), lambda i, j: (cast_block(i, j), 0)))
        cast_shapes.append(jax.ShapeDtypeStruct((rows, cols), BF16))
    return pl.pallas_call(
        functools.partial(_inproj_kernel, n_cast=len(weights)),
        grid=(n // tm, PROJ_WIDTH // PROJ_TILE),
        in_specs=[
            pl.BlockSpec((tm, D_MODEL), lambda i, j: (i, 0)),
            pl.BlockSpec((1, D_MODEL), lambda i, j: (0, 0)),
            pl.BlockSpec((PROJ_TILE, D_MODEL), lambda i, j: (j, 0)),
            pl.BlockSpec((2 * KV_WIDTH, D_MODEL), lambda i, j: (COL_KV // (2 * KV_WIDTH), 0)),
            pl.BlockSpec((GLA_GATE_RANK, D_MODEL), lambda i, j: (0, 0)),
        ] + cast_in,
        out_specs=[
            pl.BlockSpec((tm, PROJ_TILE), lambda i, j: (i, j)),
            pl.BlockSpec((tm, 2 * KV_WIDTH), lambda i, j: (i, 0)),
            pl.BlockSpec((tm, LANES), lambda i, j: (i, 0)),
        ] + cast_out,
        out_shape=[
            jax.ShapeDtypeStruct((n, PROJ_WIDTH), BF16),
            jax.ShapeDtypeStruct((n, 2 * KV_WIDTH), BF16),
            jax.ShapeDtypeStruct((n, LANES), BF16),
        ] + cast_shapes,
        scratch_shapes=[pltpu.VMEM((tm, D_MODEL), BF16)],
        compiler_params=pltpu.CompilerParams(
            dimension_semantics=("arbitrary", "arbitrary"), vmem_limit_bytes=INPROJ_VMEM_LIMIT),
        name="inproj",
    )(x2, norm_w, w_t, w_t, w_lr_t, *weights)


PAIR = 2 * HEAD_DIM
PAIRS_PER_GROUP = GQA_GROUP // 2
STACK = PAIRS_PER_GROUP * WINDOW


def _attn_blocks(sink_ref, q_ref, kv_ref, ga_ref, o_ref, blocks):
    lo = lax.broadcasted_iota(jnp.int32, (1, PAIR), 1) < HEAD_DIM
    top = lax.broadcasted_iota(jnp.int32, (PAIR, 1), 0) < HEAD_DIM
    ki = lax.broadcasted_iota(jnp.int32, (WINDOW, STACK), 0)
    qi = lax.broadcasted_iota(jnp.int32, (WINDOW, STACK), 1) & (WINDOW - 1)
    upper = ki > qi

    def rows(blk):
        return pl.ds(blocks[blk][0], WINDOW)

    def krows(blk):
        _, k_row0, has_prev = blocks[blk]
        return pl.ds(k_row0, 2 * WINDOW if has_prev else WINDOW)

    def halves(pair_tile, g):
        swapped = pltpu.roll(pair_tile, HEAD_DIM, axis=1)
        own, other = (pair_tile, swapped) if g % 2 == 0 else (swapped, pair_tile)
        zero = jnp.zeros_like(pair_tile)
        return jnp.where(lo, own, zero), jnp.where(lo, zero, other)

    def probs(st, sink_row, has_prev):
        if has_prev:
            folded = jnp.where(upper, st[:WINDOW], st[WINDOW:])
        else:
            folded = jnp.where(upper, MASK_VALUE, st)
        m = jnp.maximum(jnp.max(folded, axis=0, keepdims=True), sink_row)
        p = jnp.exp(folded - m)
        inv = 1.0 / (jnp.sum(p, axis=0, keepdims=True) + jnp.exp(sink_row - m))
        if has_prev:
            p = jnp.concatenate([jnp.where(upper, p, 0.0), jnp.where(upper, 0.0, p)], axis=0)
        return p.astype(BF16), inv

    def scores(blk, g):
        k_lanes = slice((g // 2) * PAIR, (g // 2 + 1) * PAIR)
        k_lo, k_hi = halves(kv_ref[0, krows(blk), k_lanes], g)
        first_pair = g * PAIRS_PER_GROUP
        q = jnp.concatenate(
            [q_ref[0, rows(blk), (first_pair + p) * PAIR:(first_pair + p + 1) * PAIR]
             for p in range(PAIRS_PER_GROUP)], axis=0)
        return (lax.dot_general(k_lo, q, _NT, preferred_element_type=F32),
                lax.dot_general(k_hi, q, _NT, preferred_element_type=F32))

    def weighted_values(blk, g, st_even, st_odd):
        v_lanes = slice(KV_WIDTH + (g // 2) * PAIR, KV_WIDTH + (g // 2 + 1) * PAIR)
        v_lo, v_hi = halves(kv_ref[0, krows(blk), v_lanes], g)
        first_pair = g * PAIRS_PER_GROUP
        sink_even = jnp.concatenate(
            [jnp.full((1, WINDOW), sink_ref[2 * (first_pair + p)], F32) for p in range(PAIRS_PER_GROUP)], axis=1)
        sink_odd = jnp.concatenate(
            [jnp.full((1, WINDOW), sink_ref[2 * (first_pair + p) + 1], F32) for p in range(PAIRS_PER_GROUP)],
            axis=1)
        p_even, inv_even = probs(st_even, sink_even, blocks[blk][2])
        p_odd, inv_odd = probs(st_odd, sink_odd, blocks[blk][2])
        ot = (lax.dot_general(v_lo, p_even, _TN, preferred_element_type=F32)
              + lax.dot_general(v_hi, p_odd, _TN, preferred_element_type=F32))
        return ot, jnp.where(top, inv_even, inv_odd)

    def finish(blk, g, ot, inv):
        o = (ot * inv).T
        first_pair = g * PAIRS_PER_GROUP
        for p in range(PAIRS_PER_GROUP):
            cols = slice((first_pair + p) * PAIR, (first_pair + p + 1) * PAIR)
            gate = ga_ref[0, rows(blk), cols].astype(F32)
            o_ref[0, rows(blk), cols] = (gate * o[p * WINDOW:(p + 1) * WINDOW]).astype(BF16)

    items = [(blk, g) for blk in range(len(blocks)) for g in range(N_KV_HEADS)]
    st_next = scores(*items[0])
    unfinished = None
    for n, item in enumerate(items):
        st_even, st_odd = st_next
        if n + 1 < len(items):
            st_next = scores(*items[n + 1])
        if unfinished is not None:
            finish(*unfinished)
        unfinished = item + weighted_values(*item, st_even, st_odd)
    finish(*unfinished)


def _attn_kernel(sink_ref, q_ref, kv_ref, ga_ref, *rest, tq, n_cast):
    cast_in, cast_out, o_ref = rest[:n_cast], rest[n_cast:2 * n_cast], rest[2 * n_cast]
    for src, dst in zip(cast_in, cast_out):
        dst[...] = src[0].astype(BF16)

    t = pl.program_id(1)
    later = [(blk * WINDOW, pl.multiple_of(t * tq + (blk - 1) * WINDOW, WINDOW), True)
             for blk in range(1, tq // WINDOW)]

    @pl.when(t == 0)
    def _():
        _attn_blocks(sink_ref, q_ref, kv_ref, ga_ref, o_ref, [(0, 0, False)] + later)

    @pl.when(t > 0)
    def _():
        first = (0, pl.multiple_of(t * tq - WINDOW, WINDOW), True)
        _attn_blocks(sink_ref, q_ref, kv_ref, ga_ref, o_ref, [first] + later)


def _attention(proj3, kv3, sinks, layer, weights, *, tq):
    b, t, _ = proj3.shape
    steps = b * (t // tq)
    cast_in, cast_out, cast_shapes = [], [], []
    for w in weights:
        _, rows, cols = w.shape
        assert rows % (16 * steps) == 0, (rows, steps)
        cast_in.append(pl.BlockSpec((1, rows // steps, cols), lambda i, j: (layer, i * (t // tq) + j, 0)))
        cast_out.append(pl.BlockSpec((rows // steps, cols), lambda i, j: (i * (t // tq) + j, 0)))
        cast_shapes.append(jax.ShapeDtypeStruct((rows, cols), BF16))
    return pl.pallas_call(
        functools.partial(_attn_kernel, tq=tq, n_cast=len(weights)),
        grid=(b, t // tq),
        in_specs=[
            pl.BlockSpec(memory_space=pltpu.SMEM),
            pl.BlockSpec((1, tq, D_MODEL), lambda i, j: (i, j, COL_AQ // D_MODEL)),
            pl.BlockSpec((1, t, 2 * KV_WIDTH), lambda i, j: (i, 0, 0)),
            pl.BlockSpec((1, tq, D_MODEL), lambda i, j: (i, j, COL_GATE_A // D_MODEL)),
        ] + cast_in,
        out_specs=cast_out + [pl.BlockSpec((1, tq, D_MODEL), lambda i, j: (i, j, 0))],
        out_shape=cast_shapes + [jax.ShapeDtypeStruct((b, t, D_MODEL), BF16)],
        compiler_params=pltpu.CompilerParams(
            dimension_semantics=("arbitrary", "arbitrary"), vmem_limit_bytes=VMEM_LIMIT),
        name="swa_attention",
    )(sinks, proj3, kv3, proj3, *weights)


def _log_sigmoid(x):
    return jnp.minimum(x, 0.0) - jnp.log(1.0 + jnp.exp(-jnp.abs(x)))


GLA_SUPER = 2 * GLA_CHUNK


def _split_bf16(x):
    hi = x.astype(BF16)
    return hi, (x - hi.astype(F32)).astype(BF16)


def _gla_kernel(q_ref, k_ref, v_ref, gr_ref, gb_ref, lr_ref, w2_ref, b_ref, nw_ref, *rest, tb, n_cast):
    cast_in, cast_out = rest[:n_cast], rest[n_cast:2 * n_cast]
    o_ref, s_ref = rest[2 * n_cast:]
    for src, dst in zip(cast_in, cast_out):
        dst[...] = src[0].astype(BF16)

    @pl.when(pl.program_id(2) == 0)
    def _():
        s_ref[...] = jnp.zeros_like(s_ref)

    lr = lr_ref[0]
    logit = jnp.dot(lr, w2_ref[...], preferred_element_type=F32) + b_ref[...]
    log_a = _log_sigmoid(logit) * (1.0 / GLA_GATE_NORMALIZER)

    ri = lax.broadcasted_iota(jnp.int32, (GLA_SUPER, GLA_SUPER), 0)
    ci = lax.broadcasted_iota(jnp.int32, (GLA_SUPER, GLA_SUPER), 1)
    causal = ri >= ci
    cum = (causal & ((ri // GLA_CHUNK) == (ci // GLA_CHUNK))).astype(BF16)
    cum_twice = jnp.concatenate([cum, cum], axis=1)
    later_and_all = jnp.concatenate([(ri > ci).astype(BF16), jnp.ones((GLA_SUPER, GLA_SUPER), BF16)], axis=1)
    later_and_all_twice = jnp.concatenate([later_and_all, later_and_all], axis=0)
    no_keys = jnp.zeros((GLA_CHUNK, GLA_DK), BF16)

    def log_decay_sums(c):
        la = log_a[c * GLA_SUPER:(c + 1) * GLA_SUPER, :]
        g = jnp.dot(cum_twice, jnp.concatenate(_split_bf16(la), axis=0),
                    preferred_element_type=F32)
        tail = jnp.dot(jnp.concatenate(_split_bf16(la.T), axis=1), later_and_all_twice,
                       preferred_element_type=F32)
        return g, tail

    def decays(c, g, tail):
        rows = pl.ds(c * GLA_SUPER, GLA_SUPER)
        q = q_ref[0, rows, :].astype(F32)
        k = k_ref[0, rows, :].astype(F32)
        q_dec = q * jnp.exp(g)
        carry = jnp.exp(g[GLA_CHUNK - 1:GLA_CHUNK, :])
        q_carry = jnp.concatenate([q_dec[:GLA_CHUNK], q_dec[GLA_CHUNK:] * carry], axis=0).astype(BF16)
        k_inv = (k * jnp.exp(-g)).astype(BF16)
        k_inv_a = jnp.concatenate([k_inv[:GLA_CHUNK], no_keys], axis=0)
        k_inv_b = jnp.concatenate([no_keys, k_inv[GLA_CHUNK:]], axis=0)
        k_end_t = (k.T * jnp.exp(tail[:, :GLA_SUPER])).astype(BF16)
        decay = jnp.exp(tail[:, GLA_SUPER:])
        return q_carry, q_dec.astype(BF16), k_inv_a, k_inv_b, k_end_t, decay

    def intra(c, q_carry, q_dec, k_inv_a, k_inv_b, k_end_t, decay):
        v = v_ref[0, pl.ds(c * GLA_SUPER, GLA_SUPER), :]
        att = (lax.dot_general(q_carry, k_inv_a, _NT, preferred_element_type=F32)
               + lax.dot_general(q_dec, k_inv_b, _NT, preferred_element_type=F32))
        att = jnp.where(causal, att, 0.0).astype(BF16)
        o_intra = jnp.dot(att, v, preferred_element_type=F32)
        upd = jnp.dot(k_end_t, v, preferred_element_type=F32)
        return q_carry, o_intra, upd, decay

    def inter(c, q_carry, o_intra, upd, decay):
        rows = pl.ds(c * GLA_SUPER, GLA_SUPER)
        state = s_ref[...]
        o = o_intra + jnp.dot(q_carry, state.astype(BF16), preferred_element_type=F32)
        s_ref[...] = jnp.concatenate([decay] * (GLA_DV // GLA_SUPER), axis=1) * state + upd
        o = o * _rms_scale(o) * nw_ref[...]
        o = o * (gr_ref[0, rows, :] * gb_ref[0, rows, :]).astype(F32)
        o_ref[0, rows, :] = o.astype(BF16)

    n = tb // GLA_SUPER
    staged = {}
    for step in range(n + 3):
        if step < n:
            staged[step] = log_decay_sums(step)
        if 0 <= step - 1 < n:
            staged[step - 1] = decays(step - 1, *staged[step - 1])
        if 0 <= step - 2 < n:
            staged[step - 2] = intra(step - 2, *staged[step - 2])
        if 0 <= step - 3 < n:
            inter(step - 3, *staged.pop(step - 3))


def _gla(proj3, lr3, w2, bias, norm_w, layer, weights, *, tb):
    b, t, _ = proj3.shape
    nt = t // tb
    steps = b * GLA_HEADS * nt
    cast_in, cast_out, cast_shapes = [], [], []
    for w in weights:
        _, rows, cols = w.shape
        assert rows % (16 * steps) == 0, (rows, steps)
        cast_in.append(pl.BlockSpec((1, rows // steps, cols),
                                    lambda i, h, j: (layer, (i * GLA_HEADS + h) * nt + j, 0)))
        cast_out.append(pl.BlockSpec((rows // steps, cols), lambda i, h, j: ((i * GLA_HEADS + h) * nt + j, 0)))
        cast_shapes.append(jax.ShapeDtypeStruct((rows, cols), BF16))
    return pl.pallas_call(
        functools.partial(_gla_kernel, tb=tb, n_cast=len(weights)),
        grid=(b, GLA_HEADS, nt),
        in_specs=[
            pl.BlockSpec((1, tb, GLA_DK), lambda i, h, j: (i, j, COL_GQ // GLA_DK + h)),
            pl.BlockSpec((1, tb, GLA_DK), lambda i, h, j: (i, j, COL_GK // GLA_DK + h)),
            pl.BlockSpec((1, tb, GLA_DV), lambda i, h, j: (i, j, COL_GV // GLA_DV + h)),
            pl.BlockSpec((1, tb, GLA_DV), lambda i, h, j: (i, j, COL_GR // GLA_DV + h)),
            pl.BlockSpec((1, tb, GLA_DV), lambda i, h, j: (i, j, COL_GATE_B // GLA_DV + h)),
            pl.BlockSpec((1, tb, LANES), lambda i, h, j: (i, j, 0)),
            pl.BlockSpec((LANES, GLA_DK), lambda i, h, j: (0, h)),
            pl.BlockSpec((1, GLA_DK), lambda i, h, j: (0, h)),
            pl.BlockSpec((1, GLA_DV), lambda i, h, j: (0, 0)),
        ] + cast_in,
        out_specs=cast_out + [pl.BlockSpec((1, tb, GLA_DV), lambda i, h, j: (i, j, h))],
        out_shape=cast_shapes + [jax.ShapeDtypeStruct((b, t, D_MODEL), BF16)],
        scratch_shapes=[pltpu.VMEM((GLA_DK, GLA_DV), F32)],
        compiler_params=pltpu.CompilerParams(
            dimension_semantics=("arbitrary", "arbitrary", "arbitrary"), vmem_limit_bytes=VMEM_LIMIT),
        name="gla",
    )(proj3, proj3, proj3, proj3, proj3, lr3, w2, bias, norm_w, *weights)


def _outproj_kernel(a_ref, g_ref, x_ref, w_ref, h_ref):
    merged = a_ref[...] + g_ref[...]
    h_ref[...] = x_ref[...] + jnp.dot(merged, w_ref[...], preferred_element_type=F32)


def _outproj(a2, g2, x2, w_out, *, tm):
    n = x2.shape[0]
    return pl.pallas_call(
        _outproj_kernel,
        grid=(n // tm,),
        in_specs=[
            pl.BlockSpec((tm, D_MODEL), lambda i: (i, 0)),
            pl.BlockSpec((tm, D_MODEL), lambda i: (i, 0)),
            pl.BlockSpec((tm, D_MODEL), lambda i: (i, 0)),
            pl.BlockSpec((D_MODEL, D_MODEL), lambda i: (0, 0)),
        ],
        out_specs=pl.BlockSpec((tm, D_MODEL), lambda i: (i, 0)),
        out_shape=jax.ShapeDtypeStruct((n, D_MODEL), F32),
        compiler_params=pltpu.CompilerParams(
            dimension_semantics=("arbitrary",), vmem_limit_bytes=VMEM_LIMIT),
        name="outproj",
    )(a2, g2, x2, w_out)


def _ffn_kernel(h_ref, nw_ref, wg_ref, wu_ref, wd_ref, fw_ref, o_ref, v_ref, *, final_norm):
    j = pl.program_id(1)
    last = pl.num_programs(1) - 1
    tm, th = h_ref.shape[0], wg_ref.shape[1]
    row_halves = [pl.ds(r * tm // 2, tm // 2) for r in range(2)]

    def hidden_tile(rows):
        v = v_ref[rows, :]
        part = None
        halves = [(c * th // 2, (c + 1) * th // 2) for c in range(2)]
        acts = []
        for lo, hi in halves:
            gate = jnp.dot(v, wg_ref[:, lo:hi], preferred_element_type=F32)
            up = jnp.dot(v, wu_ref[:, lo:hi], preferred_element_type=F32)
            acts.append((gate, up))
        for (lo, hi), (gate, up) in zip(halves, acts):
            act = (gate * _sigmoid(gate) * up).astype(BF16)
            contrib = jnp.dot(act, wd_ref[lo:hi, :], preferred_element_type=F32)
            part = contrib if part is None else part + contrib
        o_ref[rows, :] += part

    def prologue(rows):
        h = h_ref[rows, :]
        v_ref[rows, :] = (h * _rms_scale(h) * nw_ref[...]).astype(BF16)
        o_ref[rows, :] = h

    def epilogue(rows):
        y = o_ref[rows, :]
        o_ref[rows, :] = y * _rms_scale(y) * fw_ref[...]

    @pl.when(j == 0)
    def _():
        for rows in row_halves:
            prologue(rows)
        for rows in row_halves:
            hidden_tile(rows)

    @pl.when((j > 0) & (j < last))
    def _():
        hidden_tile(pl.ds(0, tm))

    @pl.when(j == last)
    def _():
        for rows in row_halves:
            hidden_tile(rows)
            if final_norm:
                epilogue(rows)


def _ffn(h2, norm_w, w_gate, w_up, w_down, final_w, *, tm, th, final_norm):
    n = h2.shape[0]
    assert FFN_HIDDEN // th >= 2
    return pl.pallas_call(
        functools.partial(_ffn_kernel, final_norm=final_norm),
        grid=(n // tm, FFN_HIDDEN // th),
        in_specs=[
            pl.BlockSpec((tm, D_MODEL), lambda i, j: (i, 0)),
            pl.BlockSpec((1, D_MODEL), lambda i, j: (0, 0)),
            pl.BlockSpec((D_MODEL, th), lambda i, j: (0, j)),
            pl.BlockSpec((D_MODEL, th), lambda i, j: (0, j)),
            pl.BlockSpec((th, D_MODEL), lambda i, j: (j, 0)),
            pl.BlockSpec((1, D_MODEL), lambda i, j: (0, 0)),
        ],
        out_specs=pl.BlockSpec((tm, D_MODEL), lambda i, j: (i, 0)),
        out_shape=jax.ShapeDtypeStruct((n, D_MODEL), F32),
        scratch_shapes=[pltpu.VMEM((tm, D_MODEL), BF16)],
        compiler_params=pltpu.CompilerParams(
            dimension_semantics=("arbitrary", "arbitrary"), vmem_limit_bytes=VMEM_LIMIT),
        name="ffn",
    )(h2, norm_w, w_gate, w_up, w_down, final_w)


_IN_WIDTHS = (D_MODEL, KV_WIDTH, KV_WIDTH, GLA_HEADS * GLA_DK, GLA_HEADS * GLA_DK, D_MODEL,
              GLA_GATE_RANK, D_MODEL, D_MODEL, D_MODEL)
(IN_AQ, IN_AK, IN_AV, IN_GQ, IN_GK, IN_GV, IN_LR, IN_GR, IN_GATE_A, IN_GATE_B, D_IN) = (
    sum(_IN_WIDTHS[:i]) for i in range(len(_IN_WIDTHS) + 1))


REGROUP_ROWS = 2 * KV_WIDTH
ATTN_Q_SCALE = HEAD_DIM ** -0.5
GLA_Q_SCALE = GLA_DK ** -0.5
assert ATTN_Q_SCALE == 2.0 ** -3 and GLA_Q_SCALE == 2.0 ** -4


def _regroup_source_row(i):
    row = i * REGROUP_ROWS
    src = row - COL_AQ + IN_AQ
    src = jnp.where(row >= COL_GV, row - COL_GV + IN_GV, src)
    src = jnp.where(row >= COL_GR, row - COL_GR + IN_GR, src)
    src = jnp.where(row >= COL_GQ, row - COL_GQ + IN_GQ, src)
    src = jnp.where(row >= COL_KV, row - COL_KV + IN_AK, src)
    return pl.multiple_of(src, GLA_GATE_RANK)


def _regroup_kernel(wt_ref, o_ref):
    row = pl.program_id(0) * REGROUP_ROWS
    scale = jnp.where(row < COL_AQ + D_MODEL, ATTN_Q_SCALE, 1.0)
    scale = jnp.where((row >= COL_GQ) & (row < COL_GK), GLA_Q_SCALE, scale)
    o_ref[...] = (wt_ref[0] * scale).astype(BF16)


def _regroup_w_in(w_in_t, layer):
    rows = COL_KV + 2 * KV_WIDTH
    return pl.pallas_call(
        _regroup_kernel,
        grid=(rows // REGROUP_ROWS,),
        in_specs=[pl.BlockSpec((pl.Element(1), pl.Element(REGROUP_ROWS), pl.Element(D_MODEL)),
                               lambda i: (layer, _regroup_source_row(i), 0))],
        out_specs=pl.BlockSpec((REGROUP_ROWS, D_MODEL), lambda i: (i, 0)),
        out_shape=jax.ShapeDtypeStruct((rows, D_MODEL), BF16),
        compiler_params=pltpu.CompilerParams(
            dimension_semantics=("arbitrary",), vmem_limit_bytes=VMEM_LIMIT),
        name="regroup_w_in",
    )(w_in_t)


def _tile(n, pref):
    return pref if n % pref == 0 else n


def kernel(x, norm1_w, w_in, gla_gate_w2, gla_gate_b, attn_sinks, gla_norm_w, w_out, norm2_w,
           w_ffn_gate, w_ffn_up, w_ffn_down, final_norm_w):
    b, t, d = x.shape
    n = b * t
    depth = w_in.shape[0]
    h2 = x.reshape(n, d)
    for l in range(depth):
        w_in_t = jnp.swapaxes(w_in, 1, 2)
        proj, kv, lr, wg, wu = _inproj(h2, norm1_w[l].reshape(1, d), _regroup_w_in(w_in_t, l),
                                       w_in_t[l, IN_LR:IN_LR + GLA_GATE_RANK], l, (w_ffn_gate, w_ffn_up),
                                       tm=_tile(n, 1024))
        proj3 = proj.reshape(b, t, PROJ_WIDTH)

        (a,) = _attention(proj3, kv.reshape(b, t, 2 * KV_WIDTH), attn_sinks[l], l, (), tq=_tile(t, 512))

        w2 = jnp.pad(gla_gate_w2[l], ((0, LANES - GLA_GATE_RANK), (0, 0))).astype(BF16)
        wo, wd, g = _gla(proj3, lr.reshape(b, t, LANES), w2, gla_gate_b[l].reshape(1, -1),
                         gla_norm_w[l].reshape(1, GLA_DV), l, (w_out, w_ffn_down), tb=_tile(t, 2048))

        h2 = _outproj(a.reshape(n, d), g.reshape(n, d), h2, wo, tm=_tile(n, 512))
        h2 = _ffn(h2, norm2_w[l].reshape(1, d), wg, wu, wd, final_norm_w.reshape(1, d),
                  tm=_tile(n, 1024), th=512, final_norm=l == depth - 1)
    return h2.reshape(b, t, d)
```

```python
import functools

import jax
import jax.numpy as jnp
from jax import lax
from jax.experimental import pallas as pl
from jax.experimental.pallas import tpu as pltpu

F32 = jnp.float32
BF16 = jnp.bfloat16

D_MODEL = 2048
HEAD_DIM = 64
N_Q_HEADS = D_MODEL // HEAD_DIM
N_KV_HEADS = 4
GQA_GROUP = N_Q_HEADS // N_KV_HEADS
KV_WIDTH = N_KV_HEADS * HEAD_DIM
WINDOW = 128

GLA_HEADS = 4
GLA_DK = (D_MODEL // 2) // GLA_HEADS
GLA_DV = D_MODEL // GLA_HEADS
GLA_GATE_RANK = 16
GLA_GATE_NORMALIZER = 16.0
GLA_CHUNK = 64

FFN_HIDDEN = ((8 * D_MODEL // 3 + 255) // 256) * 256
RMS_EPS = 1e-6
MASK_VALUE = -1e30

LANES = 128

COL_AQ = 0
COL_GV = COL_AQ + D_MODEL
COL_GR = COL_GV + D_MODEL
COL_GATE_A = COL_GR + D_MODEL
COL_GATE_B = COL_GATE_A + D_MODEL
COL_GQ = COL_GATE_B + D_MODEL
COL_GK = COL_GQ + GLA_HEADS * GLA_DK
PROJ_WIDTH = COL_GK + GLA_HEADS * GLA_DK
PROJ_TILE = 2048
COL_KV = PROJ_WIDTH

VMEM_LIMIT = 56 * 1024 * 1024
INPROJ_VMEM_LIMIT = 60 * 1024 * 1024

_NT = (((1,), (1,)), ((), ()))
_TN = (((0,), (0,)), ((), ()))


def _rms_scale(x):
    return lax.rsqrt(jnp.mean(x * x, axis=-1, keepdims=True) + RMS_EPS)


def _sigmoid(x):
    return 0.5 * jnp.tanh(0.5 * x) + 0.5


INPROJ_CAST_STEPS = 4


def _inproj_kernel(x_ref, nw_ref, w_ref, wkv_ref, wlr_ref, *rest, n_cast):
    cast_in, (proj_ref, kv_ref, lr_ref) = rest[:n_cast], rest[n_cast:n_cast + 3]
    cast_out, u_ref = rest[n_cast + 3:2 * n_cast + 3], rest[2 * n_cast + 3]
    j = pl.program_id(1)

    @pl.when(j < INPROJ_CAST_STEPS)
    def _():
        for src, dst in zip(cast_in, cast_out):
            dst[...] = src[0].astype(BF16)

    @pl.when(j == 0)
    def _():
        x = x_ref[...]
        u = (x * _rms_scale(x) * nw_ref[...]).astype(BF16)
        u_ref[...] = u
        kv_ref[...] = lax.dot_general(u, wkv_ref[...], _NT, preferred_element_type=F32).astype(BF16)
        w_lr = wlr_ref[...].astype(BF16)
        w_lr = jnp.concatenate([w_lr, jnp.zeros((LANES - GLA_GATE_RANK, D_MODEL), BF16)], axis=0)
        lr_ref[...] = lax.dot_general(u, w_lr, _NT, preferred_element_type=F32).astype(BF16)

    def tile():
        return lax.dot_general(u_ref[...], w_ref[...], _NT, preferred_element_type=F32)

    is_silu = (j >= COL_GR // PROJ_TILE) & (j < COL_GATE_A // PROJ_TILE)
    is_sigmoid = (j >= COL_GATE_A // PROJ_TILE) & (j < COL_GQ // PROJ_TILE)

    @pl.when(is_silu)
    def _():
        y = tile()
        proj_ref[...] = (y * _sigmoid(y)).astype(BF16)

    @pl.when(is_sigmoid)
    def _():
        proj_ref[...] = _sigmoid(tile()).astype(BF16)

    @pl.when(jnp.logical_not(is_silu | is_sigmoid))
    def _():
        proj_ref[...] = tile().astype(BF16)


def _inproj(x2, norm_w, w_t, w_lr_t, layer, weights, *, tm):
    n = x2.shape[0]
    assert PROJ_WIDTH // PROJ_TILE >= INPROJ_CAST_STEPS
    steps = (n // tm) * INPROJ_CAST_STEPS

    def cast_block(i, j):
        return i * INPROJ_CAST_STEPS + jnp.minimum(j, INPROJ_CAST_STEPS - 1)

    cast_in, cast_out, cast_shapes = [], [], []
    for w in weights:
        _, rows, cols = w.shape
        assert rows % (16 * steps) == 0, (rows, steps)
        cast_in.append(pl.BlockSpec((1, rows // steps, cols), lambda i, j: (layer, cast_block(i, j), 0)))
        cast_out.append(pl.BlockSpec((rows // steps, cols), lambda i, j: (cast_block(i, j), 0)))
        cast_shapes.append(jax.ShapeDtypeStruct((rows, cols), BF16))
    return pl.pallas_call(
        functools.partial(_inproj_kernel, n_cast=len(weights)),
        grid=(n // tm, PROJ_WIDTH // PROJ_TILE),
        in_specs=[
            pl.BlockSpec((tm, D_MODEL), lambda i, j: (i, 0)),
            pl.BlockSpec((1, D_MODEL), lambda i, j: (0, 0)),
            pl.BlockSpec((PROJ_TILE, D_MODEL), lambda i, j: (j, 0)),
            pl.BlockSpec((2 * KV_WIDTH, D_MODEL), lambda i, j: (COL_KV // (2 * KV_WIDTH), 0)),
            pl.BlockSpec((GLA_GATE_RANK, D_MODEL), lambda i, j: (0, 0)),
        ] + cast_in,
        out_specs=[
            pl.BlockSpec((tm, PROJ_TILE), lambda i, j: (i, j)),
            pl.BlockSpec((tm, 2 * KV_WIDTH), lambda i, j: (i, 0)),
            pl.BlockSpec((tm, LANES), lambda i, j: (i, 0)),
        ] + cast_out,
        out_shape=[
            jax.ShapeDtypeStruct((n, PROJ_WIDTH), BF16),
            jax.ShapeDtypeStruct((n, 2 * KV_WIDTH), BF16),
            jax.ShapeDtypeStruct((n, LANES), BF16),
        ] + cast_shapes,
        scratch_shapes=[pltpu.VMEM((tm, D_MODEL), BF16)],
        compiler_params=pltpu.CompilerParams(
            dimension_semantics=("arbitrary", "arbitrary"), vmem_limit_bytes=INPROJ_VMEM_LIMIT),
        name="inproj",
    )(x2, norm_w, w_t, w_t, w_lr_t, *weights)


PAIR = 2 * HEAD_DIM
PAIRS_PER_GROUP = GQA_GROUP // 2
STACK = PAIRS_PER_GROUP * WINDOW


def _attn_blocks(sink_ref, q_ref, kv_ref, ga_ref, o_ref, blocks):
    lo = lax.broadcasted_iota(jnp.int32, (1, PAIR), 1) < HEAD_DIM
    top = lax.broadcasted_iota(jnp.int32, (PAIR, 1), 0) < HEAD_DIM
    ki = lax.broadcasted_iota(jnp.int32, (WINDOW, STACK), 0)
    qi = lax.broadcasted_iota(jnp.int32, (WINDOW, STACK), 1) & (WINDOW - 1)
    upper = ki > qi

    def rows(blk):
        return pl.ds(blocks[blk][0], WINDOW)

    def krows(blk):
        _, k_row0, has_prev = blocks[blk]
        return pl.ds(k_row0, 2 * WINDOW if has_prev else WINDOW)

    def halves(pair_tile, g):
        swapped = pltpu.roll(pair_tile, HEAD_DIM, axis=1)
        own, other = (pair_tile, swapped) if g % 2 == 0 else (swapped, pair_tile)
        zero = jnp.zeros_like(pair_tile)
        return jnp.where(lo, own, zero), jnp.where(lo, zero, other)

    def probs(st, sink_row, has_prev):
        if has_prev:
            folded = jnp.where(upper, st[:WINDOW], st[WINDOW:])
        else:
            folded = jnp.where(upper, MASK_VALUE, st)
        m = jnp.maximum(jnp.max(folded, axis=0, keepdims=True), sink_row)
        p = jnp.exp(folded - m)
        inv = 1.0 / (jnp.sum(p, axis=0, keepdims=True) + jnp.exp(sink_row - m))
        if has_prev:
            p = jnp.concatenate([jnp.where(upper, p, 0.0), jnp.where(upper, 0.0, p)], axis=0)
        return p.astype(BF16), inv

    def scores(blk, g):
        k_lanes = slice((g // 2) * PAIR, (g // 2 + 1) * PAIR)
        k_lo, k_hi = halves(kv_ref[0, krows(blk), k_lanes], g)
        first_pair = g * PAIRS_PER_GROUP
        q = jnp.concatenate(
            [q_ref[0, rows(blk), (first_pair + p) * PAIR:(first_pair + p + 1) * PAIR]
             for p in range(PAIRS_PER_GROUP)], axis=0)
        return (lax.dot_general(k_lo, q, _NT, preferred_element_type=F32),
                lax.dot_general(k_hi, q, _NT, preferred_element_type=F32))

    def weighted_values(blk, g, st_even, st_odd):
        v_lanes = slice(KV_WIDTH + (g // 2) * PAIR, KV_WIDTH + (g // 2 + 1) * PAIR)
        v_lo, v_hi = halves(kv_ref[0, krows(blk), v_lanes], g)
        first_pair = g * PAIRS_PER_GROUP
        sink_even = jnp.concatenate(
            [jnp.full((1, WINDOW), sink_ref[2 * (first_pair + p)], F32) for p in range(PAIRS_PER_GROUP)], axis=1)
        sink_odd = jnp.concatenate(
            [jnp.full((1, WINDOW), sink_ref[2 * (first_pair + p) + 1], F32) for p in range(PAIRS_PER_GROUP)],
            axis=1)
        p_even, inv_even = probs(st_even, sink_even, blocks[blk][2])
        p_odd, inv_odd = probs(st_odd, sink_odd, blocks[blk][2])
        ot = (lax.dot_general(v_lo, p_even, _TN, preferred_element_type=F32)
              + lax.dot_general(v_hi, p_odd, _TN, preferred_element_type=F32))
        return ot, jnp.where(top, inv_even, inv_odd)

    def finish(blk, g, ot, inv):
        o = (ot * inv).T
        first_pair = g * PAIRS_PER_GROUP
        for p in range(PAIRS_PER_GROUP):
            cols = slice((first_pair + p) * PAIR, (first_pair + p + 1) * PAIR)
            gate = ga_ref[0, rows(blk), cols].astype(F32)
            o_ref[0, rows(blk), cols] = (gate * o[p * WINDOW:(p + 1) * WINDOW]).astype(BF16)

    items = [(blk, g) for blk in range(len(blocks)) for g in range(N_KV_HEADS)]
    st_next = scores(*items[0])
    unfinished = None
    for n, item in enumerate(items):
        st_even, st_odd = st_next
        if n + 1 < len(items):
            st_next = scores(*items[n + 1])
        if unfinished is not None:
            finish(*unfinished)
        unfinished = item + weighted_values(*item, st_even, st_odd)
    finish(*unfinished)


def _attn_kernel(sink_ref, q_ref, kv_ref, ga_ref, *rest, tq, n_cast):
    cast_in, cast_out, o_ref = rest[:n_cast], rest[n_cast:2 * n_cast], rest[2 * n_cast]
    for src, dst in zip(cast_in, cast_out):
        dst[...] = src[0].astype(BF16)

    t = pl.program_id(1)
    later = [(blk * WINDOW, pl.multiple_of(t * tq + (blk - 1) * WINDOW, WINDOW), True)
             for blk in range(1, tq // WINDOW)]

    @pl.when(t == 0)
    def _():
        _attn_blocks(sink_ref, q_ref, kv_ref, ga_ref, o_ref, [(0, 0, False)] + later)

    @pl.when(t > 0)
    def _():
        first = (0, pl.multiple_of(t * tq - WINDOW, WINDOW), True)
        _attn_blocks(sink_ref, q_ref, kv_ref, ga_ref, o_ref, [first] + later)


def _attention(proj3, kv3, sinks, layer, weights, *, tq):
    b, t, _ = proj3.shape
    steps = b * (t // tq)
    cast_in, cast_out, cast_shapes = [], [], []
    for w in weights:
        _, rows, cols = w.shape
        assert rows % (16 * steps) == 0, (rows, steps)
        cast_in.append(pl.BlockSpec((1, rows // steps, cols), lambda i, j: (layer, i * (t // tq) + j, 0)))
        cast_out.append(pl.BlockSpec((rows // steps, cols), lambda i, j: (i * (t // tq) + j, 0)))
        cast_shapes.append(jax.ShapeDtypeStruct((rows, cols), BF16))
    return pl.pallas_call(
        functools.partial(_attn_kernel, tq=tq, n_cast=len(weights)),
        grid=(b, t // tq),
        in_specs=[
            pl.BlockSpec(memory_space=pltpu.SMEM),
            pl.BlockSpec((1, tq, D_MODEL), lambda i, j: (i, j, COL_AQ // D_MODEL)),
            pl.BlockSpec((1, t, 2 * KV_WIDTH), lambda i, j: (i, 0, 0)),
            pl.BlockSpec((1, tq, D_MODEL), lambda i, j: (i, j, COL_GATE_A // D_MODEL)),
        ] + cast_in,
        out_specs=cast_out + [pl.BlockSpec((1, tq, D_MODEL), lambda i, j: (i, j, 0))],
        out_shape=cast_shapes + [jax.ShapeDtypeStruct((b, t, D_MODEL), BF16)],
        compiler_params=pltpu.CompilerParams(
            dimension_semantics=("arbitrary", "arbitrary"), vmem_limit_bytes=VMEM_LIMIT),
        name="swa_attention",
    )(sinks, proj3, kv3, proj3, *weights)


def _log_sigmoid(x):
    return jnp.minimum(x, 0.0) - jnp.log(1.0 + jnp.exp(-jnp.abs(x)))


GLA_SUPER = 2 * GLA_CHUNK


def _split_bf16(x):
    hi = x.astype(BF16)
    return hi, (x - hi.astype(F32)).astype(BF16)


def _gla_kernel(q_ref, k_ref, v_ref, gr_ref, gb_ref, lr_ref, w2_ref, b_ref, nw_ref, *rest, tb, n_cast):
    cast_in, cast_out = rest[:n_cast], rest[n_cast:2 * n_cast]
    o_ref, s_ref = rest[2 * n_cast:]
    for src, dst in zip(cast_in, cast_out):
        dst[...] = src[0].astype(BF16)

    @pl.when(pl.program_id(2) == 0)
    def _():
        s_ref[...] = jnp.zeros_like(s_ref)

    lr = lr_ref[0]
    logit = jnp.dot(lr, w2_ref[...], preferred_element_type=F32) + b_ref[...]
    log_a = _log_sigmoid(logit) * (1.0 / GLA_GATE_NORMALIZER)

    ri = lax.broadcasted_iota(jnp.int32, (GLA_SUPER, GLA_SUPER), 0)
    ci = lax.broadcasted_iota(jnp.int32, (GLA_SUPER, GLA_SUPER), 1)
    causal = ri >= ci
    cum = (causal & ((ri // GLA_CHUNK) == (ci // GLA_CHUNK))).astype(BF16)
    cum_twice = jnp.concatenate([cum, cum], axis=1)
    later_and_all = jnp.concatenate([(ri > ci).astype(BF16), jnp.ones((GLA_SUPER, GLA_SUPER), BF16)], axis=1)
    later_and_all_twice = jnp.concatenate([later_and_all, later_and_all], axis=0)
    no_keys = jnp.zeros((GLA_CHUNK, GLA_DK), BF16)

    def log_decay_sums(c):
        la = log_a[c * GLA_SUPER:(c + 1) * GLA_SUPER, :]
        g = jnp.dot(cum_twice, jnp.concatenate(_split_bf16(la), axis=0),
                    preferred_element_type=F32)
        tail = jnp.dot(jnp.concatenate(_split_bf16(la.T), axis=1), later_and_all_twice,
                       preferred_element_type=F32)
        return g, tail

    def decays(c, g, tail):
        rows = pl.ds(c * GLA_SUPER, GLA_SUPER)
        q = q_ref[0, rows, :].astype(F32)
        k = k_ref[0, rows, :].astype(F32)
        q_dec = q * jnp.exp(g)
        carry = jnp.exp(g[GLA_CHUNK - 1:GLA_CHUNK, :])
        q_carry = jnp.concatenate([q_dec[:GLA_CHUNK], q_dec[GLA_CHUNK:] * carry], axis=0).astype(BF16)
        k_inv = (k * jnp.exp(-g)).astype(BF16)
        k_inv_a = jnp.concatenate([k_inv[:GLA_CHUNK], no_keys], axis=0)
        k_inv_b = jnp.concatenate([no_keys, k_inv[GLA_CHUNK:]], axis=0)
        k_end_t = (k.T * jnp.exp(tail[:, :GLA_SUPER])).astype(BF16)
        decay = jnp.exp(tail[:, GLA_SUPER:])
        return q_carry, q_dec.astype(BF16), k_inv_a, k_inv_b, k_end_t, decay

    def intra(c, q_carry, q_dec, k_inv_a, k_inv_b, k_end_t, decay):
        v = v_ref[0, pl.ds(c * GLA_SUPER, GLA_SUPER), :]
        att = (lax.dot_general(q_carry, k_inv_a, _NT, preferred_element_type=F32)
               + lax.dot_general(q_dec, k_inv_b, _NT, preferred_element_type=F32))
        att = jnp.where(causal, att, 0.0).astype(BF16)
        o_intra = jnp.dot(att, v, preferred_element_type=F32)
        upd = jnp.dot(k_end_t, v, preferred_element_type=F32)
        return q_carry, o_intra, upd, decay

    def inter(c, q_carry, o_intra, upd, decay):
        rows = pl.ds(c * GLA_SUPER, GLA_SUPER)
        state = s_ref[...]
        o = o_intra + jnp.dot(q_carry, state.astype(BF16), preferred_element_type=F32)
        s_ref[...] = jnp.concatenate([decay] * (GLA_DV // GLA_SUPER), axis=1) * state + upd
        o = o * _rms_scale(o) * nw_ref[...]
        o = o * (gr_ref[0, rows, :] * gb_ref[0, rows, :]).astype(F32)
        o_ref[0, rows, :] = o.astype(BF16)

    n = tb // GLA_SUPER
    staged = {}
    for step in range(n + 3):
        if step < n:
            staged[step] = log_decay_sums(step)
        if 0 <= step - 1 < n:
            staged[step - 1] = decays(step - 1, *staged[step - 1])
        if 0 <= step - 2 < n:
            staged[step - 2] = intra(step - 2, *staged[step - 2])
        if 0 <= step - 3 < n:
            inter(step - 3, *staged.pop(step - 3))


def _gla(proj3, lr3, w2, bias, norm_w, layer, weights, *, tb):
    b, t, _ = proj3.shape
    nt = t // tb
    steps = b * GLA_HEADS * nt
    cast_in, cast_out, cast_shapes = [], [], []
    for w in weights:
        _, rows, cols = w.shape
        assert rows % (16 * steps) == 0, (rows, steps)
        cast_in.append(pl.BlockSpec((1, rows // steps, cols),
                                    lambda i, h, j: (layer, (i * GLA_HEADS + h) * nt + j, 0)))
        cast_out.append(pl.BlockSpec((rows // steps, cols), lambda i, h, j: ((i * GLA_HEADS + h) * nt + j, 0)))
        cast_shapes.append(jax.ShapeDtypeStruct((rows, cols), BF16))
    return pl.pallas_call(
        functools.partial(_gla_kernel, tb=tb, n_cast=len(weights)),
        grid=(b, GLA_HEADS, nt),
        in_specs=[
            pl.BlockSpec((1, tb, GLA_DK), lambda i, h, j: (i, j, COL_GQ // GLA_DK + h)),
            pl.BlockSpec((1, tb, GLA_DK), lambda i, h, j: (i, j, COL_GK // GLA_DK + h)),
            pl.BlockSpec((1, tb, GLA_DV), lambda i, h, j: (i, j, COL_GV // GLA_DV + h)),
            pl.BlockSpec((1, tb, GLA_DV), lambda i, h, j: (i, j, COL_GR // GLA_DV + h)),
            pl.BlockSpec((1, tb, GLA_DV), lambda i, h, j: (i, j, COL_GATE_B // GLA_DV + h)),
            pl.BlockSpec((1, tb, LANES), lambda i, h, j: (i, j, 0)),
            pl.BlockSpec((LANES, GLA_DK), lambda i, h, j: (0, h)),
            pl.BlockSpec((1, GLA_DK), lambda i, h, j: (0, h)),
            pl.BlockSpec((1, GLA_DV), lambda i, h, j: (0, 0)),
        ] + cast_in,
        out_specs=cast_out + [pl.BlockSpec((1, tb, GLA_DV), lambda i, h, j: (i, j, h))],
        out_shape=cast_shapes + [jax.ShapeDtypeStruct((b, t, D_MODEL), BF16)],
        scratch_shapes=[pltpu.VMEM((GLA_DK, GLA_DV), F32)],
        compiler_params=pltpu.CompilerParams(
            dimension_semantics=("arbitrary", "arbitrary", "arbitrary"), vmem_limit_bytes=VMEM_LIMIT),
        name="gla",
    )(proj3, proj3, proj3, proj3, proj3, lr3, w2, bias, norm_w, *weights)


def _outproj_kernel(a_ref, g_ref, x_ref, w_ref, h_ref):
    merged = a_ref[...] + g_ref[...]
    h_ref[...] = x_ref[...] + jnp.dot(merged, w_ref[...], preferred_element_type=F32)


def _outproj(a2, g2, x2, w_out, *, tm):
    n = x2.shape[0]
    return pl.pallas_call(
        _outproj_kernel,
        grid=(n // tm,),
        in_specs=[
            pl.BlockSpec((tm, D_MODEL), lambda i: (i, 0)),
            pl.BlockSpec((tm, D_MODEL), lambda i: (i, 0)),
            pl.BlockSpec((tm, D_MODEL), lambda i: (i, 0)),
            pl.BlockSpec((D_MODEL, D_MODEL), lambda i: (0, 0)),
        ],
        out_specs=pl.BlockSpec((tm, D_MODEL), lambda i: (i, 0)),
        out_shape=jax.ShapeDtypeStruct((n, D_MODEL), F32),
        compiler_params=pltpu.CompilerParams(
            dimension_semantics=("arbitrary",), vmem_limit_bytes=VMEM_LIMIT),
        name="outproj",
    )(a2, g2, x2, w_out)


def _ffn_kernel(h_ref, nw_ref, wg_ref, wu_ref, wd_ref, fw_ref, o_ref, v_ref, *, final_norm):
    j = pl.program_id(1)
    last = pl.num_programs(1) - 1
    tm, th = h_ref.shape[0], wg_ref.shape[1]
    row_halves = [pl.ds(r * tm // 2, tm // 2) for r in range(2)]

    def hidden_tile(rows):
        v = v_ref[rows, :]
        part = None
        halves = [(c * th // 2, (c + 1) * th // 2) for c in range(2)]
        acts = []
        for lo, hi in halves:
            gate = jnp.dot(v, wg_ref[:, lo:hi], preferred_element_type=F32)
            up = jnp.dot(v, wu_ref[:, lo:hi], preferred_element_type=F32)
            acts.append((gate, up))
        for (lo, hi), (gate, up) in zip(halves, acts):
            act = (gate * _sigmoid(gate) * up).astype(BF16)
            contrib = jnp.dot(act, wd_ref[lo:hi, :], preferred_element_type=F32)
            part = contrib if part is None else part + contrib
        o_ref[rows, :] += part

    def prologue(rows):
        h = h_ref[rows, :]
        v_ref[rows, :] = (h * _rms_scale(h) * nw_ref[...]).astype(BF16)
        o_ref[rows, :] = h

    def epilogue(rows):
        y = o_ref[rows, :]
        o_ref[rows, :] = y * _rms_scale(y) * fw_ref[...]

    @pl.when(j == 0)
    def _():
        for rows in row_halves:
            prologue(rows)
        for rows in row_halves:
            hidden_tile(rows)

    @pl.when((j > 0) & (j < last))
    def _():
        hidden_tile(pl.ds(0, tm))

    @pl.when(j == last)
    def _():
        for rows in row_halves:
            hidden_tile(rows)
            if final_norm:
                epilogue(rows)


def _ffn(h2, norm_w, w_gate, w_up, w_down, final_w, *, tm, th, final_norm):
    n = h2.shape[0]
    assert FFN_HIDDEN // th >= 2
    return pl.pallas_call(
        functools.partial(_ffn_kernel, final_norm=final_norm),
        grid=(n // tm, FFN_HIDDEN // th),
        in_specs=[
            pl.BlockSpec((tm, D_MODEL), lambda i, j: (i, 0)),
            pl.BlockSpec((1, D_MODEL), lambda i, j: (0, 0)),
            pl.BlockSpec((D_MODEL, th), lambda i, j: (0, j)),
            pl.BlockSpec((D_MODEL, th), lambda i, j: (0, j)),
            pl.BlockSpec((th, D_MODEL), lambda i, j: (j, 0)),
            pl.BlockSpec((1, D_MODEL), lambda i, j: (0, 0)),
        ],
        out_specs=pl.BlockSpec((tm, D_MODEL), lambda i, j: (i, 0)),
        out_shape=jax.ShapeDtypeStruct((n, D_MODEL), F32),
        scratch_shapes=[pltpu.VMEM((tm, D_MODEL), BF16)],
        compiler_params=pltpu.CompilerParams(
            dimension_semantics=("arbitrary", "arbitrary"), vmem_limit_bytes=VMEM_LIMIT),
        name="ffn",
    )(h2, norm_w, w_gate, w_up, w_down, final_w)


_IN_WIDTHS = (D_MODEL, KV_WIDTH, KV_WIDTH, GLA_HEADS * GLA_DK, GLA_HEADS * GLA_DK, D_MODEL,
              GLA_GATE_RANK, D_MODEL, D_MODEL, D_MODEL)
(IN_AQ, IN_AK, IN_AV, IN_GQ, IN_GK, IN_GV, IN_LR, IN_GR, IN_GATE_A, IN_GATE_B, D_IN) = (
    sum(_IN_WIDTHS[:i]) for i in range(len(_IN_WIDTHS) + 1))


REGROUP_ROWS = 2 * KV_WIDTH
ATTN_Q_SCALE = HEAD_DIM ** -0.5
GLA_Q_SCALE = GLA_DK ** -0.5
assert ATTN_Q_SCALE == 2.0 ** -3 and GLA_Q_SCALE == 2.0 ** -4


def _regroup_source_row(i):
    row = i * REGROUP_ROWS
    src = row - COL_AQ + IN_AQ
    src = jnp.where(row >= COL_GV, row - COL_GV + IN_GV, src)
    src = jnp.where(row >= COL_GR, row - COL_GR + IN_GR, src)
    src = jnp.where(row >= COL_GQ, row - COL_GQ + IN_GQ, src)
    src = jnp.where(row >= COL_KV, row - COL_KV + IN_AK, src)
    return pl.multiple_of(src, GLA_GATE_RANK)


def _regroup_kernel(wt_ref, o_ref):
    row = pl.program_id(0) * REGROUP_ROWS
    scale = jnp.where(row < COL_AQ + D_MODEL, ATTN_Q_SCALE, 1.0)
    scale = jnp.where((row >= COL_GQ) & (row < COL_GK), GLA_Q_SCALE, scale)
    o_ref[...] = (wt_ref[0] * scale).astype(BF16)


def _regroup_w_in(w_in_t, layer):
    rows = COL_KV + 2 * KV_WIDTH
    return pl.pallas_call(
        _regroup_kernel,
        grid=(rows // REGROUP_ROWS,),
        in_specs=[pl.BlockSpec((pl.Element(1), pl.Element(REGROUP_ROWS), pl.Element(D_MODEL)),
                               lambda i: (layer, _regroup_source_row(i), 0))],
        out_specs=pl.BlockSpec((REGROUP_ROWS, D_MODEL), lambda i: (i, 0)),
        out_shape=jax.ShapeDtypeStruct((rows, D_MODEL), BF16),
        compiler_params=pltpu.CompilerParams(
            dimension_semantics=("arbitrary",), vmem_limit_bytes=VMEM_LIMIT),
        name="regroup_w_in",
    )(w_in_t)


def _tile(n, pref):
    return pref if n % pref == 0 else n


def kernel(x, norm1_w, w_in, gla_gate_w2, gla_gate_b, attn_sinks, gla_norm_w, w_out, norm2_w,
           w_ffn_gate, w_ffn_up, w_ffn_down, final_norm_w):
    b, t, d = x.shape
    n = b * t
    depth = w_in.shape[0]
    h2 = x.reshape(n, d)
    for l in range(depth):
        w_in_t = jnp.swapaxes(w_in, 1, 2)
        proj, kv, lr, wg, wu, wo = _inproj(h2, norm1_w[l].reshape(1, d), _regroup_w_in(w_in_t, l),
                                           w_in_t[l, IN_LR:IN_LR + GLA_GATE_RANK], l,
                                           (w_ffn_gate, w_ffn_up, w_out), tm=_tile(n, 1024))
        proj3 = proj.reshape(b, t, PROJ_WIDTH)

        (a,) = _attention(proj3, kv.reshape(b, t, 2 * KV_WIDTH), attn_sinks[l], l, (), tq=_tile(t, 512))

        w2 = jnp.pad(gla_gate_w2[l], ((0, LANES - GLA_GATE_RANK), (0, 0))).astype(BF16)
        wd, g = _gla(proj3, lr.reshape(b, t, LANES), w2, gla_gate_b[l].reshape(1, -1),
                     gla_norm_w[l].reshape(1, GLA_DV), l, (w_ffn_down,), tb=_tile(t, 2048))

        h2 = _outproj(a.reshape(n, d), g.reshape(n, d), h2, wo, tm=_tile(n, 512))
        h2 = _ffn(h2, norm2_w[l].reshape(1, d), wg, wu, wd, final_norm_w.reshape(1, d),
                  tm=_tile(n, 1024), th=512, final_norm=l == depth - 1)
    return h2.reshape(b, t, d)
```
